```python
import jax, jax.numpy as jnp
from jax import lax
import numpy as np

D_MODEL = 1024
BATCH = 32
SEQ = 2048
DEPTH = 2
DEC_BATCH = 8
DEC_SEQ = 8192
PAST_LEN = 128

N_META = 16
GRID_W = 64
NA_HEADS = 16
NA_HEAD_DIM = 64
NA_WIDTH = NA_HEADS * NA_HEAD_DIM
NA_KH_MAX = 8
NA_KW = 16
NA_QBLK = 16
NA_KBLK = NA_QBLK + NA_KW
LRU_WIDTH = 1024
LRU_BLOCKS = 16
LRU_BLOCK_DIM = LRU_WIDTH // LRU_BLOCKS
CONV_W = 4
CONV_LEFT = 2
LRU_C = 8.0
D_FF = 2816
RMS_EPS = 1e-6
MASK_VALUE = -1e30
IN_WIDTH = 3 * NA_WIDTH + 2 * LRU_WIDTH + 2 * D_MODEL
IN_SPLITS = [NA_WIDTH, 2 * NA_WIDTH, 3 * NA_WIDTH, 3 * NA_WIDTH + LRU_WIDTH,
             3 * NA_WIDTH + 2 * LRU_WIDTH, 3 * NA_WIDTH + 2 * LRU_WIDTH + D_MODEL]

kernel_name = "hybrid_natten_rglru_macaron_encoder"


def rmsnorm(x, g):
    xf = x.astype(jnp.float32)
    inv = lax.rsqrt(jnp.mean(xf * xf, axis=-1, keepdims=True) + RMS_EPS)
    return (xf * inv).astype(x.dtype) * g


def swiglu(x, wg, wu, wd):
    return (jax.nn.silu(x @ wg) * (x @ wu)) @ wd


def neighbourhood_attention(q, k, v, rel_bias):
    B, L, H, dh = q.shape
    T = L - N_META
    rows = T // GRID_W
    kh = min(NA_KH_MAX, rows)
    q = q * (dh ** -0.5)
    qm, km, vm = q[:, :N_META], k[:, :N_META], v[:, :N_META]
    qg = q[:, N_META:].reshape(B, rows, GRID_W, H, dh)
    kg = k[:, N_META:].reshape(B, rows, GRID_W, H, dh)
    vg = v[:, N_META:].reshape(B, rows, GRID_W, H, dh)

    s_mm = jnp.einsum('bqhd,bkhd->bhqk', qm, km).astype(jnp.float32)
    p_mm = jax.nn.softmax(s_mm, axis=-1).astype(v.dtype)
    out_meta = jnp.einsum('bhqk,bkhd->bqhd', p_mm, vm)

    n_cb = GRID_W // NA_QBLK
    qcol = np.arange(GRID_W).reshape(n_cb, NA_QBLK)
    cs = np.clip(qcol - NA_KW // 2, 0, GRID_W - NA_KW)
    base = np.minimum(cs[:, 0], GRID_W - NA_KBLK)
    kcol = base[:, None] + np.arange(NA_KBLK)
    col_ok = (kcol[:, None, :] >= cs[:, :, None]) & (kcol[:, None, :] < cs[:, :, None] + NA_KW)
    dcol_idx = np.clip(kcol[:, None, :] - qcol[:, :, None] + NA_KW - 1, 0, 2 * NA_KW - 2)
    col_mask = col_ok[:, :, None, :]

    def row_step(r):
        rs = jnp.clip(r - kh // 2, 0, rows - kh)
        k_band = lax.dynamic_slice_in_dim(kg, rs, kh, axis=1)
        v_band = lax.dynamic_slice_in_dim(vg, rs, kh, axis=1)
        k_blk = k_band[:, :, kcol]
        v_blk = v_band[:, :, kcol]
        q_row = lax.dynamic_index_in_dim(qg, r, axis=1, keepdims=False)
        q_row = q_row.reshape(B, n_cb, NA_QBLK, H, dh)
        s_loc = jnp.einsum('bnqhd,binwhd->bhnqiw', q_row, k_blk).astype(jnp.float32)
        drow = rs + jnp.arange(kh) - r + NA_KH_MAX - 1
        bias = rel_bias[:, drow[None, None, :, None], dcol_idx[:, :, None, :]]
        s_loc = jnp.where(col_mask, s_loc + bias[None].astype(jnp.float32), MASK_VALUE)
        s_met = jnp.einsum('bnqhd,bkhd->bhnqk', q_row, km).astype(jnp.float32)
        s_all = jnp.concatenate([s_loc.reshape(B, H, n_cb, NA_QBLK, kh * NA_KBLK), s_met], axis=-1)
        p = jax.nn.softmax(s_all, axis=-1).astype(v.dtype)
        p_loc = p[..., :kh * NA_KBLK].reshape(B, H, n_cb, NA_QBLK, kh, NA_KBLK)
        p_met = p[..., kh * NA_KBLK:]
        o = (jnp.einsum('bhnqiw,binwhd->bnqhd', p_loc, v_blk)
             + jnp.einsum('bhnqk,bkhd->bnqhd', p_met, vm))
        return o.reshape(B, GRID_W, H, dh)

    out_grid = lax.map(row_step, jnp.arange(rows))
    out_grid = jnp.moveaxis(out_grid, 0, 1).reshape(B, T, H, dh)
    return jnp.concatenate([out_meta, out_grid], axis=1)


def centred_dwconv(x, w, b):
    L = x.shape[1]
    xp = jnp.pad(x, ((0, 0), (CONV_LEFT, CONV_W - 1 - CONV_LEFT), (0, 0)))
    y = b
    for j in range(CONV_W):
        y = y + xp[:, j:j + L] * w[j]
    return y


def rg_lru(x, wa, ba, wx, bx, lam, reverse):
    B, L, _ = x.shape
    xb = x.reshape(B, L, LRU_BLOCKS, LRU_BLOCK_DIM)
    r = jax.nn.sigmoid(jnp.einsum('btgi,gij->btgj', xb, wa).reshape(B, L, LRU_WIDTH) + ba)
    i = jax.nn.sigmoid(jnp.einsum('btgi,gij->btgj', xb, wx).reshape(B, L, LRU_WIDTH) + bx)
    log_a = -LRU_C * r.astype(jnp.float32) * jax.nn.softplus(-lam.astype(jnp.float32))
    a = jnp.exp(log_a)
    u = jnp.sqrt(-jnp.expm1(2.0 * log_a)) * (i * x).astype(jnp.float32)

    def combine(left, right):
        a1, b1 = left
        a2, b2 = right
        return a1 * a2, a2 * b1 + b2

    _, h = lax.associative_scan(combine, (a, u), axis=1, reverse=reverse)
    return h.astype(x.dtype)


def hybrid_mixer(h, w_in, rel_bias, conv_w, conv_b, lru_wa, lru_ba, lru_wx, lru_bx, lru_lambda,
                 w_na_proj, w_lru_proj, w_out):
    B, L, _ = h.shape
    z = h @ w_in
    q, k, v, xr, yr, g_na, g_lru = jnp.split(z, IN_SPLITS, axis=-1)
    na = neighbourhood_attention(q.reshape(B, L, NA_HEADS, NA_HEAD_DIM),
                                 k.reshape(B, L, NA_HEADS, NA_HEAD_DIM),
                                 v.reshape(B, L, NA_HEADS, NA_HEAD_DIM), rel_bias)
    na = na.reshape(B, L, NA_WIDTH) @ w_na_proj
    xc = centred_dwconv(xr, conv_w, conv_b)
    hr = (rg_lru(xc, lru_wa[0], lru_ba[0], lru_wx[0], lru_bx[0], lru_lambda[0], False)
          + rg_lru(xc, lru_wa[1], lru_ba[1], lru_wx[1], lru_bx[1], lru_lambda[1], True))
    lru = (jax.nn.gelu(yr) * hr) @ w_lru_proj
    merged = jax.nn.sigmoid(g_na) * na + jax.nn.sigmoid(g_lru) * lru
    return merged @ w_out


def encode(x, meta_tokens, norm_ffn1, ffn1_w_gate, ffn1_w_up, ffn1_w_down, norm_mix, w_in,
           na_rel_bias, conv_w, conv_b, lru_wa, lru_ba, lru_wx, lru_bx, lru_lambda,
           w_na_proj, w_lru_proj, w_out, norm_ffn2, ffn2_w_gate, ffn2_w_up, ffn2_w_down, final_norm):
    B = x.shape[0]
    meta = jnp.broadcast_to(meta_tokens.astype(x.dtype)[None], (B, N_META, x.shape[-1]))
    h = jnp.concatenate([meta, x], axis=1)
    for l in range(DEPTH):
        h = h + 0.5 * swiglu(rmsnorm(h, norm_ffn1[l]), ffn1_w_gate[l], ffn1_w_up[l], ffn1_w_down[l])
        h = h + hybrid_mixer(rmsnorm(h, norm_mix[l]), w_in[l], na_rel_bias[l], conv_w[l], conv_b[l],
                             lru_wa[l], lru_ba[l], lru_wx[l], lru_bx[l], lru_lambda[l],
                             w_na_proj[l], w_lru_proj[l], w_out[l])
        h = h + 0.5 * swiglu(rmsnorm(h, norm_ffn2[l]), ffn2_w_gate[l], ffn2_w_up[l], ffn2_w_down[l])
    h = rmsnorm(h, final_norm)
    return h[:, N_META:]


def setup_inputs(seed: int = 0) -> dict:
    key = jax.random.key(seed)
    ks = jax.random.split(key, 32)
    f32 = jnp.float32

    def nrm(k, shape, scale):
        return jax.random.normal(k, shape, f32) * scale

    u = jax.random.uniform(ks[20], (DEPTH, 2, LRU_WIDTH), f32, 0.9, 0.999)
    s = u ** (1.0 / LRU_C)
    lru_lambda = jnp.log(s) - jnp.log1p(-s)
    return {
        "x_prompt": nrm(ks[0], (BATCH, SEQ, D_MODEL), 1.0),
        "x_sample": nrm(ks[1], (DEC_BATCH, DEC_SEQ, D_MODEL), 1.0),
        "meta_tokens": nrm(ks[2], (N_META, D_MODEL), 1.0),
        "norm_ffn1": 1.0 + nrm(ks[3], (DEPTH, D_MODEL), 0.02),
        "ffn1_w_gate": nrm(ks[4], (DEPTH, D_MODEL, D_FF), D_MODEL ** -0.5),
        "ffn1_w_up": nrm(ks[5], (DEPTH, D_MODEL, D_FF), D_MODEL ** -0.5),
        "ffn1_w_down": nrm(ks[6], (DEPTH, D_FF, D_MODEL), D_FF ** -0.5),
        "norm_mix": 1.0 + nrm(ks[7], (DEPTH, D_MODEL), 0.02),
        "w_in": nrm(ks[8], (DEPTH, D_MODEL, IN_WIDTH), D_MODEL ** -0.5),
        "na_rel_bias": nrm(ks[9], (DEPTH, NA_HEADS, 2 * NA_KH_MAX - 1, 2 * NA_KW - 1), 0.1),
        "conv_w": nrm(ks[10], (DEPTH, CONV_W, LRU_WIDTH), CONV_W ** -0.5),
        "conv_b": nrm(ks[11], (DEPTH, LRU_WIDTH), 0.01),
        "lru_wa": nrm(ks[12], (DEPTH, 2, LRU_BLOCKS, LRU_BLOCK_DIM, LRU_BLOCK_DIM), LRU_BLOCK_DIM ** -0.5),
        "lru_ba": nrm(ks[13], (DEPTH, 2, LRU_WIDTH), 0.01),
        "lru_wx": nrm(ks[14], (DEPTH, 2, LRU_BLOCKS, LRU_BLOCK_DIM, LRU_BLOCK_DIM), LRU_BLOCK_DIM ** -0.5),
        "lru_bx": nrm(ks[15], (DEPTH, 2, LRU_WIDTH), 0.01),
        "lru_lambda": lru_lambda,
        "w_na_proj": nrm(ks[16], (DEPTH, NA_WIDTH, D_MODEL), NA_WIDTH ** -0.5),
        "w_lru_proj": nrm(ks[17], (DEPTH, LRU_WIDTH, D_MODEL), LRU_WIDTH ** -0.5),
        "w_out": nrm(ks[18], (DEPTH, D_MODEL, D_MODEL), D_MODEL ** -0.5),
        "norm_ffn2": 1.0 + nrm(ks[19], (DEPTH, D_MODEL), 0.02),
        "ffn2_w_gate": nrm(ks[21], (DEPTH, D_MODEL, D_FF), D_MODEL ** -0.5),
        "ffn2_w_up": nrm(ks[22], (DEPTH, D_MODEL, D_FF), D_MODEL ** -0.5),
        "ffn2_w_down": nrm(ks[23], (DEPTH, D_FF, D_MODEL), D_FF ** -0.5),
        "final_norm": 1.0 + nrm(ks[24], (D_MODEL,), 0.02),
    }


def reference(x_prompt, x_sample, meta_tokens, norm_ffn1, ffn1_w_gate, ffn1_w_up, ffn1_w_down,
              norm_mix, w_in, na_rel_bias, conv_w, conv_b, lru_wa, lru_ba, lru_wx, lru_bx,
              lru_lambda, w_na_proj, w_lru_proj, w_out, norm_ffn2, ffn2_w_gate, ffn2_w_up,
              ffn2_w_down, final_norm):
    y_prompt = encode(x_prompt, meta_tokens, norm_ffn1, ffn1_w_gate, ffn1_w_up, ffn1_w_down, norm_mix,
                      w_in, na_rel_bias, conv_w, conv_b, lru_wa, lru_ba, lru_wx, lru_bx, lru_lambda,
                      w_na_proj, w_lru_proj, w_out, norm_ffn2, ffn2_w_gate, ffn2_w_up, ffn2_w_down,
                      final_norm)
    y_sample = encode(x_sample, meta_tokens, norm_ffn1, ffn1_w_gate, ffn1_w_up, ffn1_w_down, norm_mix,
                      w_in, na_rel_bias, conv_w, conv_b, lru_wa, lru_ba, lru_wx, lru_bx, lru_lambda,
                      w_na_proj, w_lru_proj, w_out, norm_ffn2, ffn2_w_gate, ffn2_w_up, ffn2_w_down,
                      final_norm)
    return (y_prompt, y_sample)
```

```python
import functools

import numpy as np
import jax
import jax.numpy as jnp
from jax import lax
from jax.experimental import pallas as pl
from jax.experimental.pallas import tpu as pltpu

F32 = jnp.float32
BF16 = jnp.bfloat16

D_MODEL = 1024
N_META = 16
GRID_W = 64
NA_HEADS = 16
NA_HEAD_DIM = 64
NA_KH = 8
NA_KW = 16
LRU_BLOCK_DIM = 64
CONV_W = 4
CONV_LEFT = 2
LRU_C = 8.0
RMS_EPS = 1e-6
MASK_VALUE = -1e30

LANES = 128
SUBLANES = 8
N_PAIRS = NA_HEADS // 2
BAND = NA_KH * GRID_W
LRU_CHUNK = 256
LRU_SLABS = LRU_CHUNK // LANES
SCAN_GROUP = SUBLANES
VMEM_LIMIT = 56 * 1024 * 1024
MAX_ROW_TILE = 512


def _row_tile(n):
    for t in range(MAX_ROW_TILE, 0, -SUBLANES):
        if n % t == 0:
            return t
    raise ValueError(f"no row tile for {n} rows")


def _const_spec(shape):
    zeros = (0,) * len(shape)
    return pl.BlockSpec(shape, lambda *_: zeros, pipeline_mode=pl.Buffered(1))


def _params(sem):
    return pltpu.CompilerParams(dimension_semantics=sem, vmem_limit_bytes=VMEM_LIMIT)


def _rms(x, g):
    inv = lax.rsqrt(jnp.mean(x * x, axis=-1, keepdims=True) + RMS_EPS)
    return (x * inv) * g


def _gelu_tanh(x):
    c = np.float32(np.sqrt(2.0 / np.pi))
    return x * (0.5 * (1.0 + jnp.tanh(c * (x + np.float32(0.044715) * (x * x * x)))))


def _ffn_kernel(h_ref, g_ref, wg_ref, wu_ref, wd_ref, o_ref):
    x = h_ref[...]
    xn = _rms(x, g_ref[...]).astype(BF16)
    gate = jnp.dot(xn, wg_ref[...], preferred_element_type=F32)
    up = jnp.dot(xn, wu_ref[...], preferred_element_type=F32)
    act = (gate * jax.nn.sigmoid(gate) * up).astype(BF16)
    down = jnp.dot(act, wd_ref[...], preferred_element_type=F32)
    o_ref[...] = x + 0.5 * down


def _ffn(h, g, wg, wu, wd):
    n, d = h.shape
    dff = wg.shape[1]
    tm = _row_tile(n)
    row = pl.BlockSpec((tm, d), lambda i: (i, 0))
    return pl.pallas_call(
        _ffn_kernel,
        grid=(n // tm,),
        in_specs=[row, _const_spec((1, d)), _const_spec((d, dff)), _const_spec((d, dff)),
                  _const_spec((dff, d))],
        out_specs=row,
        out_shape=jax.ShapeDtypeStruct((n, d), F32),
        input_output_aliases={0: 0},
        compiler_params=_params(("parallel",)),
        name="ffn",
    )(h, g, wg, wu, wd)


def _inproj_kernel(h_ref, g_ref, w_ref, q_ref, k_ref, v_ref, xr_ref, yr_ref, gna_ref, glru_ref):
    d = h_ref.shape[1]
    xn = _rms(h_ref[...], g_ref[...]).astype(BF16)

    def proj(j):
        return jnp.dot(xn, w_ref[:, j * d:(j + 1) * d], preferred_element_type=F32)

    zq = (proj(0) * np.float32(NA_HEAD_DIM ** -0.5)).astype(BF16)
    zk = proj(1).astype(BF16)
    zv = proj(2).astype(BF16)
    for p in range(N_PAIRS):
        sl = slice(p * LANES, (p + 1) * LANES)
        q_ref[p] = zq[:, sl]
        k_ref[p] = zk[:, sl]
        v_ref[p] = zv[:, sl]
    xr_ref[...] = proj(3)
    yr_ref[...] = proj(4)
    gna_ref[...] = proj(5)
    glru_ref[...] = proj(6)


def _inproj(h, g, w_in):
    n, d = h.shape
    tm = _row_tile(n)
    row = pl.BlockSpec((tm, d), lambda i: (i, 0))
    pair = pl.BlockSpec((N_PAIRS, tm, LANES), lambda i: (0, i, 0))
    pair_shape = jax.ShapeDtypeStruct((N_PAIRS, n, LANES), BF16)
    row_shape = jax.ShapeDtypeStruct((n, d), F32)
    return pl.pallas_call(
        _inproj_kernel,
        grid=(n // tm,),
        in_specs=[row, _const_spec((1, d)), _const_spec(w_in.shape)],
        out_specs=[pair, pair, pair, row, row, row, row],
        out_shape=[pair_shape, pair_shape, pair_shape, row_shape, row_shape, row_shape, row_shape],
        compiler_params=_params(("parallel",)),
        name="inproj",
    )(h, g, w_in)


def _attn_bias_table(rel_bias):
    c = np.arange(GRID_W)
    cs = np.clip(c - NA_KW // 2, 0, GRID_W - NA_KW)
    kc = np.arange(GRID_W)
    ok = (kc[None, :] >= cs[:, None]) & (kc[None, :] < cs[:, None] + NA_KW)
    dcol = np.clip(kc[None, :] - c[:, None] + NA_KW - 1, 0, 2 * NA_KW - 2)
    e = np.arange(NA_KH)
    i = np.arange(NA_KH)
    drow = i[None, :] + (NA_KH - 1) - e[:, None]
    tbl = rel_bias.astype(F32)[:, drow[:, None, :, None], dcol[None, :, None, :]]
    tbl = jnp.where(ok[None, None, :, None, :], tbl, np.float32(MASK_VALUE))
    tbl = tbl.reshape(N_PAIRS, 2, NA_KH, GRID_W, BAND)
    return jnp.transpose(tbl, (0, 2, 1, 3, 4)).reshape(N_PAIRS, NA_KH, 2 * GRID_W, BAND)


def _attn_kernel(q_ref, k_ref, v_ref, bias_ref, o_ref):
    seq = q_ref.shape[0]
    rows = (seq - N_META) // GRID_W
    nt = (((1,), (1,)), ((), ()))
    lo = lax.broadcasted_iota(jnp.int32, (1, LANES), 1) < NA_HEAD_DIM

    def stack_heads(x):
        zero = jnp.zeros_like(x)
        return jnp.concatenate([jnp.where(lo, x, zero), jnp.where(lo, zero, x)], axis=0)

    def pick_heads(o, n):
        return jnp.where(lo, o[:n], o[n:])

    km = k_ref[0:N_META, :]
    vm = v_ref[0:N_META, :]

    qm = stack_heads(q_ref[0:N_META, :])
    sm = lax.dot_general(qm, km, nt, preferred_element_type=F32)
    em = jnp.exp(sm - jnp.max(sm, axis=-1, keepdims=True))
    om = jnp.dot(em.astype(BF16), vm, preferred_element_type=F32)
    om = om / jnp.sum(em, axis=-1, keepdims=True)
    o_ref[0:N_META, :] = pick_heads(om, N_META).astype(o_ref.dtype)

    def row_step(r, carry):
        rs = jnp.clip(r - NA_KH // 2, 0, rows - NA_KH)
        q0 = pl.multiple_of(N_META + r * GRID_W, 16)
        k0 = pl.multiple_of(N_META + rs * GRID_W, 16)
        qs = stack_heads(q_ref[pl.ds(q0, GRID_W), :])
        kb = k_ref[pl.ds(k0, BAND), :]
        vb = v_ref[pl.ds(k0, BAND), :]
        s = lax.dot_general(qs, kb, nt, preferred_element_type=F32) + bias_ref[r - rs]
        sm = lax.dot_general(qs, km, nt, preferred_element_type=F32)
        m = jnp.maximum(jnp.max(s, axis=-1, keepdims=True), jnp.max(sm, axis=-1, keepdims=True))
        e = jnp.exp(s - m)
        em = jnp.exp(sm - m)
        den = jnp.sum(e, axis=-1, keepdims=True) + jnp.sum(em, axis=-1, keepdims=True)
        o = (jnp.dot(e.astype(BF16), vb, preferred_element_type=F32)
             + jnp.dot(em.astype(BF16), vm, preferred_element_type=F32))
        o = o / den
        o_ref[pl.ds(q0, GRID_W), :] = pick_heads(o, GRID_W).astype(o_ref.dtype)
        return carry

    lax.fori_loop(0, rows, row_step, 0)


def _attention(q, k, v, bias_tbl, batch, seq):
    shape4 = (N_PAIRS, batch, seq, LANES)
    blk = pl.BlockSpec((None, None, seq, LANES), lambda p, b: (p, b, 0, 0))
    bias_spec = pl.BlockSpec((None, NA_KH, 2 * GRID_W, BAND), lambda p, b: (p, 0, 0, 0))
    out = pl.pallas_call(
        _attn_kernel,
        grid=(N_PAIRS, batch),
        in_specs=[blk, blk, blk, bias_spec],
        out_specs=blk,
        out_shape=jax.ShapeDtypeStruct(shape4, BF16),
        compiler_params=_params(("parallel", "parallel")),
        name="attention",
    )(q.reshape(shape4), k.reshape(shape4), v.reshape(shape4), bias_tbl)
    return out.reshape(N_PAIRS, batch * seq, LANES)


def _scan_time_block(seq):
    best = SUBLANES
    for odd in range(1, seq // SUBLANES + 1, 2):
        tt = SUBLANES * odd
        if tt > 512:
            break
        if seq % tt == 0:
            best = tt
    return best


def _lru_kernel(xl_ref, x_ref, xn_ref, cw_ref, cb_ref, w_ref, ba_ref, bx_ref, lam_ref, o_ref,
                xpad, a_s, u_s, carry, *, nt):
    direction = pl.program_id(0)
    step = pl.program_id(3)
    tt = x_ref.shape[1]
    blk = step + direction * (nt - 1 - 2 * step)
    has_left = blk > 0
    has_right = blk < nt - 1

    @pl.when(step == 0)
    def _():
        carry[...] = jnp.zeros_like(carry)

    lam = lam_ref[...]
    softplus = jnp.maximum(-lam, 0.0) + jnp.log1p(jnp.exp(-jnp.abs(lam)))
    neg_c_sp = np.float32(-LRU_C) * softplus
    cw = cw_ref[...]
    cb = cb_ref[...]
    ba = ba_ref[...]
    bx = bx_ref[...]

    def gates(b, c):
        zeros = jnp.zeros((SUBLANES, LRU_CHUNK), F32)
        xpad[0:SUBLANES, :] = jnp.where(has_left, xl_ref[b], zeros)
        xpad[SUBLANES:SUBLANES + tt, :] = x_ref[b]
        xpad[SUBLANES + tt:, :] = jnp.where(has_right, xn_ref[b], zeros)
        xc = cb
        for j in range(CONV_W):
            xc = xc + xpad[pl.ds(SUBLANES - CONV_LEFT + j, tt), :] * cw[j:j + 1, :]
        pre = jnp.dot(xc.astype(BF16), w_ref[...], preferred_element_type=F32)
        r = jax.nn.sigmoid(pre[:, :LRU_CHUNK] + ba)
        gi = jax.nn.sigmoid(pre[:, LRU_CHUNK:] + bx)
        log_a = neg_c_sp * r
        a = jnp.exp(log_a)
        u = jnp.sqrt(-jnp.tanh(log_a) * (a * a + 1.0)) * (gi * xc)
        row0 = pl.multiple_of(b * tt, SUBLANES)
        for s in range(LRU_SLABS):
            a_s[s, pl.ds(row0, tt), :] = a[:, s * LANES:(s + 1) * LANES]
            u_s[s, pl.ds(row0, tt), :] = u[:, s * LANES:(s + 1) * LANES]
        return c

    lax.fori_loop(0, SCAN_GROUP, gates, 0)

    def scan_step(i, hs):
        t = i + direction * (tt - 1 - 2 * i)
        new = []
        for s in range(LRU_SLABS):
            idx = pl.ds(t, SCAN_GROUP, stride=tt)
            h = a_s[s, idx, :] * hs[s] + u_s[s, idx, :]
            a_s[s, idx, :] = h
            new.append(h)
        return tuple(new)

    hs = lax.fori_loop(0, tt, scan_step, tuple(carry[s] for s in range(LRU_SLABS)), unroll=8)
    for s in range(LRU_SLABS):
        carry[s] = hs[s]

    def write(b, c):
        row0 = pl.multiple_of(b * tt, SUBLANES)
        for s in range(LRU_SLABS):
            o_ref[b, :, s * LANES:(s + 1) * LANES] = a_s[s, pl.ds(row0, tt), :]
        return c

    lax.fori_loop(0, SCAN_GROUP, write, 0)


def _lru(xr, conv_w, conv_b, w_gate, ba, bx, lam, batch, seq):
    c = xr.shape[1]
    tt = _scan_time_block(seq)
    nt = seq // tt
    halo = tt // SUBLANES
    last_halo = seq // SUBLANES - 1
    n_chunks = c // LRU_CHUNK

    def tblk(d, i):
        return i + d * (nt - 1 - 2 * i)

    x3 = xr.reshape(batch, seq, c)
    cur = pl.BlockSpec((SCAN_GROUP, tt, LRU_CHUNK), lambda d, g, ch, i: (g, tblk(d, i), ch))
    left = pl.BlockSpec((SCAN_GROUP, SUBLANES, LRU_CHUNK),
                        lambda d, g, ch, i: (g, jnp.maximum(tblk(d, i) * halo - 1, 0), ch))
    right = pl.BlockSpec((SCAN_GROUP, SUBLANES, LRU_CHUNK),
                         lambda d, g, ch, i: (g, jnp.minimum((tblk(d, i) + 1) * halo, last_halo), ch))
    per_ch = lambda rows: pl.BlockSpec((rows, LRU_CHUNK), lambda d, g, ch, i: (0, ch))
    per_dir = pl.BlockSpec((None, 1, LRU_CHUNK), lambda d, g, ch, i: (d, 0, ch))
    out = pl.pallas_call(
        functools.partial(_lru_kernel, nt=nt),
        grid=(2, batch // SCAN_GROUP, n_chunks, nt),
        in_specs=[left, cur, right, per_ch(CONV_W), per_ch(1),
                  pl.BlockSpec((None, None, LRU_CHUNK, 2 * LRU_CHUNK), lambda d, g, ch, i: (d, ch, 0, 0)),
                  per_dir, per_dir, per_dir],
        out_specs=pl.BlockSpec((None, SCAN_GROUP, tt, LRU_CHUNK),
                               lambda d, g, ch, i: (d, g, tblk(d, i), ch)),
        out_shape=jax.ShapeDtypeStruct((2, batch, seq, c), F32),
        scratch_shapes=[pltpu.VMEM((tt + 2 * SUBLANES, LRU_CHUNK), F32),
                        pltpu.VMEM((LRU_SLABS, SCAN_GROUP * tt, LANES), F32),
                        pltpu.VMEM((LRU_SLABS, SCAN_GROUP * tt, LANES), F32),
                        pltpu.VMEM((LRU_SLABS, SCAN_GROUP, LANES), F32)],
        compiler_params=_params(("arbitrary", "arbitrary", "arbitrary", "arbitrary")),
        name="lru",
    )(x3, x3, x3, conv_w, conv_b.reshape(1, c), w_gate,
      ba.reshape(2, 1, c), bx.reshape(2, 1, c), lam.reshape(2, 1, c))
    return out.reshape(2, batch * seq, c)


def _lru_gate_weights(wa, wx):
    per = LRU_CHUNK // LRU_BLOCK_DIM

    def dense(w):
        n_dir, n_blk, bd, _ = w.shape
        w = w.reshape(n_dir, n_blk // per, per, bd, bd)
        eye = jnp.eye(per, dtype=w.dtype)
        full = jnp.einsum('dcpij,pq->dcpiqj', w, eye)
        return full.reshape(n_dir, n_blk // per, LRU_CHUNK, LRU_CHUNK)

    return jnp.concatenate([dense(wa), dense(wx)], axis=-1).astype(BF16)


def _mixout_kernel(h_ref, na_ref, yr_ref, gna_ref, glru_ref, hd_ref, wn_ref, wl_ref, wo_ref, o_ref):
    na = jnp.concatenate([na_ref[p] for p in range(N_PAIRS)], axis=-1)
    na_p = jnp.dot(na, wn_ref[...], preferred_element_type=F32)
    hr = hd_ref[0] + hd_ref[1]
    lru_in = (_gelu_tanh(yr_ref[...]) * hr).astype(BF16)
    lru_p = jnp.dot(lru_in, wl_ref[...], preferred_element_type=F32)
    merged = jax.nn.sigmoid(gna_ref[...]) * na_p + jax.nn.sigmoid(glru_ref[...]) * lru_p
    o_ref[...] = h_ref[...] + jnp.dot(merged.astype(BF16), wo_ref[...], preferred_element_type=F32)


def _mixout(h, na, yr, gna, glru, hdir, wn, wl, wo):
    n, d = h.shape
    tm = _row_tile(n)
    row = pl.BlockSpec((tm, d), lambda i: (i, 0))
    return pl.pallas_call(
        _mixout_kernel,
        grid=(n // tm,),
        in_specs=[row, pl.BlockSpec((N_PAIRS, tm, LANES), lambda i: (0, i, 0)), row, row, row,
                  pl.BlockSpec((2, tm, d), lambda i: (0, i, 0)),
                  _const_spec((d, d)), _const_spec((d, d)), _const_spec((d, d))],
        out_specs=row,
        out_shape=jax.ShapeDtypeStruct((n, d), F32),
        input_output_aliases={0: 0},
        compiler_params=_params(("parallel",)),
        name="mixout",
    )(h, na, yr, gna, glru, hdir, wn, wl, wo)


def _final_kernel(h_ref, g_ref, o_ref):
    o_ref[...] = _rms(h_ref[...], g_ref[...])


def _final_norm(h, g):
    n, d = h.shape
    tm = _row_tile(n)
    row = pl.BlockSpec((tm, d), lambda i: (i, 0))
    return pl.pallas_call(
        _final_kernel,
        grid=(n // tm,),
        in_specs=[row, _const_spec((1, d))],
        out_specs=row,
        out_shape=jax.ShapeDtypeStruct((n, d), F32),
        compiler_params=_params(("parallel",)),
        name="final_norm",
    )(h, g)


def _prepare_layer(l, p):
    d = D_MODEL
    row = lambda a: a[l].reshape(1, d).astype(F32)
    return dict(
        norm_ffn1=row(p["norm_ffn1"]), norm_mix=row(p["norm_mix"]), norm_ffn2=row(p["norm_ffn2"]),
        ffn1=(p["ffn1_w_gate"][l].astype(BF16), p["ffn1_w_up"][l].astype(BF16), p["ffn1_w_down"][l].astype(BF16)),
        ffn2=(p["ffn2_w_gate"][l].astype(BF16), p["ffn2_w_up"][l].astype(BF16), p["ffn2_w_down"][l].astype(BF16)),
        w_in=p["w_in"][l].astype(BF16),
        bias_tbl=_attn_bias_table(p["na_rel_bias"][l]),
        conv_w=p["conv_w"][l].astype(F32), conv_b=p["conv_b"][l].astype(F32),
        w_gate=_lru_gate_weights(p["lru_wa"][l], p["lru_wx"][l]),
        ba=p["lru_ba"][l].astype(F32), bx=p["lru_bx"][l].astype(F32), lam=p["lru_lambda"][l].astype(F32),
        w_na_proj=p["w_na_proj"][l].astype(BF16), w_lru_proj=p["w_lru_proj"][l].astype(BF16),
        w_out=p["w_out"][l].astype(BF16),
    )


def _encode(x, meta_tokens, layers, final_norm):
    batch, t, d = x.shape
    seq = N_META + t
    meta = jnp.broadcast_to(meta_tokens.astype(x.dtype)[None], (batch, N_META, d))
    h = jnp.concatenate([meta, x], axis=1).reshape(batch * seq, d)
    for lp in layers:
        h = _ffn(h, lp["norm_ffn1"], *lp["ffn1"])
        q, k, v, xr, yr, gna, glru = _inproj(h, lp["norm_mix"], lp["w_in"])
        na = _attention(q, k, v, lp["bias_tbl"], batch, seq)
        hdir = _lru(xr, lp["conv_w"], lp["conv_b"], lp["w_gate"], lp["ba"], lp["bx"], lp["lam"], batch, seq)
        h = _mixout(h, na, yr, gna, glru, hdir, lp["w_na_proj"], lp["w_lru_proj"], lp["w_out"])
        h = _ffn(h, lp["norm_ffn2"], *lp["ffn2"])
    y = _final_norm(h, final_norm.reshape(1, d).astype(F32))
    return y.reshape(batch, seq, d)[:, N_META:]


def kernel(x_prompt, x_sample, meta_tokens, norm_ffn1, ffn1_w_gate, ffn1_w_up, ffn1_w_down, norm_mix, w_in, na_rel_bias, conv_w, conv_b, lru_wa, lru_ba, lru_wx, lru_bx, lru_lambda, w_na_proj, w_lru_proj, w_out, norm_ffn2, ffn2_w_gate, ffn2_w_up, ffn2_w_down, final_norm):
    p = dict(norm_ffn1=norm_ffn1, ffn1_w_gate=ffn1_w_gate, ffn1_w_up=ffn1_w_up, ffn1_w_down=ffn1_w_down,
             norm_mix=norm_mix, w_in=w_in, na_rel_bias=na_rel_bias, conv_w=conv_w, conv_b=conv_b,
             lru_wa=lru_wa, lru_ba=lru_ba, lru_wx=lru_wx, lru_bx=lru_bx, lru_lambda=lru_lambda,
             w_na_proj=w_na_proj, w_lru_proj=w_lru_proj, w_out=w_out, norm_ffn2=norm_ffn2,
             ffn2_w_gate=ffn2_w_gate, ffn2_w_up=ffn2_w_up, ffn2_w_down=ffn2_w_down)
    layers = [_prepare_layer(l, p) for l in range(norm_ffn1.shape[0])]
    y_prompt = _encode(x_prompt, meta_tokens, layers, final_norm)
    y_sample = _encode(x_sample, meta_tokens, layers, final_norm)
    return (y_prompt, y_sample)
```

```python
import functools

import numpy as np
import jax
import jax.numpy as jnp
from jax import lax
from jax.experimental import pallas as pl
from jax.experimental.pallas import tpu as pltpu

F32 = jnp.float32
BF16 = jnp.bfloat16

D_MODEL = 1024
N_META = 16
GRID_W = 64
NA_HEADS = 16
NA_HEAD_DIM = 64
NA_KH = 8
NA_KW = 16
LRU_BLOCK_DIM = 64
CONV_W = 4
CONV_LEFT = 2
LRU_C = 8.0
RMS_EPS = 1e-6
MASK_VALUE = -1e30

LANES = 128
SUBLANES = 8
N_PAIRS = NA_HEADS // 2
ATTN_QROWS = 4
ATTN_KROWS = ATTN_QROWS + NA_KH - 1
ATTN_KLOC = ATTN_KROWS * GRID_W
ATTN_KEXT = 768
LRU_CHUNK = 256
LRU_SLABS = LRU_CHUNK // LANES
SCAN_GROUP = SUBLANES
LRU_TC = 16
LRU_MAX_TT = 704
VMEM_LIMIT = 56 * 1024 * 1024
MAX_ROW_TILE = 512


def _row_tile(n):
    for t in range(MAX_ROW_TILE, 0, -SUBLANES):
        if n % t == 0:
            return t
    raise ValueError(f"no row tile for {n} rows")


def _const_spec(shape):
    zeros = (0,) * len(shape)
    return pl.BlockSpec(shape, lambda *_: zeros, pipeline_mode=pl.Buffered(1))


def _params(sem):
    return pltpu.CompilerParams(dimension_semantics=sem, vmem_limit_bytes=VMEM_LIMIT)


def _rms(x, g):
    inv = lax.rsqrt(jnp.mean(x * x, axis=-1, keepdims=True) + RMS_EPS)
    return (x * inv) * g


def _gelu_tanh(x):
    c = np.float32(np.sqrt(2.0 / np.pi))
    return x * (0.5 * (1.0 + jnp.tanh(c * (x + np.float32(0.044715) * (x * x * x)))))


def _ffn_kernel(h_ref, g_ref, wg_ref, wu_ref, wd_ref, gf_ref, o_ref, *, final):
    x = h_ref[...]
    xn = _rms(x, g_ref[...]).astype(BF16)
    gate = jnp.dot(xn, wg_ref[...], preferred_element_type=F32)
    up = jnp.dot(xn, wu_ref[...], preferred_element_type=F32)
    act = (gate * jax.nn.sigmoid(gate) * up).astype(BF16)
    down = jnp.dot(act, wd_ref[...], preferred_element_type=F32)
    y = x + 0.5 * down
    o_ref[...] = _rms(y, gf_ref[...]) if final else y


def _ffn(h, g, wg, wu, wd, final_g=None):
    n, d = h.shape
    dff = wg.shape[1]
    tm = _row_tile(n)
    row = pl.BlockSpec((tm, d), lambda i: (i, 0))
    final = final_g is not None
    return pl.pallas_call(
        functools.partial(_ffn_kernel, final=final),
        grid=(n // tm,),
        in_specs=[row, _const_spec((1, d)), _const_spec((d, dff)), _const_spec((d, dff)),
                  _const_spec((dff, d)), _const_spec((1, d))],
        out_specs=row,
        out_shape=jax.ShapeDtypeStruct((n, d), F32),
        input_output_aliases={0: 0},
        compiler_params=_params(("parallel",)),
        name="ffn_final" if final else "ffn",
    )(h, g, wg, wu, wd, final_g if final else g)


def _inproj_kernel(h_ref, g_ref, w_ref, q_ref, k_ref, v_ref, xr_ref, yr_ref, gna_ref, glru_ref):
    d = h_ref.shape[1]
    xn = _rms(h_ref[...], g_ref[...]).astype(BF16)

    def proj(j):
        return jnp.dot(xn, w_ref[:, j * d:(j + 1) * d], preferred_element_type=F32)

    zq = (proj(0) * np.float32(NA_HEAD_DIM ** -0.5)).astype(BF16)
    zk = proj(1).astype(BF16)
    zv = proj(2).astype(BF16)
    for p in range(N_PAIRS):
        sl = slice(p * LANES, (p + 1) * LANES)
        q_ref[p] = zq[:, sl]
        k_ref[p] = zk[:, sl]
        v_ref[p] = zv[:, sl]
    xr_ref[...] = proj(3)
    yr_ref[...] = proj(4)
    gna_ref[...] = proj(5)
    glru_ref[...] = proj(6)


def _inproj(h, g, w_in):
    n, d = h.shape
    tm = _row_tile(n)
    row = pl.BlockSpec((tm, d), lambda i: (i, 0))
    pair = pl.BlockSpec((N_PAIRS, tm, LANES), lambda i: (0, i, 0))
    pair_shape = jax.ShapeDtypeStruct((N_PAIRS, n, LANES), BF16)
    row_shape = jax.ShapeDtypeStruct((n, d), F32)
    return pl.pallas_call(
        _inproj_kernel,
        grid=(n // tm,),
        in_specs=[row, _const_spec((1, d)), _const_spec(w_in.shape)],
        out_specs=[pair, pair, pair, row, row, row, row],
        out_shape=[pair_shape, pair_shape, pair_shape, row_shape, row_shape, row_shape, row_shape],
        compiler_params=_params(("parallel",)),
        name="inproj",
    )(h, g, w_in)


def _attn_block_cases(rows):
    nb = rows // ATTN_QROWS
    assert rows % ATTN_QROWS == 0 and nb >= 3 and nb % 2 == 0, rows

    def case(m):
        ks = min(max(ATTN_QROWS * m - NA_KH // 2, 0), rows - ATTN_KROWS)
        out = []
        for q in range(ATTN_QROWS):
            r = ATTN_QROWS * m + q
            rs = min(max(r - NA_KH // 2, 0), rows - NA_KH)
            out.append((rs - ks, r - rs))
        return tuple(out)

    assert all(case(m) == case(1) for m in range(1, nb - 1))
    return case(0), case(1), case(nb - 1)


def _attn_bias_table(rel_bias, rows):
    c = np.arange(GRID_W)
    cs = np.clip(c - NA_KW // 2, 0, GRID_W - NA_KW)
    kc = np.arange(GRID_W)
    ok = (kc[None, :] >= cs[:, None]) & (kc[None, :] < cs[:, None] + NA_KW)
    dcol = kc[None, :] - c[:, None] + NA_KW - 1
    onehot = (dcol[:, :, None] == np.arange(2 * NA_KW - 1)).astype(np.float32)
    cols = jnp.einsum('hdj,ckj->hdck', rel_bias.astype(F32), onehot, precision=lax.Precision.HIGHEST)
    cols = jnp.where(ok[None, None], cols, np.float32(MASK_VALUE))
    mask = lambda n: jnp.full((NA_HEADS, n, GRID_W, GRID_W), np.float32(MASK_VALUE))
    types = []
    for cases in _attn_block_cases(rows):
        per_q = []
        for off, e in cases:
            band = cols[:, NA_KH - 1 - e:2 * NA_KH - 1 - e]
            per_q.append(jnp.concatenate([mask(off), band, mask(ATTN_KROWS - NA_KH - off)], axis=1))
        types.append(jnp.stack(per_q, axis=1))
    tbl = jnp.stack(types, axis=1)
    tbl = jnp.transpose(tbl, (0, 1, 2, 4, 3, 5))
    tbl = tbl.reshape(N_PAIRS, 2, 3, ATTN_QROWS * GRID_W, ATTN_KLOC)
    tbl = jnp.transpose(tbl, (0, 2, 1, 3, 4)).reshape(N_PAIRS, 3, 2 * ATTN_QROWS * GRID_W, ATTN_KLOC)
    lead = tbl.shape[:3]
    return jnp.concatenate([tbl, jnp.zeros(lead + (N_META,), F32),
                            jnp.full(lead + (ATTN_KEXT - ATTN_KLOC - N_META,), np.float32(MASK_VALUE))], axis=-1)


def _attn_kernel(q_ref, k_ref, v_ref, bias_ref, o_ref, kx_ref, vx_ref):
    seq = q_ref.shape[0]
    rows = (seq - N_META) // GRID_W
    nb = rows // ATTN_QROWS
    nq = ATTN_QROWS * GRID_W
    nt = (((1,), (1,)), ((), ()))
    lo = lax.broadcasted_iota(jnp.int32, (1, LANES), 1) < NA_HEAD_DIM

    def stack_heads(x):
        zero = jnp.zeros_like(x)
        return jnp.concatenate([jnp.where(lo, x, zero), jnp.where(lo, zero, x)], axis=0)

    def pick_heads(o, n):
        return jnp.where(lo, o[:n], o[n:])

    km = k_ref[0:N_META, :]
    vm = v_ref[0:N_META, :]

    qm = stack_heads(q_ref[0:N_META, :])
    sm = lax.dot_general(qm, km, nt, preferred_element_type=F32)
    em = jnp.exp(sm - jnp.max(sm, axis=-1, keepdims=True))
    om = jnp.dot(em.astype(BF16), vm, preferred_element_type=F32)
    om = om / jnp.sum(em, axis=-1, keepdims=True)
    o_ref[0:N_META, :] = pick_heads(om, N_META).astype(o_ref.dtype)

    pad = jnp.zeros((ATTN_KEXT - ATTN_KLOC - N_META, LANES), BF16)
    for j in range(2):
        kx_ref[j, ATTN_KLOC:ATTN_KLOC + N_META, :] = km
        vx_ref[j, ATTN_KLOC:ATTN_KLOC + N_META, :] = vm
        kx_ref[j, ATTN_KLOC + N_META:, :] = pad
        vx_ref[j, ATTN_KLOC + N_META:, :] = pad

    def scores(m, j):
        ks = jnp.clip(ATTN_QROWS * m - NA_KH // 2, 0, rows - ATTN_KROWS)
        q0 = pl.multiple_of(N_META + m * nq, 16)
        k0 = pl.multiple_of(N_META + ks * GRID_W, 16)
        kx_ref[j, 0:ATTN_KLOC, :] = k_ref[pl.ds(k0, ATTN_KLOC), :]
        vx_ref[j, 0:ATTN_KLOC, :] = v_ref[pl.ds(k0, ATTN_KLOC), :]
        qs = stack_heads(q_ref[pl.ds(q0, nq), :])
        kind = jnp.where(m == 0, 0, jnp.where(m == nb - 1, 2, 1))
        return lax.dot_general(qs, kx_ref[j], nt, preferred_element_type=F32) + bias_ref[kind]

    def attend(m, j, s):
        q0 = pl.multiple_of(N_META + m * nq, 16)
        e = jnp.exp(s - jnp.max(s, axis=-1, keepdims=True))
        den = jnp.sum(e, axis=-1, keepdims=True)
        o = jnp.dot(e.astype(BF16), vx_ref[j], preferred_element_type=F32) / den
        o_ref[pl.ds(q0, nq), :] = pick_heads(o, nq).astype(o_ref.dtype)

    def two_blocks(i, carry):
        s0 = scores(2 * i, 0)
        s1 = scores(2 * i + 1, 1)
        attend(2 * i, 0, s0)
        attend(2 * i + 1, 1, s1)
        return carry

    lax.fori_loop(0, nb // 2, two_blocks, 0)


def _attention(q, k, v, bias_tbl, batch, seq):
    shape4 = (N_PAIRS, batch, seq, LANES)
    blk = pl.BlockSpec((None, None, seq, LANES), lambda p, b: (p, b, 0, 0))
    bias_spec = pl.BlockSpec((None,) + bias_tbl.shape[1:], lambda p, b: (p, 0, 0, 0))
    out = pl.pallas_call(
        _attn_kernel,
        grid=(N_PAIRS, batch),
        in_specs=[blk, blk, blk, bias_spec],
        out_specs=blk,
        out_shape=jax.ShapeDtypeStruct(shape4, BF16),
        scratch_shapes=[pltpu.VMEM((2, ATTN_KEXT, LANES), BF16), pltpu.VMEM((2, ATTN_KEXT, LANES), BF16)],
        compiler_params=_params(("parallel", "parallel")),
        name="attention",
    )(q.reshape(shape4), k.reshape(shape4), v.reshape(shape4), bias_tbl)
    return out.reshape(N_PAIRS, batch * seq, LANES)


def _scan_time_block(seq):
    best = None
    for tt in range(LRU_TC, LRU_MAX_TT + 1, LRU_TC):
        if seq % tt == 0:
            best = tt
    assert best is not None, seq
    return best


def _lru_kernel(xl_ref, x_ref, xn_ref, cw_ref, cb_ref, w_ref, ba_ref, bx_ref, lam_ref, o_ref,
                x_s, h_s, carry, *, nt, reverse):
    step = pl.program_id(2)
    tt = x_ref.shape[1]
    blk = (nt - 1 - step) if reverse else step
    has_left = blk > 0
    has_right = blk < nt - 1
    halo_rows = SUBLANES * SCAN_GROUP
    n_chunks = tt // LRU_TC
    chunk_rows = LRU_TC * SCAN_GROUP

    @pl.when(step == 0)
    def _():
        carry[...] = jnp.zeros_like(carry)

    zeros = jnp.zeros((SUBLANES, LANES), F32)
    for b in range(SCAN_GROUP):
        for s in range(LRU_SLABS):
            sl = slice(s * LANES, (s + 1) * LANES)
            x_s[s, pl.ds(b, SUBLANES, stride=SCAN_GROUP), :] = jnp.where(has_left, xl_ref[b, :, sl], zeros)
            x_s[s, pl.ds(halo_rows + b, tt, stride=SCAN_GROUP), :] = x_ref[b, :, sl]
            x_s[s, pl.ds(halo_rows + SCAN_GROUP * tt + b, SUBLANES, stride=SCAN_GROUP), :] = (
                jnp.where(has_right, xn_ref[b, :, sl], zeros))

    lam = lam_ref[...]
    softplus = jnp.maximum(-lam, 0.0) + jnp.log1p(jnp.exp(-jnp.abs(lam)))
    neg_c_sp = np.float32(-LRU_C) * softplus
    cw = cw_ref[...]
    cb = cb_ref[...]
    ba = ba_ref[...]
    bx = bx_ref[...]

    def chunk(i, h):
        c = (n_chunks - 1 - i) if reverse else i
        row0 = pl.multiple_of(halo_rows + c * chunk_rows, SUBLANES)
        parts = []
        for s in range(LRU_SLABS):
            sl = slice(s * LANES, (s + 1) * LANES)
            acc = cb[:, sl]
            for j in range(CONV_W):
                tap = x_s[s, pl.ds(row0 + (j - CONV_LEFT) * SCAN_GROUP, chunk_rows), :]
                acc = acc + tap * cw[j:j + 1, sl]
            parts.append(acc)
        xc = jnp.concatenate(parts, axis=-1)
        pre = jnp.dot(xc.astype(BF16), w_ref[...], preferred_element_type=F32)
        r = jax.nn.sigmoid(pre[:, :LRU_CHUNK] + ba)
        gi = jax.nn.sigmoid(pre[:, LRU_CHUNK:] + bx)
        log_a = neg_c_sp * r
        a = jnp.exp(log_a)
        y = -jnp.tanh(log_a) * (a * a + 1.0)
        root = jnp.where(y > 0.0, y * lax.rsqrt(y), 0.0)
        u = root * (gi * xc)
        hs = [None] * LRU_TC
        order = range(LRU_TC - 1, -1, -1) if reverse else range(LRU_TC)
        for k in order:
            rows = slice(k * SCAN_GROUP, (k + 1) * SCAN_GROUP)
            h = a[rows] * h + u[rows]
            hs[k] = h
        hc = jnp.concatenate(hs, axis=0)
        out0 = pl.multiple_of(c * chunk_rows, SUBLANES)
        for s in range(LRU_SLABS):
            h_s[s, pl.ds(out0, chunk_rows), :] = hc[:, s * LANES:(s + 1) * LANES]
        return h

    carry[...] = lax.fori_loop(0, n_chunks, chunk, carry[...], unroll=2)

    for b in range(SCAN_GROUP):
        for s in range(LRU_SLABS):
            o_ref[b, :, s * LANES:(s + 1) * LANES] = h_s[s, pl.ds(b, tt, stride=SCAN_GROUP), :]


def _lru(xr, conv_w, conv_b, w_gate, ba, bx, lam, batch, seq, reverse):
    c = xr.shape[1]
    tt = _scan_time_block(seq)
    nt = seq // tt
    halo = tt // SUBLANES
    last_halo = seq // SUBLANES - 1
    n_chunks = c // LRU_CHUNK
    d = 1 if reverse else 0

    def tblk(i):
        return (nt - 1 - i) if reverse else i

    x3 = xr.reshape(batch, seq, c)
    cur = pl.BlockSpec((SCAN_GROUP, tt, LRU_CHUNK), lambda g, ch, i: (g, tblk(i), ch))
    left = pl.BlockSpec((SCAN_GROUP, SUBLANES, LRU_CHUNK),
                        lambda g, ch, i: (g, jnp.maximum(tblk(i) * halo - 1, 0), ch))
    right = pl.BlockSpec((SCAN_GROUP, SUBLANES, LRU_CHUNK),
                         lambda g, ch, i: (g, jnp.minimum((tblk(i) + 1) * halo, last_halo), ch))
    per_ch = lambda rows: pl.BlockSpec((rows, LRU_CHUNK), lambda g, ch, i: (0, ch))
    per_dir = pl.BlockSpec((None, 1, LRU_CHUNK), lambda g, ch, i: (d, 0, ch))
    out = pl.pallas_call(
        functools.partial(_lru_kernel, nt=nt, reverse=reverse),
        grid=(batch // SCAN_GROUP, n_chunks, nt),
        in_specs=[left, cur, right, per_ch(CONV_W), per_ch(1),
                  pl.BlockSpec((None, None, LRU_CHUNK, 2 * LRU_CHUNK), lambda g, ch, i: (d, ch, 0, 0)),
                  per_dir, per_dir, per_dir],
        out_specs=cur,
        out_shape=jax.ShapeDtypeStruct((batch, seq, c), F32),
        scratch_shapes=[pltpu.VMEM((LRU_SLABS, (tt + 2 * SUBLANES) * SCAN_GROUP, LANES), F32),
                        pltpu.VMEM((LRU_SLABS, tt * SCAN_GROUP, LANES), F32),
                        pltpu.VMEM((SCAN_GROUP, LRU_CHUNK), F32)],
        compiler_params=_params(("parallel", "parallel", "arbitrary")),
        name="lru_bwd" if reverse else "lru_fwd",
    )(x3, x3, x3, conv_w, conv_b.reshape(1, c), w_gate,
      ba.reshape(2, 1, c), bx.reshape(2, 1, c), lam.reshape(2, 1, c))
    return out.reshape(batch * seq, c)


def _lru_gate_weights(wa, wx):
    per = LRU_CHUNK // LRU_BLOCK_DIM

    def dense(w):
        n_dir, n_blk, bd, _ = w.shape
        w = w.reshape(n_dir, n_blk // per, per, bd, bd)
        eye = jnp.eye(per, dtype=w.dtype)
        full = jnp.einsum('dcpij,pq->dcpiqj', w, eye)
        return full.reshape(n_dir, n_blk // per, LRU_CHUNK, LRU_CHUNK)

    return jnp.concatenate([dense(wa), dense(wx)], axis=-1).astype(BF16)


def _mixout_kernel(h_ref, na_ref, yr_ref, gna_ref, glru_ref, hf_ref, hb_ref, wn_ref, wl_ref, wo_ref, o_ref):
    na = jnp.concatenate([na_ref[p] for p in range(N_PAIRS)], axis=-1)
    na_p = jnp.dot(na, wn_ref[...], preferred_element_type=F32)
    hr = hf_ref[...] + hb_ref[...]
    lru_in = (_gelu_tanh(yr_ref[...]) * hr).astype(BF16)
    lru_p = jnp.dot(lru_in, wl_ref[...], preferred_element_type=F32)
    merged = jax.nn.sigmoid(gna_ref[...]) * na_p + jax.nn.sigmoid(glru_ref[...]) * lru_p
    o_ref[...] = h_ref[...] + jnp.dot(merged.astype(BF16), wo_ref[...], preferred_element_type=F32)


def _mixout(h, na, yr, gna, glru, hf, hb, wn, wl, wo):
    n, d = h.shape
    tm = _row_tile(n)
    row = pl.BlockSpec((tm, d), lambda i: (i, 0))
    return pl.pallas_call(
        _mixout_kernel,
        grid=(n // tm,),
        in_specs=[row, pl.BlockSpec((N_PAIRS, tm, LANES), lambda i: (0, i, 0)), row, row, row, row, row,
                  _const_spec((d, d)), _const_spec((d, d)), _const_spec((d, d))],
        out_specs=row,
        out_shape=jax.ShapeDtypeStruct((n, d), F32),
        input_output_aliases={0: 0},
        compiler_params=_params(("parallel",)),
        name="mixout",
    )(h, na, yr, gna, glru, hf, hb, wn, wl, wo)


def _prepare_layer(l, p, rows):
    d = D_MODEL
    row = lambda a: a[l].reshape(1, d).astype(F32)
    return dict(
        norm_ffn1=row(p["norm_ffn1"]), norm_mix=row(p["norm_mix"]), norm_ffn2=row(p["norm_ffn2"]),
        ffn1=(p["ffn1_w_gate"][l].astype(BF16), p["ffn1_w_up"][l].astype(BF16), p["ffn1_w_down"][l].astype(BF16)),
        ffn2=(p["ffn2_w_gate"][l].astype(BF16), p["ffn2_w_up"][l].astype(BF16), p["ffn2_w_down"][l].astype(BF16)),
        w_in=p["w_in"][l].astype(BF16),
        bias_tbl=_attn_bias_table(p["na_rel_bias"][l], rows),
        conv_w=p["conv_w"][l].astype(F32), conv_b=p["conv_b"][l].astype(F32),
        w_gate=_lru_gate_weights(p["lru_wa"][l], p["lru_wx"][l]),
        ba=p["lru_ba"][l].astype(F32), bx=p["lru_bx"][l].astype(F32), lam=p["lru_lambda"][l].astype(F32),
        w_na_proj=p["w_na_proj"][l].astype(BF16), w_lru_proj=p["w_lru_proj"][l].astype(BF16),
        w_out=p["w_out"][l].astype(BF16),
    )


def _encode(x, meta_tokens, layers, final_norm):
    batch, t, d = x.shape
    seq = N_META + t
    meta = jnp.broadcast_to(meta_tokens.astype(x.dtype)[None], (batch, N_META, d))
    h = jnp.concatenate([meta, x], axis=1).reshape(batch * seq, d)
    gf = final_norm.reshape(1, d).astype(F32)
    for li, lp in enumerate(layers):
        h = _ffn(h, lp["norm_ffn1"], *lp["ffn1"])
        q, k, v, xr, yr, gna, glru = _inproj(h, lp["norm_mix"], lp["w_in"])
        na = _attention(q, k, v, lp["bias_tbl"], batch, seq)
        lru_args = (xr, lp["conv_w"], lp["conv_b"], lp["w_gate"], lp["ba"], lp["bx"], lp["lam"], batch, seq)
        hf = _lru(*lru_args, reverse=False)
        hb = _lru(*lru_args, reverse=True)
        h = _mixout(h, na, yr, gna, glru, hf, hb, lp["w_na_proj"], lp["w_lru_proj"], lp["w_out"])
        h = _ffn(h, lp["norm_ffn2"], *lp["ffn2"], final_g=gf if li == len(layers) - 1 else None)
    return h.reshape(batch, seq, d)[:, N_META:]


def kernel(x_prompt, x_sample, meta_tokens, norm_ffn1, ffn1_w_gate, ffn1_w_up, ffn1_w_down, norm_mix, w_in, na_rel_bias, conv_w, conv_b, lru_wa, lru_ba, lru_wx, lru_bx, lru_lambda, w_na_proj, w_lru_proj, w_out, norm_ffn2, ffn2_w_gate, ffn2_w_up, ffn2_w_down, final_norm):
    p = dict(norm_ffn1=norm_ffn1, ffn1_w_gate=ffn1_w_gate, ffn1_w_up=ffn1_w_up, ffn1_w_down=ffn1_w_down,
             norm_mix=norm_mix, w_in=w_in, na_rel_bias=na_rel_bias, conv_w=conv_w, conv_b=conv_b,
             lru_wa=lru_wa, lru_ba=lru_ba, lru_wx=lru_wx, lru_bx=lru_bx, lru_lambda=lru_lambda,
             w_na_proj=w_na_proj, w_lru_proj=w_lru_proj, w_out=w_out, norm_ffn2=norm_ffn2,
             ffn2_w_gate=ffn2_w_gate, ffn2_w_up=ffn2_w_up, ffn2_w_down=ffn2_w_down)
    rows = x_prompt.shape[1] // GRID_W
    assert _attn_block_cases(rows) == _attn_block_cases(x_sample.shape[1] // GRID_W)
    layers = [_prepare_layer(l, p, rows) for l in range(norm_ffn1.shape[0])]
    y_prompt = _encode(x_prompt, meta_tokens, layers, final_norm)
    y_sample = _encode(x_sample, meta_tokens, layers, final_norm)
    return (y_prompt, y_sample)
```

```python
import functools

import numpy as np
import jax
import jax.numpy as jnp
from jax import lax
from jax.experimental import pallas as pl
from jax.experimental.pallas import tpu as pltpu

F32 = jnp.float32
BF16 = jnp.bfloat16

D_MODEL = 1024
N_META = 16
GRID_W = 64
NA_HEADS = 16
NA_HEAD_DIM = 64
NA_KH = 8
NA_KW = 16
LRU_BLOCK_DIM = 64
CONV_W = 4
CONV_LEFT = 2
LRU_C = 8.0
RMS_EPS = 1e-6
MASK_VALUE = -1e30

LANES = 128
SUBLANES = 8
N_PAIRS = NA_HEADS // 2
ATTN_QROWS = 4
ATTN_KROWS = ATTN_QROWS + NA_KH - 1
ATTN_KLOC = ATTN_KROWS * GRID_W
ATTN_KEXT = 768
LRU_CHUNK = 256
SCAN_GROUP = SUBLANES
LRU_TC = 16
LRU_MAX_TT = 704
VMEM_LIMIT = 56 * 1024 * 1024
MAX_ROW_TILE = 512


def _row_tile(n):
    for t in range(MAX_ROW_TILE, 0, -SUBLANES):
        if n % t == 0:
            return t
    raise ValueError(f"no row tile for {n} rows")


def _const_spec(shape):
    zeros = (0,) * len(shape)
    return pl.BlockSpec(shape, lambda *_: zeros, pipeline_mode=pl.Buffered(1))


def _params(sem):
    return pltpu.CompilerParams(dimension_semantics=sem, vmem_limit_bytes=VMEM_LIMIT)


def _rms(x, g):
    inv = lax.rsqrt(jnp.mean(x * x, axis=-1, keepdims=True) + RMS_EPS)
    return (x * inv) * g


def _gelu_tanh(x):
    c = np.float32(np.sqrt(2.0 / np.pi))
    return x * (0.5 * (1.0 + jnp.tanh(c * (x + np.float32(0.044715) * (x * x * x)))))


def _ffn_kernel(h_ref, g_ref, wg_ref, wu_ref, wd_ref, gf_ref, o_ref, *, final):
    x = h_ref[...]
    xn = _rms(x, g_ref[...]).astype(BF16)
    gate = jnp.dot(xn, wg_ref[...], preferred_element_type=F32)
    up = jnp.dot(xn, wu_ref[...], preferred_element_type=F32)
    act = (gate * jax.nn.sigmoid(gate) * up).astype(BF16)
    down = jnp.dot(act, wd_ref[...], preferred_element_type=F32)
    y = x + 0.5 * down
    o_ref[...] = _rms(y, gf_ref[...]) if final else y


def _ffn(h, g, wg, wu, wd, final_g=None):
    n, d = h.shape
    dff = wg.shape[1]
    tm = _row_tile(n)
    row = pl.BlockSpec((tm, d), lambda i: (i, 0))
    final = final_g is not None
    return pl.pallas_call(
        functools.partial(_ffn_kernel, final=final),
        grid=(n // tm,),
        in_specs=[row, _const_spec((1, d)), _const_spec((d, dff)), _const_spec((d, dff)),
                  _const_spec((dff, d)), _const_spec((1, d))],
        out_specs=row,
        out_shape=jax.ShapeDtypeStruct((n, d), F32),
        input_output_aliases={0: 0},
        compiler_params=_params(("parallel",)),
        name="ffn_final" if final else "ffn",
    )(h, g, wg, wu, wd, final_g if final else g)


def _inproj_kernel(h_ref, g_ref, w_ref, q_ref, k_ref, v_ref, xr_ref, yr_ref, gna_ref, glru_ref):
    d = h_ref.shape[1]
    xn = _rms(h_ref[...], g_ref[...]).astype(BF16)

    def proj(j):
        return jnp.dot(xn, w_ref[:, j * d:(j + 1) * d], preferred_element_type=F32)

    zq = (proj(0) * np.float32(NA_HEAD_DIM ** -0.5)).astype(BF16)
    zk = proj(1).astype(BF16)
    zv = proj(2).astype(BF16)
    for p in range(N_PAIRS):
        sl = slice(p * LANES, (p + 1) * LANES)
        q_ref[p] = zq[:, sl]
        k_ref[p] = zk[:, sl]
        v_ref[p] = zv[:, sl]
    xr_ref[...] = proj(3)
    yr_ref[...] = proj(4)
    gna_ref[...] = proj(5)
    glru_ref[...] = proj(6)


def _inproj(h, g, w_in):
    n, d = h.shape
    tm = _row_tile(n)
    row = pl.BlockSpec((tm, d), lambda i: (i, 0))
    pair = pl.BlockSpec((N_PAIRS, tm, LANES), lambda i: (0, i, 0))
    pair_shape = jax.ShapeDtypeStruct((N_PAIRS, n, LANES), BF16)
    row_shape = jax.ShapeDtypeStruct((n, d), F32)
    return pl.pallas_call(
        _inproj_kernel,
        grid=(n // tm,),
        in_specs=[row, _const_spec((1, d)), _const_spec(w_in.shape)],
        out_specs=[pair, pair, pair, row, row, row, row],
        out_shape=[pair_shape, pair_shape, pair_shape, row_shape, row_shape, row_shape, row_shape],
        compiler_params=_params(("parallel",)),
        name="inproj",
    )(h, g, w_in)


def _attn_block_cases(rows):
    nb = rows // ATTN_QROWS
    assert rows % ATTN_QROWS == 0 and nb >= 3 and nb % 2 == 0, rows

    def case(m):
        ks = min(max(ATTN_QROWS * m - NA_KH // 2, 0), rows - ATTN_KROWS)
        out = []
        for q in range(ATTN_QROWS):
            r = ATTN_QROWS * m + q
            rs = min(max(r - NA_KH // 2, 0), rows - NA_KH)
            out.append((rs - ks, r - rs))
        return tuple(out)

    assert all(case(m) == case(1) for m in range(1, nb - 1))
    return case(0), case(1), case(nb - 1)


def _attn_bias_table(rel_bias, rows):
    c = np.arange(GRID_W)
    cs = np.clip(c - NA_KW // 2, 0, GRID_W - NA_KW)
    kc = np.arange(GRID_W)
    ok = (kc[None, :] >= cs[:, None]) & (kc[None, :] < cs[:, None] + NA_KW)
    dcol = kc[None, :] - c[:, None] + NA_KW - 1
    onehot = (dcol[:, :, None] == np.arange(2 * NA_KW - 1)).astype(np.float32)
    cols = jnp.einsum('hdj,ckj->hdck', rel_bias.astype(F32), onehot, precision=lax.Precision.HIGHEST)
    cols = jnp.where(ok[None, None], cols, np.float32(MASK_VALUE))
    mask = lambda n: jnp.full((NA_HEADS, n, GRID_W, GRID_W), np.float32(MASK_VALUE))
    types = []
    for cases in _attn_block_cases(rows):
        per_q = []
        for off, e in cases:
            band = cols[:, NA_KH - 1 - e:2 * NA_KH - 1 - e]
            per_q.append(jnp.concatenate([mask(off), band, mask(ATTN_KROWS - NA_KH - off)], axis=1))
        types.append(jnp.stack(per_q, axis=1))
    tbl = jnp.stack(types, axis=1)
    tbl = jnp.transpose(tbl, (0, 1, 2, 4, 3, 5))
    tbl = tbl.reshape(N_PAIRS, 2, 3, ATTN_QROWS * GRID_W, ATTN_KLOC)
    tbl = jnp.transpose(tbl, (0, 2, 1, 3, 4)).reshape(N_PAIRS, 3, 2 * ATTN_QROWS * GRID_W, ATTN_KLOC)
    lead = tbl.shape[:3]
    return jnp.concatenate([tbl, jnp.zeros(lead + (N_META,), F32),
                            jnp.full(lead + (ATTN_KEXT - ATTN_KLOC - N_META,), np.float32(MASK_VALUE))], axis=-1)


def _attn_kernel(q_ref, k_ref, v_ref, bias_ref, o_ref, kx_ref, vx_ref, s_ref, p_ref, den_ref):
    seq = q_ref.shape[0]
    rows = (seq - N_META) // GRID_W
    nb = rows // ATTN_QROWS
    nq = ATTN_QROWS * GRID_W
    nt = (((1,), (1,)), ((), ()))
    lo = lax.broadcasted_iota(jnp.int32, (1, LANES), 1) < NA_HEAD_DIM

    def stack_heads(x):
        zero = jnp.zeros_like(x)
        return jnp.concatenate([jnp.where(lo, x, zero), jnp.where(lo, zero, x)], axis=0)

    def pick_heads(o, n):
        return jnp.where(lo, o[:n], o[n:])

    km = k_ref[0:N_META, :]
    vm = v_ref[0:N_META, :]

    qm = stack_heads(q_ref[0:N_META, :])
    sm = lax.dot_general(qm, km, nt, preferred_element_type=F32)
    em = jnp.exp(sm - jnp.max(sm, axis=-1, keepdims=True))
    om = jnp.dot(em.astype(BF16), vm, preferred_element_type=F32)
    om = om / jnp.sum(em, axis=-1, keepdims=True)
    o_ref[0:N_META, :] = pick_heads(om, N_META).astype(o_ref.dtype)

    pad = jnp.zeros((ATTN_KEXT - ATTN_KLOC - N_META, LANES), BF16)
    for j in range(2):
        kx_ref[j, ATTN_KLOC:ATTN_KLOC + N_META, :] = km
        vx_ref[j, ATTN_KLOC:ATTN_KLOC + N_META, :] = vm
        kx_ref[j, ATTN_KLOC + N_META:, :] = pad
        vx_ref[j, ATTN_KLOC + N_META:, :] = pad

    def key_start(m):
        ks = jnp.clip(ATTN_QROWS * m - NA_KH // 2, 0, rows - ATTN_KROWS)
        return pl.multiple_of(N_META + ks * GRID_W, 16)

    def query_start(m):
        return pl.multiple_of(N_META + m * nq, 16)

    def scores(m, j):
        kx_ref[j, 0:ATTN_KLOC, :] = k_ref[pl.ds(key_start(m), ATTN_KLOC), :]
        qs = stack_heads(q_ref[pl.ds(query_start(m), nq), :])
        kind = jnp.where(m == 0, 0, jnp.where(m == nb - 1, 2, 1))
        s_ref[j] = lax.dot_general(qs, kx_ref[j], nt, preferred_element_type=F32) + bias_ref[kind]

    def softmax(j):
        s = s_ref[j]
        e = jnp.exp(s - jnp.max(s, axis=-1, keepdims=True))
        den_ref[j] = jnp.sum(e, axis=-1, keepdims=True)
        p_ref[j] = e.astype(BF16)

    def values(m, j):
        vx_ref[j, 0:ATTN_KLOC, :] = v_ref[pl.ds(key_start(m), ATTN_KLOC), :]
        o = jnp.dot(p_ref[j], vx_ref[j], preferred_element_type=F32) / den_ref[j]
        o_ref[pl.ds(query_start(m), nq), :] = pick_heads(o, nq).astype(o_ref.dtype)

    scores(0, 0)
    scores(1, 1)
    softmax(0)

    def two_blocks(i, carry):
        m = 2 * i + 2
        scores(m, 0)
        softmax(1)
        values(m - 2, 0)
        scores(m + 1, 1)
        softmax(0)
        values(m - 1, 1)
        return carry

    lax.fori_loop(0, nb // 2 - 1, two_blocks, 0)
    softmax(1)
    values(nb - 2, 0)
    values(nb - 1, 1)


def _attention(q, k, v, bias_tbl, groups, seq):
    batch = groups * SCAN_GROUP
    shape4 = (N_PAIRS, groups, seq, SCAN_GROUP * LANES)
    blk = pl.BlockSpec((None, None, seq, LANES), lambda p, b: (p, b // SCAN_GROUP, 0, b % SCAN_GROUP))
    bias_spec = pl.BlockSpec((None,) + bias_tbl.shape[1:], lambda p, b: (p, 0, 0, 0))
    out = pl.pallas_call(
        _attn_kernel,
        grid=(N_PAIRS, batch),
        in_specs=[blk, blk, blk, bias_spec],
        out_specs=blk,
        out_shape=jax.ShapeDtypeStruct(shape4, BF16),
        scratch_shapes=[pltpu.VMEM((2, ATTN_KEXT, LANES), BF16), pltpu.VMEM((2, ATTN_KEXT, LANES), BF16),
                        pltpu.VMEM((2, 2 * ATTN_QROWS * GRID_W, ATTN_KEXT), F32),
                        pltpu.VMEM((2, 2 * ATTN_QROWS * GRID_W, ATTN_KEXT), BF16),
                        pltpu.VMEM((2, 2 * ATTN_QROWS * GRID_W, 1), F32)],
        compiler_params=_params(("parallel", "parallel")),
        name="attention",
    )(q.reshape(shape4), k.reshape(shape4), v.reshape(shape4), bias_tbl)
    return out.reshape(q.shape)


def _scan_time_block(seq):
    best = None
    for tt in range(LRU_TC, LRU_MAX_TT + 1, LRU_TC):
        if seq % tt == 0:
            best = tt
    assert best is not None, seq
    return best


def _lru_kernel(xl_ref, x_ref, xn_ref, cw_ref, cb_ref, w_ref, ba_ref, bx_ref, lam_ref, o_ref,
                x_s, carry, *, nt, reverse):
    step = pl.program_id(2)
    n_rows = x_ref.shape[0]
    blk = (nt - 1 - step) if reverse else step
    has_left = blk > 0
    has_right = blk < nt - 1
    left_rows = xl_ref.shape[0]
    chunk_rows = LRU_TC * SCAN_GROUP
    n_chunks = n_rows // chunk_rows

    @pl.when(step == 0)
    def _():
        carry[...] = jnp.zeros_like(carry)

    x_s[0:left_rows, :] = jnp.where(has_left, xl_ref[...], jnp.zeros_like(xl_ref))
    x_s[left_rows:left_rows + n_rows, :] = x_ref[...]
    x_s[left_rows + n_rows:, :] = jnp.where(has_right, xn_ref[...], jnp.zeros_like(xn_ref))

    lam = lam_ref[...]
    softplus = jnp.maximum(-lam, 0.0) + jnp.log1p(jnp.exp(-jnp.abs(lam)))
    neg_c_sp = np.float32(-LRU_C) * softplus
    cw = cw_ref[...]
    cb = cb_ref[...]
    ba = ba_ref[...]
    bx = bx_ref[...]

    def chunk(i, h):
        c = (n_chunks - 1 - i) if reverse else i
        row0 = pl.multiple_of(c * chunk_rows, chunk_rows)
        xc = cb
        for j in range(CONV_W):
            xc = xc + x_s[pl.ds(row0 + j * SCAN_GROUP, chunk_rows), :] * cw[j:j + 1, :]
        pre = jnp.dot(xc.astype(BF16), w_ref[...], preferred_element_type=F32)
        r = jax.nn.sigmoid(pre[:, :LRU_CHUNK] + ba)
        gi = jax.nn.sigmoid(pre[:, LRU_CHUNK:] + bx)
        log_a = neg_c_sp * r
        a = jnp.exp(log_a)
        y = -jnp.tanh(log_a) * (a * a + 1.0)
        root = jnp.where(y > 0.0, y * lax.rsqrt(y), 0.0)
        u = root * (gi * xc)
        hs = [None] * LRU_TC
        order = range(LRU_TC - 1, -1, -1) if reverse else range(LRU_TC)
        for k in order:
            rows = slice(k * SCAN_GROUP, (k + 1) * SCAN_GROUP)
            h = a[rows] * h + u[rows]
            hs[k] = h
        o_ref[pl.ds(row0, chunk_rows), :] = jnp.concatenate(hs, axis=0)
        return h

    carry[...] = lax.fori_loop(0, n_chunks, chunk, carry[...], unroll=2)


def _lru(xr, conv_w, conv_b, w_gate, ba, bx, lam, groups, seq, reverse):
    c = xr.shape[1]
    tt = _scan_time_block(seq)
    nt = seq // tt
    n_chunks = c // LRU_CHUNK
    d = 1 if reverse else 0
    left_steps = CONV_LEFT
    right_steps = CONV_W - 1 - CONV_LEFT
    assert tt % left_steps == 0 and right_steps == 1

    def tblk(i):
        return (nt - 1 - i) if reverse else i

    x3 = xr.reshape(groups, seq * SCAN_GROUP, c)
    cur = pl.BlockSpec((None, tt * SCAN_GROUP, LRU_CHUNK), lambda g, ch, i: (g, tblk(i), ch))
    left = pl.BlockSpec((None, left_steps * SCAN_GROUP, LRU_CHUNK),
                        lambda g, ch, i: (g, jnp.maximum(tblk(i) * (tt // left_steps) - 1, 0), ch))
    right = pl.BlockSpec((None, right_steps * SCAN_GROUP, LRU_CHUNK),
                         lambda g, ch, i: (g, jnp.minimum((tblk(i) + 1) * tt, seq - 1), ch))
    per_ch = lambda rows: pl.BlockSpec((rows, LRU_CHUNK), lambda g, ch, i: (0, ch))
    per_dir = pl.BlockSpec((None, 1, LRU_CHUNK), lambda g, ch, i: (d, 0, ch))
    out = pl.pallas_call(
        functools.partial(_lru_kernel, nt=nt, reverse=reverse),
        grid=(groups, n_chunks, nt),
        in_specs=[left, cur, right, per_ch(CONV_W), per_ch(1),
                  pl.BlockSpec((None, None, LRU_CHUNK, 2 * LRU_CHUNK), lambda g, ch, i: (d, ch, 0, 0)),
                  per_dir, per_dir, per_dir],
        out_specs=cur,
        out_shape=jax.ShapeDtypeStruct(x3.shape, F32),
        scratch_shapes=[pltpu.VMEM(((tt + CONV_W - 1) * SCAN_GROUP, LRU_CHUNK), F32),
                        pltpu.VMEM((SCAN_GROUP, LRU_CHUNK), F32)],
        compiler_params=_params(("parallel", "parallel", "arbitrary")),
        name="lru_bwd" if reverse else "lru_fwd",
    )(x3, x3, x3, conv_w, conv_b.reshape(1, c), w_gate,
      ba.reshape(2, 1, c), bx.reshape(2, 1, c), lam.reshape(2, 1, c))
    return out.reshape(xr.shape)


def _lru_gate_weights(wa, wx):
    per = LRU_CHUNK // LRU_BLOCK_DIM

    def dense(w):
        n_dir, n_blk, bd, _ = w.shape
        w = w.reshape(n_dir, n_blk // per, per, bd, bd)
        eye = jnp.eye(per, dtype=w.dtype)
        full = jnp.einsum('dcpij,pq->dcpiqj', w, eye)
        return full.reshape(n_dir, n_blk // per, LRU_CHUNK, LRU_CHUNK)

    return jnp.concatenate([dense(wa), dense(wx)], axis=-1).astype(BF16)


def _mixout_kernel(h_ref, na_ref, yr_ref, gna_ref, glru_ref, hf_ref, hb_ref, wn_ref, wl_ref, wo_ref, o_ref):
    na = jnp.concatenate([na_ref[p] for p in range(N_PAIRS)], axis=-1)
    na_p = jnp.dot(na, wn_ref[...], preferred_element_type=F32)
    hr = hf_ref[...] + hb_ref[...]
    lru_in = (_gelu_tanh(yr_ref[...]) * hr).astype(BF16)
    lru_p = jnp.dot(lru_in, wl_ref[...], preferred_element_type=F32)
    merged = jax.nn.sigmoid(gna_ref[...]) * na_p + jax.nn.sigmoid(glru_ref[...]) * lru_p
    o_ref[...] = h_ref[...] + jnp.dot(merged.astype(BF16), wo_ref[...], preferred_element_type=F32)


def _mixout(h, na, yr, gna, glru, hf, hb, wn, wl, wo):
    n, d = h.shape
    tm = _row_tile(n)
    row = pl.BlockSpec((tm, d), lambda i: (i, 0))
    return pl.pallas_call(
        _mixout_kernel,
        grid=(n // tm,),
        in_specs=[row, pl.BlockSpec((N_PAIRS, tm, LANES), lambda i: (0, i, 0)), row, row, row, row, row,
                  _const_spec((d, d)), _const_spec((d, d)), _const_spec((d, d))],
        out_specs=row,
        out_shape=jax.ShapeDtypeStruct((n, d), F32),
        input_output_aliases={0: 0},
        compiler_params=_params(("parallel",)),
        name="mixout",
    )(h, na, yr, gna, glru, hf, hb, wn, wl, wo)


def _prepare_layer(l, p, rows):
    d = D_MODEL
    row = lambda a: a[l].reshape(1, d).astype(F32)
    return dict(
        norm_ffn1=row(p["norm_ffn1"]), norm_mix=row(p["norm_mix"]), norm_ffn2=row(p["norm_ffn2"]),
        ffn1=(p["ffn1_w_gate"][l].astype(BF16), p["ffn1_w_up"][l].astype(BF16), p["ffn1_w_down"][l].astype(BF16)),
        ffn2=(p["ffn2_w_gate"][l].astype(BF16), p["ffn2_w_up"][l].astype(BF16), p["ffn2_w_down"][l].astype(BF16)),
        w_in=p["w_in"][l].astype(BF16),
        bias_tbl=_attn_bias_table(p["na_rel_bias"][l], rows),
        conv_w=p["conv_w"][l].astype(F32), conv_b=p["conv_b"][l].astype(F32),
        w_gate=_lru_gate_weights(p["lru_wa"][l], p["lru_wx"][l]),
        ba=p["lru_ba"][l].astype(F32), bx=p["lru_bx"][l].astype(F32), lam=p["lru_lambda"][l].astype(F32),
        w_na_proj=p["w_na_proj"][l].astype(BF16), w_lru_proj=p["w_lru_proj"][l].astype(BF16),
        w_out=p["w_out"][l].astype(BF16),
    )


def _encode(x, meta_tokens, layers, final_norm):
    batch, t, d = x.shape
    seq = N_META + t
    assert batch % SCAN_GROUP == 0
    groups = batch // SCAN_GROUP
    meta = jnp.broadcast_to(meta_tokens.astype(x.dtype)[None], (batch, N_META, d))
    h = jnp.concatenate([meta, x], axis=1)
    h = h.reshape(groups, SCAN_GROUP, seq, d).transpose(0, 2, 1, 3).reshape(batch * seq, d)
    gf = final_norm.reshape(1, d).astype(F32)
    for li, lp in enumerate(layers):
        h = _ffn(h, lp["norm_ffn1"], *lp["ffn1"])
        q, k, v, xr, yr, gna, glru = _inproj(h, lp["norm_mix"], lp["w_in"])
        na = _attention(q, k, v, lp["bias_tbl"], groups, seq)
        lru_args = (xr, lp["conv_w"], lp["conv_b"], lp["w_gate"], lp["ba"], lp["bx"], lp["lam"], groups, seq)
        hf = _lru(*lru_args, reverse=False)
        hb = _lru(*lru_args, reverse=True)
        h = _mixout(h, na, yr, gna, glru, hf, hb, lp["w_na_proj"], lp["w_lru_proj"], lp["w_out"])
        h = _ffn(h, lp["norm_ffn2"], *lp["ffn2"], final_g=gf if li == len(layers) - 1 else None)
    h = h.reshape(groups, seq, SCAN_GROUP, d).transpose(0, 2, 1, 3).reshape(batch, seq, d)
    return h[:, N_META:]


def kernel(x_prompt, x_sample, meta_tokens, norm_ffn1, ffn1_w_gate, ffn1_w_up, ffn1_w_down, norm_mix, w_in, na_rel_bias, conv_w, conv_b, lru_wa, lru_ba, lru_wx, lru_bx, lru_lambda, w_na_proj, w_lru_proj, w_out, norm_ffn2, ffn2_w_gate, ffn2_w_up, ffn2_w_down, final_norm):
    p = dict(norm_ffn1=norm_ffn1, ffn1_w_gate=ffn1_w_gate, ffn1_w_up=ffn1_w_up, ffn1_w_down=ffn1_w_down,
             norm_mix=norm_mix, w_in=w_in, na_rel_bias=na_rel_bias, conv_w=conv_w, conv_b=conv_b,
             lru_wa=lru_wa, lru_ba=lru_ba, lru_wx=lru_wx, lru_bx=lru_bx, lru_lambda=lru_lambda,
             w_na_proj=w_na_proj, w_lru_proj=w_lru_proj, w_out=w_out, norm_ffn2=norm_ffn2,
             ffn2_w_gate=ffn2_w_gate, ffn2_w_up=ffn2_w_up, ffn2_w_down=ffn2_w_down)
    rows = x_prompt.shape[1] // GRID_W
    assert _attn_block_cases(rows) == _attn_block_cases(x_sample.shape[1] // GRID_W)
    layers = [_prepare_layer(l, p, rows) for l in range(norm_ffn1.shape[0])]
    y_prompt = _encode(x_prompt, meta_tokens, layers, final_norm)
    y_sample = _encode(x_sample, meta_tokens, layers, final_norm)
    return (y_prompt, y_sample)
```

```python
import functools

import numpy as np
import jax
import jax.numpy as jnp
from jax import lax
from jax.experimental import pallas as pl
from jax.experimental.pallas import tpu as pltpu

F32 = jnp.float32
BF16 = jnp.bfloat16

D_MODEL = 1024
N_META = 16
GRID_W = 64
NA_HEADS = 16
NA_HEAD_DIM = 64
NA_KH = 8
NA_KW = 16
LRU_BLOCK_DIM = 64
CONV_W = 4
CONV_LEFT = 2
LRU_C = 8.0
RMS_EPS = 1e-6
MASK_VALUE = -1e30

LANES = 128
SUBLANES = 8
N_PAIRS = NA_HEADS // 2
ATTN_QROWS = 4
ATTN_KROWS = ATTN_QROWS + NA_KH - 1
ATTN_KLOC = ATTN_KROWS * GRID_W
ATTN_KEXT = 768
LRU_CHUNK = 256
SCAN_GROUP = SUBLANES
LRU_TC = 16
LRU_MAX_TT = 704
VMEM_LIMIT = 56 * 1024 * 1024
MAX_ROW_TILE = 512


def _row_tile(n):
    for t in range(MAX_ROW_TILE, 0, -SUBLANES):
        if n % t == 0:
            return t
    raise ValueError(f"no row tile for {n} rows")


def _const_spec(shape):
    zeros = (0,) * len(shape)
    return pl.BlockSpec(shape, lambda *_: zeros, pipeline_mode=pl.Buffered(1))


def _params(sem):
    return pltpu.CompilerParams(dimension_semantics=sem, vmem_limit_bytes=VMEM_LIMIT)


def _rms(x, g):
    inv = lax.rsqrt(jnp.mean(x * x, axis=-1, keepdims=True) + RMS_EPS)
    return (x * inv) * g


def _gelu_tanh(x):
    c = np.float32(np.sqrt(2.0 / np.pi))
    return x * (0.5 * (1.0 + jnp.tanh(c * (x + np.float32(0.044715) * (x * x * x)))))


def _ffn_kernel(h_ref, g_ref, wg_ref, wu_ref, wd_ref, gf_ref, o_ref, *, final):
    x = h_ref[...]
    xn = _rms(x, g_ref[...]).astype(BF16)
    gate = jnp.dot(xn, wg_ref[...], preferred_element_type=F32)
    up = jnp.dot(xn, wu_ref[...], preferred_element_type=F32)
    act = (gate * jax.nn.sigmoid(gate) * up).astype(BF16)
    down = jnp.dot(act, wd_ref[...], preferred_element_type=F32)
    y = x + 0.5 * down
    o_ref[...] = _rms(y, gf_ref[...]) if final else y


def _ffn(h, g, wg, wu, wd, final_g=None):
    n, d = h.shape
    dff = wg.shape[1]
    tm = _row_tile(n)
    row = pl.BlockSpec((tm, d), lambda i: (i, 0))
    final = final_g is not None
    return pl.pallas_call(
        functools.partial(_ffn_kernel, final=final),
        grid=(n // tm,),
        in_specs=[row, _const_spec((1, d)), _const_spec((d, dff)), _const_spec((d, dff)),
                  _const_spec((dff, d)), _const_spec((1, d))],
        out_specs=row,
        out_shape=jax.ShapeDtypeStruct((n, d), F32),
        input_output_aliases={0: 0},
        compiler_params=_params(("parallel",)),
        name="ffn_final" if final else "ffn",
    )(h, g, wg, wu, wd, final_g if final else g)


def _seq_tile(seq):
    best = None
    for tq in range(16, MAX_ROW_TILE // SCAN_GROUP + 1, 16):
        if seq % tq == 0:
            best = tq
    assert best is not None, seq
    return best


def _inproj_kernel(h_ref, g_ref, w_ref, q_ref, k_ref, v_ref, xr_ref, yr_ref, gna_ref, glru_ref):
    nseq, tq, d = h_ref.shape
    xn = _rms(h_ref[...].reshape(nseq * tq, d), g_ref[...]).astype(BF16)

    def proj(j):
        return jnp.dot(xn, w_ref[:, j * d:(j + 1) * d], preferred_element_type=F32)

    zq = (proj(0) * np.float32(NA_HEAD_DIM ** -0.5)).astype(BF16)
    zk = proj(1).astype(BF16)
    zv = proj(2).astype(BF16)
    for p in range(N_PAIRS):
        sl = slice(p * LANES, (p + 1) * LANES)
        for j in range(nseq):
            rows = slice(j * tq, (j + 1) * tq)
            q_ref[p, j] = zq[rows, sl]
            k_ref[p, j] = zk[rows, sl]
            v_ref[p, j] = zv[rows, sl]
    xr = proj(3)
    for s in range(d // LANES):
        for j in range(nseq):
            xr_ref[s, pl.ds(j, tq, stride=nseq), :] = xr[j * tq:(j + 1) * tq, s * LANES:(s + 1) * LANES]
    yr_ref[...] = proj(4).reshape(nseq, tq, d)
    gna_ref[...] = proj(5).reshape(nseq, tq, d)
    glru_ref[...] = proj(6).reshape(nseq, tq, d)


def _inproj(h, g, w_in, groups, seq):
    n, d = h.shape
    batch = groups * SCAN_GROUP
    tq = _seq_tile(seq)
    n_slabs = d // LANES
    tile = pl.BlockSpec((SCAN_GROUP, tq, d), lambda g_, i: (g_, i, 0))
    pair = pl.BlockSpec((N_PAIRS, SCAN_GROUP, tq, LANES), lambda g_, i: (0, g_, i, 0))
    slab = pl.BlockSpec((n_slabs, None, tq * SCAN_GROUP, LANES), lambda g_, i: (0, g_, i, 0))
    pair_shape = jax.ShapeDtypeStruct((N_PAIRS, batch, seq, LANES), BF16)
    tile_shape = jax.ShapeDtypeStruct((batch, seq, d), F32)
    slab_shape = jax.ShapeDtypeStruct((n_slabs, groups, seq * SCAN_GROUP, LANES), F32)
    return pl.pallas_call(
        _inproj_kernel,
        grid=(groups, seq // tq),
        in_specs=[tile, _const_spec((1, d)), _const_spec(w_in.shape)],
        out_specs=[pair, pair, pair, slab, tile, tile, tile],
        out_shape=[pair_shape, pair_shape, pair_shape, slab_shape, tile_shape, tile_shape, tile_shape],
        compiler_params=_params(("parallel", "parallel")),
        name="inproj",
    )(h.reshape(batch, seq, d), g, w_in)


def _attn_block_cases(rows):
    nb = rows // ATTN_QROWS
    assert rows % ATTN_QROWS == 0 and nb >= 3 and nb % 2 == 0, rows

    def case(m):
        ks = min(max(ATTN_QROWS * m - NA_KH // 2, 0), rows - ATTN_KROWS)
        out = []
        for q in range(ATTN_QROWS):
            r = ATTN_QROWS * m + q
            rs = min(max(r - NA_KH // 2, 0), rows - NA_KH)
            out.append((rs - ks, r - rs))
        return tuple(out)

    assert all(case(m) == case(1) for m in range(1, nb - 1))
    return case(0), case(1), case(nb - 1)


def _attn_bias_table(rel_bias, rows):
    c = np.arange(GRID_W)
    cs = np.clip(c - NA_KW // 2, 0, GRID_W - NA_KW)
    kc = np.arange(GRID_W)
    ok = (kc[None, :] >= cs[:, None]) & (kc[None, :] < cs[:, None] + NA_KW)
    dcol = kc[None, :] - c[:, None] + NA_KW - 1
    onehot = (dcol[:, :, None] == np.arange(2 * NA_KW - 1)).astype(np.float32)
    cols = jnp.einsum('hdj,ckj->hdck', rel_bias.astype(F32), onehot, precision=lax.Precision.HIGHEST)
    cols = jnp.where(ok[None, None], cols, np.float32(MASK_VALUE))
    mask = lambda n: jnp.full((NA_HEADS, n, GRID_W, GRID_W), np.float32(MASK_VALUE))
    types = []
    for cases in _attn_block_cases(rows):
        per_q = []
        for off, e in cases:
            band = cols[:, NA_KH - 1 - e:2 * NA_KH - 1 - e]
            per_q.append(jnp.concatenate([mask(off), band, mask(ATTN_KROWS - NA_KH - off)], axis=1))
        types.append(jnp.stack(per_q, axis=1))
    tbl = jnp.stack(types, axis=1)
    tbl = jnp.transpose(tbl, (0, 1, 2, 4, 3, 5))
    tbl = tbl.reshape(N_PAIRS, 2, 3, ATTN_QROWS * GRID_W, ATTN_KLOC)
    tbl = jnp.transpose(tbl, (0, 2, 1, 3, 4)).reshape(N_PAIRS, 3, 2 * ATTN_QROWS * GRID_W, ATTN_KLOC)
    lead = tbl.shape[:3]
    return jnp.concatenate([tbl, jnp.zeros(lead + (N_META,), F32),
                            jnp.full(lead + (ATTN_KEXT - ATTN_KLOC - N_META,), np.float32(MASK_VALUE))], axis=-1)


def _attn_kernel(q_ref, k_ref, v_ref, bias_ref, o_ref, kx_ref, vx_ref, s_ref, p_ref, den_ref):
    seq = q_ref.shape[0]
    rows = (seq - N_META) // GRID_W
    nb = rows // ATTN_QROWS
    nq = ATTN_QROWS * GRID_W
    nt = (((1,), (1,)), ((), ()))
    lo = lax.broadcasted_iota(jnp.int32, (1, LANES), 1) < NA_HEAD_DIM

    def stack_heads(x):
        zero = jnp.zeros_like(x)
        return jnp.concatenate([jnp.where(lo, x, zero), jnp.where(lo, zero, x)], axis=0)

    def pick_heads(o, n):
        return jnp.where(lo, o[:n], o[n:])

    km = k_ref[0:N_META, :]
    vm = v_ref[0:N_META, :]

    qm = stack_heads(q_ref[0:N_META, :])
    sm = lax.dot_general(qm, km, nt, preferred_element_type=F32)
    em = jnp.exp(sm - jnp.max(sm, axis=-1, keepdims=True))
    om = jnp.dot(em.astype(BF16), vm, preferred_element_type=F32)
    om = om / jnp.sum(em, axis=-1, keepdims=True)
    o_ref[0:N_META, :] = pick_heads(om, N_META).astype(o_ref.dtype)

    pad = jnp.zeros((ATTN_KEXT - ATTN_KLOC - N_META, LANES), BF16)
    for j in range(2):
        kx_ref[j, ATTN_KLOC:ATTN_KLOC + N_META, :] = km
        vx_ref[j, ATTN_KLOC:ATTN_KLOC + N_META, :] = vm
        kx_ref[j, ATTN_KLOC + N_META:, :] = pad
        vx_ref[j, ATTN_KLOC + N_META:, :] = pad

    def key_start(m):
        ks = jnp.clip(ATTN_QROWS * m - NA_KH // 2, 0, rows - ATTN_KROWS)
        return pl.multiple_of(N_META + ks * GRID_W, 16)

    def query_start(m):
        return pl.multiple_of(N_META + m * nq, 16)

    def scores(m, j):
        kx_ref[j, 0:ATTN_KLOC, :] = k_ref[pl.ds(key_start(m), ATTN_KLOC), :]
        qs = stack_heads(q_ref[pl.ds(query_start(m), nq), :])
        kind = jnp.where(m == 0, 0, jnp.where(m == nb - 1, 2, 1))
        s_ref[j] = lax.dot_general(qs, kx_ref[j], nt, preferred_element_type=F32) + bias_ref[kind]

    def softmax(j):
        s = s_ref[j]
        e = jnp.exp(s - jnp.max(s, axis=-1, keepdims=True))
        den_ref[j] = jnp.sum(e, axis=-1, keepdims=True)
        p_ref[j] = e.astype(BF16)

    def values(m, j):
        vx_ref[j, 0:ATTN_KLOC, :] = v_ref[pl.ds(key_start(m), ATTN_KLOC), :]
        o = jnp.dot(p_ref[j], vx_ref[j], preferred_element_type=F32) / den_ref[j]
        o_ref[pl.ds(query_start(m), nq), :] = pick_heads(o, nq).astype(o_ref.dtype)

    scores(0, 0)
    scores(1, 1)
    softmax(0)

    def two_blocks(i, carry):
        m = 2 * i + 2
        scores(m, 0)
        softmax(1)
        values(m - 2, 0)
        scores(m + 1, 1)
        softmax(0)
        values(m - 1, 1)
        return carry

    lax.fori_loop(0, nb // 2 - 1, two_blocks, 0)
    softmax(1)
    values(nb - 2, 0)
    values(nb - 1, 1)


def _attention(q, k, v, bias_tbl):
    _, batch, seq, _ = q.shape
    blk = pl.BlockSpec((None, None, seq, LANES), lambda p, b: (p, b, 0, 0))
    bias_spec = pl.BlockSpec((None,) + bias_tbl.shape[1:], lambda p, b: (p, 0, 0, 0))
    out = pl.pallas_call(
        _attn_kernel,
        grid=(N_PAIRS, batch),
        in_specs=[blk, blk, blk, bias_spec],
        out_specs=blk,
        out_shape=jax.ShapeDtypeStruct(q.shape, BF16),
        scratch_shapes=[pltpu.VMEM((2, ATTN_KEXT, LANES), BF16), pltpu.VMEM((2, ATTN_KEXT, LANES), BF16),
                        pltpu.VMEM((2, 2 * ATTN_QROWS * GRID_W, ATTN_KEXT), F32),
                        pltpu.VMEM((2, 2 * ATTN_QROWS * GRID_W, ATTN_KEXT), BF16),
                        pltpu.VMEM((2, 2 * ATTN_QROWS * GRID_W, 1), F32)],
        compiler_params=_params(("parallel", "parallel")),
        name="attention",
    )(q, k, v, bias_tbl)
    return out


def _scan_time_block(seq):
    best = None
    for tt in range(LRU_TC, LRU_MAX_TT + 1, LRU_TC):
        if seq % tt == 0:
            best = tt
    assert best is not None, seq
    return best


def _lru_kernel(xl_ref, x_ref, xn_ref, cw_ref, cb_ref, w_ref, ba_ref, bx_ref, lam_ref, o_ref,
                x_s, carry, *, nt, reverse):
    step = pl.program_id(2)
    n_slabs, n_rows, _ = x_ref.shape
    blk = (nt - 1 - step) if reverse else step
    has_left = blk > 0
    has_right = blk < nt - 1
    left_rows = xl_ref.shape[1]
    chunk_rows = LRU_TC * SCAN_GROUP
    n_chunks = n_rows // chunk_rows

    @pl.when(step == 0)
    def _():
        carry[...] = jnp.zeros_like(carry)

    for s in range(n_slabs):
        sl = slice(s * LANES, (s + 1) * LANES)
        x_s[0:left_rows, sl] = jnp.where(has_left, xl_ref[s], jnp.zeros_like(xl_ref[s]))
        x_s[left_rows:left_rows + n_rows, sl] = x_ref[s]
        x_s[left_rows + n_rows:, sl] = jnp.where(has_right, xn_ref[s], jnp.zeros_like(xn_ref[s]))

    lam = lam_ref[...]
    softplus = jnp.maximum(-lam, 0.0) + jnp.log1p(jnp.exp(-jnp.abs(lam)))
    neg_c_sp = np.float32(-LRU_C) * softplus
    cw = cw_ref[...]
    cb = cb_ref[...]
    ba = ba_ref[...]
    bx = bx_ref[...]

    def chunk(i, h):
        c = (n_chunks - 1 - i) if reverse else i
        row0 = pl.multiple_of(c * chunk_rows, chunk_rows)
        xc = cb
        for j in range(CONV_W):
            xc = xc + x_s[pl.ds(row0 + j * SCAN_GROUP, chunk_rows), :] * cw[j:j + 1, :]
        pre = jnp.dot(xc.astype(BF16), w_ref[...], preferred_element_type=F32)
        r = jax.nn.sigmoid(pre[:, :LRU_CHUNK] + ba)
        gi = jax.nn.sigmoid(pre[:, LRU_CHUNK:] + bx)
        log_a = neg_c_sp * r
        a = jnp.exp(log_a)
        y = -jnp.tanh(log_a) * (a * a + 1.0)
        root = jnp.where(y > 0.0, y * lax.rsqrt(y), 0.0)
        u = root * (gi * xc)
        hs = [None] * LRU_TC
        order = range(LRU_TC - 1, -1, -1) if reverse else range(LRU_TC)
        for k in order:
            rows = slice(k * SCAN_GROUP, (k + 1) * SCAN_GROUP)
            h = a[rows] * h + u[rows]
            hs[k] = h
        hc = jnp.concatenate(hs, axis=0)
        for s in range(n_slabs):
            o_ref[s, pl.ds(row0, chunk_rows), :] = hc[:, s * LANES:(s + 1) * LANES]
        return h

    carry[...] = lax.fori_loop(0, n_chunks, chunk, carry[...], unroll=2)


def _lru(xr, conv_w, conv_b, w_gate, ba, bx, lam, seq, reverse):
    n_slabs, groups, _, _ = xr.shape
    c = n_slabs * LANES
    tt = _scan_time_block(seq)
    nt = seq // tt
    n_chunks = c // LRU_CHUNK
    slabs = LRU_CHUNK // LANES
    d = 1 if reverse else 0
    left_steps = CONV_LEFT
    right_steps = CONV_W - 1 - CONV_LEFT
    assert tt % left_steps == 0 and right_steps == 1

    def tblk(i):
        return (nt - 1 - i) if reverse else i

    cur = pl.BlockSpec((slabs, None, tt * SCAN_GROUP, LANES), lambda g, ch, i: (ch, g, tblk(i), 0))
    left = pl.BlockSpec((slabs, None, left_steps * SCAN_GROUP, LANES),
                        lambda g, ch, i: (ch, g, jnp.maximum(tblk(i) * (tt // left_steps) - 1, 0), 0))
    right = pl.BlockSpec((slabs, None, right_steps * SCAN_GROUP, LANES),
                         lambda g, ch, i: (ch, g, jnp.minimum((tblk(i) + 1) * tt, seq - 1), 0))
    per_ch = lambda rows: pl.BlockSpec((rows, LRU_CHUNK), lambda g, ch, i: (0, ch))
    per_dir = pl.BlockSpec((None, 1, LRU_CHUNK), lambda g, ch, i: (d, 0, ch))
    return pl.pallas_call(
        functools.partial(_lru_kernel, nt=nt, reverse=reverse),
        grid=(groups, n_chunks, nt),
        in_specs=[left, cur, right, per_ch(CONV_W), per_ch(1),
                  pl.BlockSpec((None, None, LRU_CHUNK, 2 * LRU_CHUNK), lambda g, ch, i: (d, ch, 0, 0)),
                  per_dir, per_dir, per_dir],
        out_specs=cur,
        out_shape=jax.ShapeDtypeStruct(xr.shape, F32),
        scratch_shapes=[pltpu.VMEM(((tt + CONV_W - 1) * SCAN_GROUP, LRU_CHUNK), F32),
                        pltpu.VMEM((SCAN_GROUP, LRU_CHUNK), F32)],
        compiler_params=_params(("parallel", "parallel", "arbitrary")),
        name="lru_bwd" if reverse else "lru_fwd",
    )(xr, xr, xr, conv_w, conv_b.reshape(1, c), w_gate,
      ba.reshape(2, 1, c), bx.reshape(2, 1, c), lam.reshape(2, 1, c))


def _lru_gate_weights(wa, wx):
    per = LRU_CHUNK // LRU_BLOCK_DIM

    def dense(w):
        n_dir, n_blk, bd, _ = w.shape
        w = w.reshape(n_dir, n_blk // per, per, bd, bd)
        eye = jnp.eye(per, dtype=w.dtype)
        full = jnp.einsum('dcpij,pq->dcpiqj', w, eye)
        return full.reshape(n_dir, n_blk // per, LRU_CHUNK, LRU_CHUNK)

    return jnp.concatenate([dense(wa), dense(wx)], axis=-1).astype(BF16)


def _mixout_kernel(h_ref, na_ref, yr_ref, gna_ref, glru_ref, hf_ref, hb_ref, wn_ref, wl_ref, wo_ref, o_ref):
    nseq, tq, d = h_ref.shape
    rows = lambda ref: ref[...].reshape(nseq * tq, d)

    def scan_rows(ref):
        return jnp.concatenate(
            [jnp.concatenate([ref[s, pl.ds(j, tq, stride=nseq), :] for j in range(nseq)], axis=0)
             for s in range(d // LANES)], axis=-1)

    na = jnp.concatenate(
        [jnp.concatenate([na_ref[p, j] for j in range(nseq)], axis=0) for p in range(N_PAIRS)], axis=-1)
    na_p = jnp.dot(na, wn_ref[...], preferred_element_type=F32)
    hr = scan_rows(hf_ref) + scan_rows(hb_ref)
    lru_in = (_gelu_tanh(rows(yr_ref)) * hr).astype(BF16)
    lru_p = jnp.dot(lru_in, wl_ref[...], preferred_element_type=F32)
    merged = jax.nn.sigmoid(rows(gna_ref)) * na_p + jax.nn.sigmoid(rows(glru_ref)) * lru_p
    out = rows(h_ref) + jnp.dot(merged.astype(BF16), wo_ref[...], preferred_element_type=F32)
    o_ref[...] = out.reshape(nseq, tq, d)


def _mixout(h, na, yr, gna, glru, hf, hb, wn, wl, wo):
    n, d = h.shape
    _, batch, seq, _ = na.shape
    groups = batch // SCAN_GROUP
    tq = _seq_tile(seq)
    n_slabs = d // LANES
    tile = pl.BlockSpec((SCAN_GROUP, tq, d), lambda g_, i: (g_, i, 0))
    pair = pl.BlockSpec((N_PAIRS, SCAN_GROUP, tq, LANES), lambda g_, i: (0, g_, i, 0))
    slab = pl.BlockSpec((n_slabs, None, tq * SCAN_GROUP, LANES), lambda g_, i: (0, g_, i, 0))
    out = pl.pallas_call(
        _mixout_kernel,
        grid=(groups, seq // tq),
        in_specs=[tile, pair, tile, tile, tile, slab, slab,
                  _const_spec((d, d)), _const_spec((d, d)), _const_spec((d, d))],
        out_specs=tile,
        out_shape=jax.ShapeDtypeStruct((batch, seq, d), F32),
        input_output_aliases={0: 0},
        compiler_params=_params(("parallel", "parallel")),
        name="mixout",
    )(h.reshape(batch, seq, d), na, yr, gna, glru, hf, hb, wn, wl, wo)
    return out.reshape(n, d)


def _prepare_layer(l, p, rows):
    d = D_MODEL
    row = lambda a: a[l].reshape(1, d).astype(F32)
    return dict(
        norm_ffn1=row(p["norm_ffn1"]), norm_mix=row(p["norm_mix"]), norm_ffn2=row(p["norm_ffn2"]),
        ffn1=(p["ffn1_w_gate"][l].astype(BF16), p["ffn1_w_up"][l].astype(BF16), p["ffn1_w_down"][l].astype(BF16)),
        ffn2=(p["ffn2_w_gate"][l].astype(BF16), p["ffn2_w_up"][l].astype(BF16), p["ffn2_w_down"][l].astype(BF16)),
        w_in=p["w_in"][l].astype(BF16),
        bias_tbl=_attn_bias_table(p["na_rel_bias"][l], rows),
        conv_w=p["conv_w"][l].astype(F32), conv_b=p["conv_b"][l].astype(F32),
        w_gate=_lru_gate_weights(p["lru_wa"][l], p["lru_wx"][l]),
        ba=p["lru_ba"][l].astype(F32), bx=p["lru_bx"][l].astype(F32), lam=p["lru_lambda"][l].astype(F32),
        w_na_proj=p["w_na_proj"][l].astype(BF16), w_lru_proj=p["w_lru_proj"][l].astype(BF16),
        w_out=p["w_out"][l].astype(BF16),
    )


def _encode(x, meta_tokens, layers, final_norm):
    batch, t, d = x.shape
    seq = N_META + t
    assert batch % SCAN_GROUP == 0
    groups = batch // SCAN_GROUP
    meta = jnp.broadcast_to(meta_tokens.astype(x.dtype)[None], (batch, N_META, d))
    h = jnp.concatenate([meta, x], axis=1).reshape(batch * seq, d)
    gf = final_norm.reshape(1, d).astype(F32)
    for li, lp in enumerate(layers):
        h = _ffn(h, lp["norm_ffn1"], *lp["ffn1"])
        q, k, v, xr, yr, gna, glru = _inproj(h, lp["norm_mix"], lp["w_in"], groups, seq)
        na = _attention(q, k, v, lp["bias_tbl"])
        lru_args = (xr, lp["conv_w"], lp["conv_b"], lp["w_gate"], lp["ba"], lp["bx"], lp["lam"], seq)
        hf = _lru(*lru_args, reverse=False)
        hb = _lru(*lru_args, reverse=True)
        h = _mixout(h, na, yr, gna, glru, hf, hb, lp["w_na_proj"], lp["w_lru_proj"], lp["w_out"])
        h = _ffn(h, lp["norm_ffn2"], *lp["ffn2"], final_g=gf if li == len(layers) - 1 else None)
    return h.reshape(batch, seq, d)[:, N_META:]


def kernel(x_prompt, x_sample, meta_tokens, norm_ffn1, ffn1_w_gate, ffn1_w_up, ffn1_w_down, norm_mix, w_in, na_rel_bias, conv_w, conv_b, lru_wa, lru_ba, lru_wx, lru_bx, lru_lambda, w_na_proj, w_lru_proj, w_out, norm_ffn2, ffn2_w_gate, ffn2_w_up, ffn2_w_down, final_norm):
    p = dict(norm_ffn1=norm_ffn1, ffn1_w_gate=ffn1_w_gate, ffn1_w_up=ffn1_w_up, ffn1_w_down=ffn1_w_down,
             norm_mix=norm_mix, w_in=w_in, na_rel_bias=na_rel_bias, conv_w=conv_w, conv_b=conv_b,
             lru_wa=lru_wa, lru_ba=lru_ba, lru_wx=lru_wx, lru_bx=lru_bx, lru_lambda=lru_lambda,
             w_na_proj=w_na_proj, w_lru_proj=w_lru_proj, w_out=w_out, norm_ffn2=norm_ffn2,
             ffn2_w_gate=ffn2_w_gate, ffn2_w_up=ffn2_w_up, ffn2_w_down=ffn2_w_down)
    rows = x_prompt.shape[1] // GRID_W
    assert _attn_block_cases(rows) == _attn_block_cases(x_sample.shape[1] // GRID_W)
    layers = [_prepare_layer(l, p, rows) for l in range(norm_ffn1.shape[0])]
    y_prompt = _encode(x_prompt, meta_tokens, layers, final_norm)
    y_sample = _encode(x_sample, meta_tokens, layers, final_norm)
    return (y_prompt, y_sample)
```

```python
import functools

import numpy as np
import jax
import jax.numpy as jnp
from jax import lax
from jax.experimental import pallas as pl
from jax.experimental.pallas import tpu as pltpu

F32 = jnp.float32
BF16 = jnp.bfloat16

D_MODEL = 1024
N_META = 16
GRID_W = 64
NA_HEADS = 16
NA_HEAD_DIM = 64
NA_KH = 8
NA_KW = 16
LRU_BLOCK_DIM = 64
CONV_W = 4
CONV_LEFT = 2
LRU_C = 8.0
RMS_EPS = 1e-6
MASK_VALUE = -1e30

LANES = 128
SUBLANES = 8
N_PAIRS = NA_HEADS // 2
ATTN_QROWS = 4
ATTN_KROWS = ATTN_QROWS + NA_KH - 1
ATTN_KLOC = ATTN_KROWS * GRID_W
ATTN_KEXT = 768
LRU_CHUNK = 256
SCAN_GROUP = SUBLANES
LRU_TC = 16
LRU_MAX_TT = 704
VMEM_LIMIT = 56 * 1024 * 1024
MAX_ROW_TILE = 512


def _row_tile(n):
    for t in range(MAX_ROW_TILE, 0, -SUBLANES):
        if n % t == 0:
            return t
    raise ValueError(f"no row tile for {n} rows")


def _const_spec(shape):
    zeros = (0,) * len(shape)
    return pl.BlockSpec(shape, lambda *_: zeros, pipeline_mode=pl.Buffered(1))


def _params(sem):
    return pltpu.CompilerParams(dimension_semantics=sem, vmem_limit_bytes=VMEM_LIMIT)


def _rms(x, g):
    inv = lax.rsqrt(jnp.mean(x * x, axis=-1, keepdims=True) + RMS_EPS)
    return (x * inv) * g


def _gelu_tanh(x):
    c = np.float32(np.sqrt(2.0 / np.pi))
    return x * (0.5 * (1.0 + jnp.tanh(c * (x + np.float32(0.044715) * (x * x * x)))))


def _ffn_kernel(h_ref, g_ref, wg_ref, wu_ref, wd_ref, gf_ref, o_ref, *, final):
    x = h_ref[...]
    xn = _rms(x, g_ref[...]).astype(BF16)
    gate = jnp.dot(xn, wg_ref[...], preferred_element_type=F32)
    up = jnp.dot(xn, wu_ref[...], preferred_element_type=F32)
    act = (gate * jax.nn.sigmoid(gate) * up).astype(BF16)
    down = jnp.dot(act, wd_ref[...], preferred_element_type=F32)
    y = x + 0.5 * down
    o_ref[...] = _rms(y, gf_ref[...]) if final else y


def _ffn_specs(d, dff):
    return [_const_spec((1, d)), _const_spec((d, dff)), _const_spec((d, dff)), _const_spec((dff, d)),
            _const_spec((1, d))]


def _ffn(h, g, wg, wu, wd):
    n, d = h.shape
    tm = _row_tile(n)
    row = pl.BlockSpec((tm, d), lambda i: (i, 0))
    return pl.pallas_call(
        functools.partial(_ffn_kernel, final=False),
        grid=(n // tm,),
        in_specs=[row] + _ffn_specs(d, wg.shape[1]),
        out_specs=row,
        out_shape=jax.ShapeDtypeStruct((n, d), F32),
        input_output_aliases={0: 0},
        compiler_params=_params(("parallel",)),
        name="ffn",
    )(h, g, wg, wu, wd, g)


def _grid_rows_spec(t, seq, tm, d):
    assert seq % N_META == 0 and tm % N_META == 0
    return pl.BlockSpec((pl.Element(tm), pl.Element(d)),
                        lambda b, j: (pl.multiple_of(b * seq + N_META + j * tm, N_META), 0))


def _meta_rows_kernel(m_ref, g_ref, wg_ref, wu_ref, wd_ref, gf_ref, h_ref, o_ref, done_ref):
    del h_ref

    @pl.when(pl.program_id(0) == 0)
    def _():
        _ffn_kernel(m_ref, g_ref, wg_ref, wu_ref, wd_ref, gf_ref, done_ref, final=False)

    o_ref[...] = done_ref[...]


def _ffn_embed(x, meta_tokens, g, wg, wu, wd):
    batch, t, d = x.shape
    seq = N_META + t
    tm = _row_tile(t)
    weights = _ffn_specs(d, wg.shape[1])
    h = pl.pallas_call(
        functools.partial(_ffn_kernel, final=False),
        grid=(batch, t // tm),
        in_specs=[pl.BlockSpec((tm, d), lambda b, j: (b * (t // tm) + j, 0))] + weights,
        out_specs=_grid_rows_spec(t, seq, tm, d),
        out_shape=jax.ShapeDtypeStruct((batch * seq, d), F32),
        compiler_params=_params(("parallel", "parallel")),
        name="ffn_embed",
    )(x.reshape(batch * t, d), g, wg, wu, wd, g)
    return pl.pallas_call(
        _meta_rows_kernel,
        grid=(batch,),
        in_specs=[_const_spec((N_META, d))] + weights + [pl.BlockSpec(memory_space=pl.ANY)],
        out_specs=pl.BlockSpec((N_META, d), lambda b: (b * (seq // N_META), 0)),
        out_shape=jax.ShapeDtypeStruct((batch * seq, d), F32),
        scratch_shapes=[pltpu.VMEM((N_META, d), F32)],
        input_output_aliases={6: 0},
        compiler_params=_params(("arbitrary",)),
        name="ffn_meta",
    )(meta_tokens.astype(F32), g, wg, wu, wd, g, h)


def _ffn_final(h, g, wg, wu, wd, final_g, batch, t):
    n, d = h.shape
    seq = n // batch
    tm = _row_tile(t)
    out = pl.pallas_call(
        functools.partial(_ffn_kernel, final=True),
        grid=(batch, t // tm),
        in_specs=[_grid_rows_spec(t, seq, tm, d)] + _ffn_specs(d, wg.shape[1]),
        out_specs=pl.BlockSpec((tm, d), lambda b, j: (b * (t // tm) + j, 0)),
        out_shape=jax.ShapeDtypeStruct((batch * t, d), F32),
        compiler_params=_params(("parallel", "parallel")),
        name="ffn_final",
    )(h, g, wg, wu, wd, final_g)
    return out.reshape(batch, t, d)


def _seq_tile(seq):
    best = None
    for tq in range(16, MAX_ROW_TILE // SCAN_GROUP + 1, 16):
        if seq % tq == 0:
            best = tq
    assert best is not None, seq
    return best


def _inproj_kernel(h_ref, g_ref, w_ref, q_ref, k_ref, v_ref, xr_ref, yr_ref, gna_ref, glru_ref):
    nseq, tq, d = h_ref.shape
    xn = _rms(h_ref[...].reshape(nseq * tq, d), g_ref[...]).astype(BF16)

    def proj(j):
        return jnp.dot(xn, w_ref[:, j * d:(j + 1) * d], preferred_element_type=F32)

    zq = (proj(0) * np.float32(NA_HEAD_DIM ** -0.5)).astype(BF16)
    zk = proj(1).astype(BF16)
    zv = proj(2).astype(BF16)
    for p in range(N_PAIRS):
        sl = slice(p * LANES, (p + 1) * LANES)
        for j in range(nseq):
            rows = slice(j * tq, (j + 1) * tq)
            q_ref[p, j] = zq[rows, sl]
            k_ref[p, j] = zk[rows, sl]
            v_ref[p, j] = zv[rows, sl]
    xr = proj(3)
    for s in range(d // LANES):
        for j in range(nseq):
            xr_ref[s, pl.ds(j, tq, stride=nseq), :] = xr[j * tq:(j + 1) * tq, s * LANES:(s + 1) * LANES]
    yr_ref[...] = proj(4).reshape(nseq, tq, d)
    gna_ref[...] = proj(5).reshape(nseq, tq, d)
    glru_ref[...] = proj(6).reshape(nseq, tq, d)


def _inproj(h, g, w_in, groups, seq):
    n, d = h.shape
    batch = groups * SCAN_GROUP
    tq = _seq_tile(seq)
    n_slabs = d // LANES
    tile = pl.BlockSpec((SCAN_GROUP, tq, d), lambda g_, i: (g_, i, 0))
    pair = pl.BlockSpec((N_PAIRS, SCAN_GROUP, tq, LANES), lambda g_, i: (0, g_, i, 0))
    slab = pl.BlockSpec((n_slabs, None, tq * SCAN_GROUP, LANES), lambda g_, i: (0, g_, i, 0))
    pair_shape = jax.ShapeDtypeStruct((N_PAIRS, batch, seq, LANES), BF16)
    tile_shape = jax.ShapeDtypeStruct((batch, seq, d), F32)
    slab_shape = jax.ShapeDtypeStruct((n_slabs, groups, seq * SCAN_GROUP, LANES), F32)
    return pl.pallas_call(
        _inproj_kernel,
        grid=(groups, seq // tq),
        in_specs=[tile, _const_spec((1, d)), _const_spec(w_in.shape)],
        out_specs=[pair, pair, pair, slab, tile, tile, tile],
        out_shape=[pair_shape, pair_shape, pair_shape, slab_shape, tile_shape, tile_shape, tile_shape],
        compiler_params=_params(("parallel", "parallel")),
        name="inproj",
    )(h.reshape(batch, seq, d), g, w_in)


def _attn_block_cases(rows):
    nb = rows // ATTN_QROWS
    assert rows % ATTN_QROWS == 0 and nb >= 3 and nb % 2 == 0, rows

    def case(m):
        ks = min(max(ATTN_QROWS * m - NA_KH // 2, 0), rows - ATTN_KROWS)
        out = []
        for q in range(ATTN_QROWS):
            r = ATTN_QROWS * m + q
            rs = min(max(r - NA_KH // 2, 0), rows - NA_KH)
            out.append((rs - ks, r - rs))
        return tuple(out)

    assert all(case(m) == case(1) for m in range(1, nb - 1))
    return case(0), case(1), case(nb - 1)


def _attn_bias_table(rel_bias, rows):
    c = np.arange(GRID_W)
    cs = np.clip(c - NA_KW // 2, 0, GRID_W - NA_KW)
    kc = np.arange(GRID_W)
    ok = (kc[None, :] >= cs[:, None]) & (kc[None, :] < cs[:, None] + NA_KW)
    dcol = kc[None, :] - c[:, None] + NA_KW - 1
    onehot = (dcol[:, :, None] == np.arange(2 * NA_KW - 1)).astype(np.float32)
    cols = jnp.einsum('hdj,ckj->hdck', rel_bias.astype(F32), onehot, precision=lax.Precision.HIGHEST)
    cols = jnp.where(ok[None, None], cols, np.float32(MASK_VALUE))
    mask = lambda n: jnp.full((NA_HEADS, n, GRID_W, GRID_W), np.float32(MASK_VALUE))
    types = []
    for cases in _attn_block_cases(rows):
        per_q = []
        for off, e in cases:
            band = cols[:, NA_KH - 1 - e:2 * NA_KH - 1 - e]
            per_q.append(jnp.concatenate([mask(off), band, mask(ATTN_KROWS - NA_KH - off)], axis=1))
        types.append(jnp.stack(per_q, axis=1))
    tbl = jnp.stack(types, axis=1)
    tbl = jnp.transpose(tbl, (0, 1, 2, 4, 3, 5))
    tbl = tbl.reshape(N_PAIRS, 2, 3, ATTN_QROWS * GRID_W, ATTN_KLOC)
    tbl = jnp.transpose(tbl, (0, 2, 1, 3, 4)).reshape(N_PAIRS, 3, 2 * ATTN_QROWS * GRID_W, ATTN_KLOC)
    lead = tbl.shape[:3]
    return jnp.concatenate([tbl, jnp.zeros(lead + (N_META,), F32),
                            jnp.full(lead + (ATTN_KEXT - ATTN_KLOC - N_META,), np.float32(MASK_VALUE))], axis=-1)


def _attn_kernel(q_ref, k_ref, v_ref, bias_ref, o_ref, kx_ref, vx_ref, s_ref, p_ref, den_ref):
    seq = q_ref.shape[0]
    rows = (seq - N_META) // GRID_W
    nb = rows // ATTN_QROWS
    nq = ATTN_QROWS * GRID_W
    nt = (((1,), (1,)), ((), ()))
    lo = lax.broadcasted_iota(jnp.int32, (1, LANES), 1) < NA_HEAD_DIM

    def stack_heads(x):
        zero = jnp.zeros_like(x)
        return jnp.concatenate([jnp.where(lo, x, zero), jnp.where(lo, zero, x)], axis=0)

    def pick_heads(o, n):
        return jnp.where(lo, o[:n], o[n:])

    km = k_ref[0:N_META, :]
    vm = v_ref[0:N_META, :]

    qm = stack_heads(q_ref[0:N_META, :])
    sm = lax.dot_general(qm, km, nt, preferred_element_type=F32)
    em = jnp.exp(sm - jnp.max(sm, axis=-1, keepdims=True))
    om = jnp.dot(em.astype(BF16), vm, preferred_element_type=F32)
    om = om / jnp.sum(em, axis=-1, keepdims=True)
    o_ref[0:N_META, :] = pick_heads(om, N_META).astype(o_ref.dtype)

    pad = jnp.zeros((ATTN_KEXT - ATTN_KLOC - N_META, LANES), BF16)
    for j in range(2):
        kx_ref[j, ATTN_KLOC:ATTN_KLOC + N_META, :] = km
        vx_ref[j, ATTN_KLOC:ATTN_KLOC + N_META, :] = vm
        kx_ref[j, ATTN_KLOC + N_META:, :] = pad
        vx_ref[j, ATTN_KLOC + N_META:, :] = pad

    def key_start(m):
        ks = jnp.clip(ATTN_QROWS * m - NA_KH // 2, 0, rows - ATTN_KROWS)
        return pl.multiple_of(N_META + ks * GRID_W, 16)

    def query_start(m):
        return pl.multiple_of(N_META + m * nq, 16)

    def scores(m, j):
        kx_ref[j, 0:ATTN_KLOC, :] = k_ref[pl.ds(key_start(m), ATTN_KLOC), :]
        qs = stack_heads(q_ref[pl.ds(query_start(m), nq), :])
        kind = jnp.where(m == 0, 0, jnp.where(m == nb - 1, 2, 1))
        s_ref[j] = lax.dot_general(qs, kx_ref[j], nt, preferred_element_type=F32) + bias_ref[kind]

    def softmax(j):
        s = s_ref[j]
        e = jnp.exp(s - jnp.max(s, axis=-1, keepdims=True))
        den_ref[j] = jnp.sum(e, axis=-1, keepdims=True)
        p_ref[j] = e.astype(BF16)

    def values(m, j):
        vx_ref[j, 0:ATTN_KLOC, :] = v_ref[pl.ds(key_start(m), ATTN_KLOC), :]
        o = jnp.dot(p_ref[j], vx_ref[j], preferred_element_type=F32) / den_ref[j]
        o_ref[pl.ds(query_start(m), nq), :] = pick_heads(o, nq).astype(o_ref.dtype)

    scores(0, 0)
    scores(1, 1)
    softmax(0)

    def two_blocks(i, carry):
        m = 2 * i + 2
        scores(m, 0)
        softmax(1)
        values(m - 2, 0)
        scores(m + 1, 1)
        softmax(0)
        values(m - 1, 1)
        return carry

    lax.fori_loop(0, nb // 2 - 1, two_blocks, 0)
    softmax(1)
    values(nb - 2, 0)
    values(nb - 1, 1)


def _attention(q, k, v, bias_tbl):
    _, batch, seq, _ = q.shape
    blk = pl.BlockSpec((None, None, seq, LANES), lambda p, b: (p, b, 0, 0))
    bias_spec = pl.BlockSpec((None,) + bias_tbl.shape[1:], lambda p, b: (p, 0, 0, 0))
    out = pl.pallas_call(
        _attn_kernel,
        grid=(N_PAIRS, batch),
        in_specs=[blk, blk, blk, bias_spec],
        out_specs=blk,
        out_shape=jax.ShapeDtypeStruct(q.shape, BF16),
        scratch_shapes=[pltpu.VMEM((2, ATTN_KEXT, LANES), BF16), pltpu.VMEM((2, ATTN_KEXT, LANES), BF16),
                        pltpu.VMEM((2, 2 * ATTN_QROWS * GRID_W, ATTN_KEXT), F32),
                        pltpu.VMEM((2, 2 * ATTN_QROWS * GRID_W, ATTN_KEXT), BF16),
                        pltpu.VMEM((2, 2 * ATTN_QROWS * GRID_W, 1), F32)],
        compiler_params=_params(("parallel", "parallel")),
        name="attention",
    )(q, k, v, bias_tbl)
    return out


def _scan_time_block(seq):
    best = None
    for tt in range(LRU_TC, LRU_MAX_TT + 1, LRU_TC):
        if seq % tt == 0:
            best = tt
    assert best is not None, seq
    return best


def _lru_kernel(xl_ref, x_ref, xn_ref, cw_ref, cb_ref, w_ref, ba_ref, bx_ref, lam_ref, o_ref,
                x_s, carry, *, nt, reverse):
    step = pl.program_id(2)
    n_slabs, n_rows, _ = x_ref.shape
    blk = (nt - 1 - step) if reverse else step
    has_left = blk > 0
    has_right = blk < nt - 1
    left_rows = xl_ref.shape[1]
    chunk_rows = LRU_TC * SCAN_GROUP
    n_chunks = n_rows // chunk_rows

    @pl.when(step == 0)
    def _():
        carry[...] = jnp.zeros_like(carry)

    for s in range(n_slabs):
        sl = slice(s * LANES, (s + 1) * LANES)
        x_s[0:left_rows, sl] = jnp.where(has_left, xl_ref[s], jnp.zeros_like(xl_ref[s]))
        x_s[left_rows:left_rows + n_rows, sl] = x_ref[s]
        x_s[left_rows + n_rows:, sl] = jnp.where(has_right, xn_ref[s], jnp.zeros_like(xn_ref[s]))

    lam = lam_ref[...]
    softplus = jnp.maximum(-lam, 0.0) + jnp.log1p(jnp.exp(-jnp.abs(lam)))
    neg_c_sp = np.float32(-LRU_C) * softplus
    cw = cw_ref[...]
    cb = cb_ref[...]
    ba = ba_ref[...]
    bx = bx_ref[...]

    def chunk(i, h):
        c = (n_chunks - 1 - i) if reverse else i
        row0 = pl.multiple_of(c * chunk_rows, chunk_rows)
        xc = cb
        for j in range(CONV_W):
            xc = xc + x_s[pl.ds(row0 + j * SCAN_GROUP, chunk_rows), :] * cw[j:j + 1, :]
        pre = jnp.dot(xc.astype(BF16), w_ref[...], preferred_element_type=F32)
        r = jax.nn.sigmoid(pre[:, :LRU_CHUNK] + ba)
        gi = jax.nn.sigmoid(pre[:, LRU_CHUNK:] + bx)
        log_a = neg_c_sp * r
        a = jnp.exp(log_a)
        y = -jnp.tanh(log_a) * (a * a + 1.0)
        root = jnp.where(y > 0.0, y * lax.rsqrt(y), 0.0)
        u = root * (gi * xc)
        hs = [None] * LRU_TC
        order = range(LRU_TC - 1, -1, -1) if reverse else range(LRU_TC)
        for k in order:
            rows = slice(k * SCAN_GROUP, (k + 1) * SCAN_GROUP)
            h = a[rows] * h + u[rows]
            hs[k] = h
        hc = jnp.concatenate(hs, axis=0)
        for s in range(n_slabs):
            o_ref[s, pl.ds(row0, chunk_rows), :] = hc[:, s * LANES:(s + 1) * LANES]
        return h

    carry[...] = lax.fori_loop(0, n_chunks, chunk, carry[...], unroll=4)


def _lru(xr, conv_w, conv_b, w_gate, ba, bx, lam, seq, reverse):
    n_slabs, groups, _, _ = xr.shape
    c = n_slabs * LANES
    tt = _scan_time_block(seq)
    nt = seq // tt
    n_chunks = c // LRU_CHUNK
    slabs = LRU_CHUNK // LANES
    d = 1 if reverse else 0
    left_steps = CONV_LEFT
    right_steps = CONV_W - 1 - CONV_LEFT
    assert tt % left_steps == 0 and right_steps == 1

    def tblk(i):
        return (nt - 1 - i) if reverse else i

    cur = pl.BlockSpec((slabs, None, tt * SCAN_GROUP, LANES), lambda g, ch, i: (ch, g, tblk(i), 0))
    left = pl.BlockSpec((slabs, None, left_steps * SCAN_GROUP, LANES),
                        lambda g, ch, i: (ch, g, jnp.maximum(tblk(i) * (tt // left_steps) - 1, 0), 0))
    right = pl.BlockSpec((slabs, None, right_steps * SCAN_GROUP, LANES),
                         lambda g, ch, i: (ch, g, jnp.minimum((tblk(i) + 1) * tt, seq - 1), 0))
    per_ch = lambda rows: pl.BlockSpec((rows, LRU_CHUNK), lambda g, ch, i: (0, ch))
    per_dir = pl.BlockSpec((None, 1, LRU_CHUNK), lambda g, ch, i: (d, 0, ch))
    return pl.pallas_call(
        functools.partial(_lru_kernel, nt=nt, reverse=reverse),
        grid=(groups, n_chunks, nt),
        in_specs=[left, cur, right, per_ch(CONV_W), per_ch(1),
                  pl.BlockSpec((None, None, LRU_CHUNK, 2 * LRU_CHUNK), lambda g, ch, i: (d, ch, 0, 0)),
                  per_dir, per_dir, per_dir],
        out_specs=cur,
        out_shape=jax.ShapeDtypeStruct(xr.shape, F32),
        scratch_shapes=[pltpu.VMEM(((tt + CONV_W - 1) * SCAN_GROUP, LRU_CHUNK), F32),
                        pltpu.VMEM((SCAN_GROUP, LRU_CHUNK), F32)],
        compiler_params=_params(("parallel", "parallel", "arbitrary")),
        name="lru_bwd" if reverse else "lru_fwd",
    )(xr, xr, xr, conv_w, conv_b.reshape(1, c), w_gate,
      ba.reshape(2, 1, c), bx.reshape(2, 1, c), lam.reshape(2, 1, c))


def _lru_gate_weights(wa, wx):
    per = LRU_CHUNK // LRU_BLOCK_DIM

    def dense(w):
        n_dir, n_blk, bd, _ = w.shape
        w = w.reshape(n_dir, n_blk // per, per, bd, bd)
        eye = jnp.eye(per, dtype=w.dtype)
        full = jnp.einsum('dcpij,pq->dcpiqj', w, eye)
        return full.reshape(n_dir, n_blk // per, LRU_CHUNK, LRU_CHUNK)

    return jnp.concatenate([dense(wa), dense(wx)], axis=-1).astype(BF16)


def _mixout_kernel(h_ref, na_ref, yr_ref, gna_ref, glru_ref, hf_ref, hb_ref, wn_ref, wl_ref, wo_ref, o_ref):
    nseq, tq, d = h_ref.shape
    rows = lambda ref: ref[...].reshape(nseq * tq, d)

    def scan_rows(ref):
        return jnp.concatenate(
            [jnp.concatenate([ref[s, pl.ds(j, tq, stride=nseq), :] for j in range(nseq)], axis=0)
             for s in range(d // LANES)], axis=-1)

    na = jnp.concatenate(
        [jnp.concatenate([na_ref[p, j] for j in range(nseq)], axis=0) for p in range(N_PAIRS)], axis=-1)
    na_p = jnp.dot(na, wn_ref[...], preferred_element_type=F32)
    hr = scan_rows(hf_ref) + scan_rows(hb_ref)
    lru_in = (_gelu_tanh(rows(yr_ref)) * hr).astype(BF16)
    lru_p = jnp.dot(lru_in, wl_ref[...], preferred_element_type=F32)
    merged = jax.nn.sigmoid(rows(gna_ref)) * na_p + jax.nn.sigmoid(rows(glru_ref)) * lru_p
    out = rows(h_ref) + jnp.dot(merged.astype(BF16), wo_ref[...], preferred_element_type=F32)
    o_ref[...] = out.reshape(nseq, tq, d)


def _mixout(h, na, yr, gna, glru, hf, hb, wn, wl, wo):
    n, d = h.shape
    _, batch, seq, _ = na.shape
    groups = batch // SCAN_GROUP
    tq = _seq_tile(seq)
    n_slabs = d // LANES
    tile = pl.BlockSpec((SCAN_GROUP, tq, d), lambda g_, i: (g_, i, 0))
    pair = pl.BlockSpec((N_PAIRS, SCAN_GROUP, tq, LANES), lambda g_, i: (0, g_, i, 0))
    slab = pl.BlockSpec((n_slabs, None, tq * SCAN_GROUP, LANES), lambda g_, i: (0, g_, i, 0))
    out = pl.pallas_call(
        _mixout_kernel,
        grid=(groups, seq // tq),
        in_specs=[tile, pair, tile, tile, tile, slab, slab,
                  _const_spec((d, d)), _const_spec((d, d)), _const_spec((d, d))],
        out_specs=tile,
        out_shape=jax.ShapeDtypeStruct((batch, seq, d), F32),
        input_output_aliases={0: 0},
        compiler_params=_params(("parallel", "parallel")),
        name="mixout",
    )(h.reshape(batch, seq, d), na, yr, gna, glru, hf, hb, wn, wl, wo)
    return out.reshape(n, d)


def _prepare_layer(l, p, rows):
    d = D_MODEL
    row = lambda a: a[l].reshape(1, d).astype(F32)
    return dict(
        norm_ffn1=row(p["norm_ffn1"]), norm_mix=row(p["norm_mix"]), norm_ffn2=row(p["norm_ffn2"]),
        ffn1=(p["ffn1_w_gate"][l].astype(BF16), p["ffn1_w_up"][l].astype(BF16), p["ffn1_w_down"][l].astype(BF16)),
        ffn2=(p["ffn2_w_gate"][l].astype(BF16), p["ffn2_w_up"][l].astype(BF16), p["ffn2_w_down"][l].astype(BF16)),
        w_in=p["w_in"][l].astype(BF16),
        bias_tbl=_attn_bias_table(p["na_rel_bias"][l], rows),
        conv_w=p["conv_w"][l].astype(F32), conv_b=p["conv_b"][l].astype(F32),
        w_gate=_lru_gate_weights(p["lru_wa"][l], p["lru_wx"][l]),
        ba=p["lru_ba"][l].astype(F32), bx=p["lru_bx"][l].astype(F32), lam=p["lru_lambda"][l].astype(F32),
        w_na_proj=p["w_na_proj"][l].astype(BF16), w_lru_proj=p["w_lru_proj"][l].astype(BF16),
        w_out=p["w_out"][l].astype(BF16),
    )


def _encode(x, meta_tokens, layers, final_norm):
    batch, t, d = x.shape
    seq = N_META + t
    assert batch % SCAN_GROUP == 0 and seq % N_META == 0
    groups = batch // SCAN_GROUP
    gf = final_norm.reshape(1, d).astype(F32)
    h = None
    for li, lp in enumerate(layers):
        if li == 0:
            h = _ffn_embed(x, meta_tokens, lp["norm_ffn1"], *lp["ffn1"])
        else:
            h = _ffn(h, lp["norm_ffn1"], *lp["ffn1"])
        q, k, v, xr, yr, gna, glru = _inproj(h, lp["norm_mix"], lp["w_in"], groups, seq)
        na = _attention(q, k, v, lp["bias_tbl"])
        lru_args = (xr, lp["conv_w"], lp["conv_b"], lp["w_gate"], lp["ba"], lp["bx"], lp["lam"], seq)
        hf = _lru(*lru_args, reverse=False)
        hb = _lru(*lru_args, reverse=True)
        h = _mixout(h, na, yr, gna, glru, hf, hb, lp["w_na_proj"], lp["w_lru_proj"], lp["w_out"])
        if li < len(layers) - 1:
            h = _ffn(h, lp["norm_ffn2"], *lp["ffn2"])
    return _ffn_final(h, layers[-1]["norm_ffn2"], *layers[-1]["ffn2"], gf, batch, t)


def kernel(x_prompt, x_sample, meta_tokens, norm_ffn1, ffn1_w_gate, ffn1_w_up, ffn1_w_down, norm_mix, w_in, na_rel_bias, conv_w, conv_b, lru_wa, lru_ba, lru_wx, lru_bx, lru_lambda, w_na_proj, w_lru_proj, w_out, norm_ffn2, ffn2_w_gate, ffn2_w_up, ffn2_w_down, final_norm):
    p = dict(norm_ffn1=norm_ffn1, ffn1_w_gate=ffn1_w_gate, ffn1_w_up=ffn1_w_up, ffn1_w_down=ffn1_w_down,
             norm_mix=norm_mix, w_in=w_in, na_rel_bias=na_rel_bias, conv_w=conv_w, conv_b=conv_b,
             lru_wa=lru_wa, lru_ba=lru_ba, lru_wx=lru_wx, lru_bx=lru_bx, lru_lambda=lru_lambda,
             w_na_proj=w_na_proj, w_lru_proj=w_lru_proj, w_out=w_out, norm_ffn2=norm_ffn2,
             ffn2_w_gate=ffn2_w_gate, ffn2_w_up=ffn2_w_up, ffn2_w_down=ffn2_w_down)
    rows = x_prompt.shape[1] // GRID_W
    assert _attn_block_cases(rows) == _attn_block_cases(x_sample.shape[1] // GRID_W)
    layers = [_prepare_layer(l, p, rows) for l in range(norm_ffn1.shape[0])]
    y_prompt = _encode(x_prompt, meta_tokens, layers, final_norm)
    y_sample = _encode(x_sample, meta_tokens, layers, final_norm)
    return (y_prompt, y_sample)
```

```python
import functools

import numpy as np
import jax
import jax.numpy as jnp
from jax import lax
from jax.experimental import pallas as pl
from jax.experimental.pallas import tpu as pltpu

F32 = jnp.float32
BF16 = jnp.bfloat16

D_MODEL = 1024
N_META = 16
GRID_W = 64
NA_HEADS = 16
NA_HEAD_DIM = 64
NA_KH = 8
NA_KW = 16
LRU_BLOCK_DIM = 64
CONV_W = 4
CONV_LEFT = 2
LRU_C = 8.0
RMS_EPS = 1e-6
MASK_VALUE = -1e30

LANES = 128
SUBLANES = 8
N_PAIRS = NA_HEADS // 2
ATTN_QROWS = 4
ATTN_KROWS = ATTN_QROWS + NA_KH - 1
ATTN_KLOC = ATTN_KROWS * GRID_W
ATTN_KEXT = 768
LRU_CHUNK = 256
SCAN_GROUP = SUBLANES
LRU_TC = 16
LRU_MAX_TT = 704
VMEM_LIMIT = 56 * 1024 * 1024
MAX_ROW_TILE = 512


def _row_tile(n):
    for t in range(MAX_ROW_TILE, 0, -SUBLANES):
        if n % t == 0:
            return t
    raise ValueError(f"no row tile for {n} rows")


def _const_spec(shape):
    zeros = (0,) * len(shape)
    return pl.BlockSpec(shape, lambda *_: zeros, pipeline_mode=pl.Buffered(1))


def _params(sem):
    return pltpu.CompilerParams(dimension_semantics=sem, vmem_limit_bytes=VMEM_LIMIT)


def _rms(x, g):
    inv = lax.rsqrt(jnp.mean(x * x, axis=-1, keepdims=True) + RMS_EPS)
    return (x * inv) * g


def _gelu_tanh(x):
    c = np.float32(np.sqrt(2.0 / np.pi))
    return x * (0.5 * (1.0 + jnp.tanh(c * (x + np.float32(0.044715) * (x * x * x)))))


def _ffn_kernel(h_ref, g_ref, wg_ref, wu_ref, wd_ref, gf_ref, o_ref, *, final):
    x = h_ref[...]
    xn = _rms(x, g_ref[...]).astype(BF16)
    gate = jnp.dot(xn, wg_ref[...], preferred_element_type=F32)
    up = jnp.dot(xn, wu_ref[...], preferred_element_type=F32)
    act = (gate * jax.nn.sigmoid(gate) * up).astype(BF16)
    down = jnp.dot(act, wd_ref[...], preferred_element_type=F32)
    y = x + 0.5 * down
    o_ref[...] = _rms(y, gf_ref[...]) if final else y


def _ffn_specs(d, dff):
    return [_const_spec((1, d)), _const_spec((d, dff)), _const_spec((d, dff)), _const_spec((dff, d)),
            _const_spec((1, d))]


def _ffn(h, g, wg, wu, wd):
    n, d = h.shape
    tm = _row_tile(n)
    row = pl.BlockSpec((tm, d), lambda i: (i, 0))
    return pl.pallas_call(
        functools.partial(_ffn_kernel, final=False),
        grid=(n // tm,),
        in_specs=[row] + _ffn_specs(d, wg.shape[1]),
        out_specs=row,
        out_shape=jax.ShapeDtypeStruct((n, d), F32),
        input_output_aliases={0: 0},
        compiler_params=_params(("parallel",)),
        name="ffn",
    )(h, g, wg, wu, wd, g)


def _grid_rows_spec(t, seq, tm, d):
    assert seq % N_META == 0 and tm % N_META == 0
    return pl.BlockSpec((pl.Element(tm), pl.Element(d)),
                        lambda b, j: (pl.multiple_of(b * seq + N_META + j * tm, N_META), 0))


def _meta_rows_kernel(m_ref, g_ref, wg_ref, wu_ref, wd_ref, gf_ref, h_ref, o_ref, done_ref):
    del h_ref

    @pl.when(pl.program_id(0) == 0)
    def _():
        _ffn_kernel(m_ref, g_ref, wg_ref, wu_ref, wd_ref, gf_ref, done_ref, final=False)

    o_ref[...] = done_ref[...]


def _ffn_embed(x, meta_tokens, g, wg, wu, wd):
    batch, t, d = x.shape
    seq = N_META + t
    tm = _row_tile(t)
    weights = _ffn_specs(d, wg.shape[1])
    h = pl.pallas_call(
        functools.partial(_ffn_kernel, final=False),
        grid=(batch, t // tm),
        in_specs=[pl.BlockSpec((tm, d), lambda b, j: (b * (t // tm) + j, 0))] + weights,
        out_specs=_grid_rows_spec(t, seq, tm, d),
        out_shape=jax.ShapeDtypeStruct((batch * seq, d), F32),
        compiler_params=_params(("parallel", "parallel")),
        name="ffn_embed",
    )(x.reshape(batch * t, d), g, wg, wu, wd, g)
    return pl.pallas_call(
        _meta_rows_kernel,
        grid=(batch,),
        in_specs=[_const_spec((N_META, d))] + weights + [pl.BlockSpec(memory_space=pl.ANY)],
        out_specs=pl.BlockSpec((N_META, d), lambda b: (b * (seq // N_META), 0)),
        out_shape=jax.ShapeDtypeStruct((batch * seq, d), F32),
        scratch_shapes=[pltpu.VMEM((N_META, d), F32)],
        input_output_aliases={6: 0},
        compiler_params=_params(("arbitrary",)),
        name="ffn_meta",
    )(meta_tokens.astype(F32), g, wg, wu, wd, g, h)


def _ffn_final(h, g, wg, wu, wd, final_g, batch, t):
    n, d = h.shape
    seq = n // batch
    tm = _row_tile(t)
    out = pl.pallas_call(
        functools.partial(_ffn_kernel, final=True),
        grid=(batch, t // tm),
        in_specs=[_grid_rows_spec(t, seq, tm, d)] + _ffn_specs(d, wg.shape[1]),
        out_specs=pl.BlockSpec((tm, d), lambda b, j: (b * (t // tm) + j, 0)),
        out_shape=jax.ShapeDtypeStruct((batch * t, d), F32),
        compiler_params=_params(("parallel", "parallel")),
        name="ffn_final",
    )(h, g, wg, wu, wd, final_g)
    return out.reshape(batch, t, d)


def _seq_tile(seq):
    best = None
    for tq in range(16, MAX_ROW_TILE // SCAN_GROUP + 1, 16):
        if seq % tq == 0:
            best = tq
    assert best is not None, seq
    return best


def _inproj_kernel(h_ref, g_ref, w_ref, q_ref, k_ref, v_ref, xr_ref, yr_ref, gna_ref, glru_ref):
    nseq, tq, d = h_ref.shape
    xn = _rms(h_ref[...].reshape(nseq * tq, d), g_ref[...]).astype(BF16)

    def proj(j):
        return jnp.dot(xn, w_ref[:, j * d:(j + 1) * d], preferred_element_type=F32)

    zq = (proj(0) * np.float32(NA_HEAD_DIM ** -0.5)).astype(BF16)
    zk = proj(1).astype(BF16)
    zv = proj(2).astype(BF16)
    for p in range(N_PAIRS):
        sl = slice(p * LANES, (p + 1) * LANES)
        for j in range(nseq):
            rows = slice(j * tq, (j + 1) * tq)
            q_ref[p, j] = zq[rows, sl]
            k_ref[p, j] = zk[rows, sl]
            v_ref[p, j] = zv[rows, sl]
    xr = proj(3)
    for s in range(d // LANES):
        for j in range(nseq):
            xr_ref[s, pl.ds(j, tq, stride=nseq), :] = xr[j * tq:(j + 1) * tq, s * LANES:(s + 1) * LANES]
    yr_ref[...] = proj(4).reshape(nseq, tq, d)
    gna_ref[...] = proj(5).reshape(nseq, tq, d)
    glru_ref[...] = proj(6).reshape(nseq, tq, d)


def _inproj(h, g, w_in, groups, seq):
    n, d = h.shape
    batch = groups * SCAN_GROUP
    tq = _seq_tile(seq)
    n_slabs = d // LANES
    tile = pl.BlockSpec((SCAN_GROUP, tq, d), lambda g_, i: (g_, i, 0))
    pair = pl.BlockSpec((N_PAIRS, SCAN_GROUP, tq, LANES), lambda g_, i: (0, g_, i, 0))
    slab = pl.BlockSpec((n_slabs, None, tq * SCAN_GROUP, LANES), lambda g_, i: (0, g_, i, 0))
    pair_shape = jax.ShapeDtypeStruct((N_PAIRS, batch, seq, LANES), BF16)
    tile_shape = jax.ShapeDtypeStruct((batch, seq, d), F32)
    slab_shape = jax.ShapeDtypeStruct((n_slabs, groups, seq * SCAN_GROUP, LANES), F32)
    return pl.pallas_call(
        _inproj_kernel,
        grid=(groups, seq // tq),
        in_specs=[tile, _const_spec((1, d)), _const_spec(w_in.shape)],
        out_specs=[pair, pair, pair, slab, tile, tile, tile],
        out_shape=[pair_shape, pair_shape, pair_shape, slab_shape, tile_shape, tile_shape, tile_shape],
        compiler_params=_params(("parallel", "parallel")),
        name="inproj",
    )(h.reshape(batch, seq, d), g, w_in)


def _attn_block_cases(rows):
    nb = rows // ATTN_QROWS
    assert rows % ATTN_QROWS == 0 and nb >= 3 and nb % 2 == 0, rows

    def case(m):
        ks = min(max(ATTN_QROWS * m - NA_KH // 2, 0), rows - ATTN_KROWS)
        out = []
        for q in range(ATTN_QROWS):
            r = ATTN_QROWS * m + q
            rs = min(max(r - NA_KH // 2, 0), rows - NA_KH)
            out.append((rs - ks, r - rs))
        return tuple(out)

    assert all(case(m) == case(1) for m in range(1, nb - 1))
    return case(0), case(1), case(nb - 1)


def _attn_bias_table(rel_bias, rows):
    c = np.arange(GRID_W)
    cs = np.clip(c - NA_KW // 2, 0, GRID_W - NA_KW)
    kc = np.arange(GRID_W)
    ok = (kc[None, :] >= cs[:, None]) & (kc[None, :] < cs[:, None] + NA_KW)
    dcol = kc[None, :] - c[:, None] + NA_KW - 1
    onehot = (dcol[:, :, None] == np.arange(2 * NA_KW - 1)).astype(np.float32)
    cols = jnp.einsum('hdj,ckj->hdck', rel_bias.astype(F32), onehot, precision=lax.Precision.HIGHEST)
    cols = jnp.where(ok[None, None], cols, np.float32(MASK_VALUE))
    mask = lambda n: jnp.full((NA_HEADS, n, GRID_W, GRID_W), np.float32(MASK_VALUE))
    types = []
    for cases in _attn_block_cases(rows):
        per_q = []
        for off, e in cases:
            band = cols[:, NA_KH - 1 - e:2 * NA_KH - 1 - e]
            per_q.append(jnp.concatenate([mask(off), band, mask(ATTN_KROWS - NA_KH - off)], axis=1))
        types.append(jnp.stack(per_q, axis=1))
    tbl = jnp.stack(types, axis=1)
    tbl = jnp.transpose(tbl, (0, 1, 2, 4, 3, 5))
    tbl = tbl.reshape(N_PAIRS, 2, 3, ATTN_QROWS * GRID_W, ATTN_KLOC)
    tbl = jnp.transpose(tbl, (0, 2, 1, 3, 4)).reshape(N_PAIRS, 3, 2 * ATTN_QROWS * GRID_W, ATTN_KLOC)
    lead = tbl.shape[:3]
    return jnp.concatenate([tbl, jnp.zeros(lead + (N_META,), F32),
                            jnp.full(lead + (ATTN_KEXT - ATTN_KLOC - N_META,), np.float32(MASK_VALUE))], axis=-1)


def _attn_kernel(q_ref, k_ref, v_ref, bias_ref, o_ref, kx_ref, vx_ref, s_ref, p_ref, den_ref):
    seq = q_ref.shape[0]
    rows = (seq - N_META) // GRID_W
    nb = rows // ATTN_QROWS
    nq = ATTN_QROWS * GRID_W
    nt = (((1,), (1,)), ((), ()))
    lo = lax.broadcasted_iota(jnp.int32, (1, LANES), 1) < NA_HEAD_DIM

    def stack_heads(x):
        zero = jnp.zeros_like(x)
        return jnp.concatenate([jnp.where(lo, x, zero), jnp.where(lo, zero, x)], axis=0)

    def pick_heads(o, n):
        return jnp.where(lo, o[:n], o[n:])

    km = k_ref[0:N_META, :]
    vm = v_ref[0:N_META, :]

    qm = stack_heads(q_ref[0:N_META, :])
    sm = lax.dot_general(qm, km, nt, preferred_element_type=F32)
    em = jnp.exp(sm - jnp.max(sm, axis=-1, keepdims=True))
    om = jnp.dot(em.astype(BF16), vm, preferred_element_type=F32)
    om = om / jnp.sum(em, axis=-1, keepdims=True)
    o_ref[0:N_META, :] = pick_heads(om, N_META).astype(o_ref.dtype)

    pad = jnp.zeros((ATTN_KEXT - ATTN_KLOC - N_META, LANES), BF16)
    for j in range(2):
        kx_ref[j, ATTN_KLOC:ATTN_KLOC + N_META, :] = km
        vx_ref[j, ATTN_KLOC:ATTN_KLOC + N_META, :] = vm
        kx_ref[j, ATTN_KLOC + N_META:, :] = pad
        vx_ref[j, ATTN_KLOC + N_META:, :] = pad

    def key_start(m):
        ks = jnp.clip(ATTN_QROWS * m - NA_KH // 2, 0, rows - ATTN_KROWS)
        return pl.multiple_of(N_META + ks * GRID_W, 16)

    def query_start(m):
        return pl.multiple_of(N_META + m * nq, 16)

    def scores(m, j):
        kx_ref[j, 0:ATTN_KLOC, :] = k_ref[pl.ds(key_start(m), ATTN_KLOC), :]
        qs = stack_heads(q_ref[pl.ds(query_start(m), nq), :])
        kind = jnp.where(m == 0, 0, jnp.where(m == nb - 1, 2, 1))
        s_ref[j] = lax.dot_general(qs, kx_ref[j], nt, preferred_element_type=F32) + bias_ref[kind]

    def softmax(j):
        s = s_ref[j]
        e = jnp.exp(s - jnp.max(s, axis=-1, keepdims=True))
        den_ref[j] = jnp.sum(e, axis=-1, keepdims=True)
        p_ref[j] = e.astype(BF16)

    def values(m, j):
        vx_ref[j, 0:ATTN_KLOC, :] = v_ref[pl.ds(key_start(m), ATTN_KLOC), :]
        o = jnp.dot(p_ref[j], vx_ref[j], preferred_element_type=F32) / den_ref[j]
        o_ref[pl.ds(query_start(m), nq), :] = pick_heads(o, nq).astype(o_ref.dtype)

    scores(0, 0)
    scores(1, 1)
    softmax(0)

    def two_blocks(i, carry):
        m = 2 * i + 2
        scores(m, 0)
        softmax(1)
        values(m - 2, 0)
        scores(m + 1, 1)
        softmax(0)
        values(m - 1, 1)
        return carry

    lax.fori_loop(0, nb // 2 - 1, two_blocks, 0)
    softmax(1)
    values(nb - 2, 0)
    values(nb - 1, 1)


def _attention(q, k, v, bias_tbl):
    _, batch, seq, _ = q.shape
    blk = pl.BlockSpec((None, None, seq, LANES), lambda p, b: (p, b, 0, 0))
    bias_spec = pl.BlockSpec((None,) + bias_tbl.shape[1:], lambda p, b: (p, 0, 0, 0))
    out = pl.pallas_call(
        _attn_kernel,
        grid=(N_PAIRS, batch),
        in_specs=[blk, blk, blk, bias_spec],
        out_specs=blk,
        out_shape=jax.ShapeDtypeStruct(q.shape, BF16),
        scratch_shapes=[pltpu.VMEM((2, ATTN_KEXT, LANES), BF16), pltpu.VMEM((2, ATTN_KEXT, LANES), BF16),
                        pltpu.VMEM((2, 2 * ATTN_QROWS * GRID_W, ATTN_KEXT), F32),
                        pltpu.VMEM((2, 2 * ATTN_QROWS * GRID_W, ATTN_KEXT), BF16),
                        pltpu.VMEM((2, 2 * ATTN_QROWS * GRID_W, 1), F32)],
        compiler_params=_params(("parallel", "parallel")),
        name="attention",
    )(q, k, v, bias_tbl)
    return out


def _scan_time_block(seq):
    best = None
    for tt in range(LRU_TC, LRU_MAX_TT + 1, LRU_TC):
        if seq % tt == 0:
            best = tt
    assert best is not None, seq
    return best


def _lru_kernel(xl_ref, x_ref, xn_ref, cw_ref, cb_ref, w_ref, ba_ref, bx_ref, lam_ref, *rest, nt, reverse):
    acc_ref, o_ref, x_s, carry = rest if reverse else (None,) + rest
    step = pl.program_id(2)
    n_slabs, n_rows, _ = x_ref.shape
    blk = (nt - 1 - step) if reverse else step
    has_left = blk > 0
    has_right = blk < nt - 1
    left_rows = xl_ref.shape[1]
    chunk_rows = LRU_TC * SCAN_GROUP
    n_chunks = n_rows // chunk_rows

    @pl.when(step == 0)
    def _():
        carry[...] = jnp.zeros_like(carry)

    for s in range(n_slabs):
        sl = slice(s * LANES, (s + 1) * LANES)
        x_s[0:left_rows, sl] = jnp.where(has_left, xl_ref[s], jnp.zeros_like(xl_ref[s]))
        x_s[left_rows:left_rows + n_rows, sl] = x_ref[s]
        x_s[left_rows + n_rows:, sl] = jnp.where(has_right, xn_ref[s], jnp.zeros_like(xn_ref[s]))

    lam = lam_ref[...]
    softplus = jnp.maximum(-lam, 0.0) + jnp.log1p(jnp.exp(-jnp.abs(lam)))
    neg_c_sp = np.float32(-LRU_C) * softplus
    cw = cw_ref[...]
    cb = cb_ref[...]
    ba = ba_ref[...]
    bx = bx_ref[...]

    def chunk(i, h):
        c = (n_chunks - 1 - i) if reverse else i
        row0 = pl.multiple_of(c * chunk_rows, chunk_rows)
        xc = cb
        for j in range(CONV_W):
            xc = xc + x_s[pl.ds(row0 + j * SCAN_GROUP, chunk_rows), :] * cw[j:j + 1, :]
        pre = jnp.dot(xc.astype(BF16), w_ref[...], preferred_element_type=F32)
        r = jax.nn.sigmoid(pre[:, :LRU_CHUNK] + ba)
        gi = jax.nn.sigmoid(pre[:, LRU_CHUNK:] + bx)
        log_a = neg_c_sp * r
        a = jnp.exp(log_a)
        y = -jnp.tanh(log_a) * (a * a + 1.0)
        root = jnp.where(y > 0.0, y * lax.rsqrt(y), 0.0)
        u = root * (gi * xc)
        hs = [None] * LRU_TC
        order = range(LRU_TC - 1, -1, -1) if reverse else range(LRU_TC)
        for k in order:
            rows = slice(k * SCAN_GROUP, (k + 1) * SCAN_GROUP)
            h = a[rows] * h + u[rows]
            hs[k] = h
        hc = jnp.concatenate(hs, axis=0)
        for s in range(n_slabs):
            part = hc[:, s * LANES:(s + 1) * LANES]
            if reverse:
                part = part + acc_ref[s, pl.ds(row0, chunk_rows), :]
            o_ref[s, pl.ds(row0, chunk_rows), :] = part
        return h

    carry[...] = lax.fori_loop(0, n_chunks, chunk, carry[...], unroll=4)


def _lru(xr, conv_w, conv_b, w_gate, ba, bx, lam, seq, forward_out=None):
    reverse = forward_out is not None
    n_slabs, groups, _, _ = xr.shape
    c = n_slabs * LANES
    tt = _scan_time_block(seq)
    nt = seq // tt
    n_chunks = c // LRU_CHUNK
    slabs = LRU_CHUNK // LANES
    d = 1 if reverse else 0
    left_steps = CONV_LEFT
    right_steps = CONV_W - 1 - CONV_LEFT
    assert tt % left_steps == 0 and right_steps == 1

    def tblk(i):
        return (nt - 1 - i) if reverse else i

    cur = pl.BlockSpec((slabs, None, tt * SCAN_GROUP, LANES), lambda g, ch, i: (ch, g, tblk(i), 0))
    left = pl.BlockSpec((slabs, None, left_steps * SCAN_GROUP, LANES),
                        lambda g, ch, i: (ch, g, jnp.maximum(tblk(i) * (tt // left_steps) - 1, 0), 0))
    right = pl.BlockSpec((slabs, None, right_steps * SCAN_GROUP, LANES),
                         lambda g, ch, i: (ch, g, jnp.minimum((tblk(i) + 1) * tt, seq - 1), 0))
    per_ch = lambda rows: pl.BlockSpec((rows, LRU_CHUNK), lambda g, ch, i: (0, ch))
    per_dir = pl.BlockSpec((None, 1, LRU_CHUNK), lambda g, ch, i: (d, 0, ch))
    return pl.pallas_call(
        functools.partial(_lru_kernel, nt=nt, reverse=reverse),
        grid=(groups, n_chunks, nt),
        in_specs=[left, cur, right, per_ch(CONV_W), per_ch(1),
                  pl.BlockSpec((None, None, LRU_CHUNK, 2 * LRU_CHUNK), lambda g, ch, i: (d, ch, 0, 0)),
                  per_dir, per_dir, per_dir] + ([cur] if reverse else []),
        out_specs=cur,
        out_shape=jax.ShapeDtypeStruct(xr.shape, F32),
        input_output_aliases={9: 0} if reverse else {},
        scratch_shapes=[pltpu.VMEM(((tt + CONV_W - 1) * SCAN_GROUP, LRU_CHUNK), F32),
                        pltpu.VMEM((SCAN_GROUP, LRU_CHUNK), F32)],
        compiler_params=_params(("parallel", "parallel", "arbitrary")),
        name="lru_bwd" if reverse else "lru_fwd",
    )(xr, xr, xr, conv_w, conv_b.reshape(1, c), w_gate,
      ba.reshape(2, 1, c), bx.reshape(2, 1, c), lam.reshape(2, 1, c), *([forward_out] if reverse else []))


def _lru_gate_weights(wa, wx):
    per = LRU_CHUNK // LRU_BLOCK_DIM

    def dense(w):
        n_dir, n_blk, bd, _ = w.shape
        w = w.reshape(n_dir, n_blk // per, per, bd, bd)
        eye = jnp.eye(per, dtype=w.dtype)
        full = jnp.einsum('dcpij,pq->dcpiqj', w, eye)
        return full.reshape(n_dir, n_blk // per, LRU_CHUNK, LRU_CHUNK)

    return jnp.concatenate([dense(wa), dense(wx)], axis=-1).astype(BF16)


def _mixout_kernel(h_ref, na_ref, yr_ref, gna_ref, glru_ref, hr_ref, wn_ref, wl_ref, wo_ref, o_ref):
    nseq, tq, d = h_ref.shape
    rows = lambda ref: ref[...].reshape(nseq * tq, d)

    na = jnp.concatenate(
        [jnp.concatenate([na_ref[p, j] for j in range(nseq)], axis=0) for p in range(N_PAIRS)], axis=-1)
    na_p = jnp.dot(na, wn_ref[...], preferred_element_type=F32)
    hr = jnp.concatenate(
        [jnp.concatenate([hr_ref[s, pl.ds(j, tq, stride=nseq), :] for j in range(nseq)], axis=0)
         for s in range(d // LANES)], axis=-1)
    lru_in = (_gelu_tanh(rows(yr_ref)) * hr).astype(BF16)
    lru_p = jnp.dot(lru_in, wl_ref[...], preferred_element_type=F32)
    merged = jax.nn.sigmoid(rows(gna_ref)) * na_p + jax.nn.sigmoid(rows(glru_ref)) * lru_p
    out = rows(h_ref) + jnp.dot(merged.astype(BF16), wo_ref[...], preferred_element_type=F32)
    o_ref[...] = out.reshape(nseq, tq, d)


def _mixout(h, na, yr, gna, glru, hr, wn, wl, wo):
    n, d = h.shape
    _, batch, seq, _ = na.shape
    groups = batch // SCAN_GROUP
    tq = _seq_tile(seq)
    n_slabs = d // LANES
    tile = pl.BlockSpec((SCAN_GROUP, tq, d), lambda g_, i: (g_, i, 0))
    pair = pl.BlockSpec((N_PAIRS, SCAN_GROUP, tq, LANES), lambda g_, i: (0, g_, i, 0))
    slab = pl.BlockSpec((n_slabs, None, tq * SCAN_GROUP, LANES), lambda g_, i: (0, g_, i, 0))
    out = pl.pallas_call(
        _mixout_kernel,
        grid=(groups, seq // tq),
        in_specs=[tile, pair, tile, tile, tile, slab,
                  _const_spec((d, d)), _const_spec((d, d)), _const_spec((d, d))],
        out_specs=tile,
        out_shape=jax.ShapeDtypeStruct((batch, seq, d), F32),
        input_output_aliases={0: 0},
        compiler_params=_params(("parallel", "parallel")),
        name="mixout",
    )(h.reshape(batch, seq, d), na, yr, gna, glru, hr, wn, wl, wo)
    return out.reshape(n, d)


def _prepare_layer(l, p, rows):
    d = D_MODEL
    row = lambda a: a[l].reshape(1, d).astype(F32)
    return dict(
        norm_ffn1=row(p["norm_ffn1"]), norm_mix=row(p["norm_mix"]), norm_ffn2=row(p["norm_ffn2"]),
        ffn1=(p["ffn1_w_gate"][l].astype(BF16), p["ffn1_w_up"][l].astype(BF16), p["ffn1_w_down"][l].astype(BF16)),
        ffn2=(p["ffn2_w_gate"][l].astype(BF16), p["ffn2_w_up"][l].astype(BF16), p["ffn2_w_down"][l].astype(BF16)),
        w_in=p["w_in"][l].astype(BF16),
        bias_tbl=_attn_bias_table(p["na_rel_bias"][l], rows),
        conv_w=p["conv_w"][l].astype(F32), conv_b=p["conv_b"][l].astype(F32),
        w_gate=_lru_gate_weights(p["lru_wa"][l], p["lru_wx"][l]),
        ba=p["lru_ba"][l].astype(F32), bx=p["lru_bx"][l].astype(F32), lam=p["lru_lambda"][l].astype(F32),
        w_na_proj=p["w_na_proj"][l].astype(BF16), w_lru_proj=p["w_lru_proj"][l].astype(BF16),
        w_out=p["w_out"][l].astype(BF16),
    )


def _encode(x, meta_tokens, layers, final_norm):
    batch, t, d = x.shape
    seq = N_META + t
    assert batch % SCAN_GROUP == 0 and seq % N_META == 0
    groups = batch // SCAN_GROUP
    gf = final_norm.reshape(1, d).astype(F32)
    h = None
    for li, lp in enumerate(layers):
        if li == 0:
            h = _ffn_embed(x, meta_tokens, lp["norm_ffn1"], *lp["ffn1"])
        else:
            h = _ffn(h, lp["norm_ffn1"], *lp["ffn1"])
        q, k, v, xr, yr, gna, glru = _inproj(h, lp["norm_mix"], lp["w_in"], groups, seq)
        na = _attention(q, k, v, lp["bias_tbl"])
        lru_args = (xr, lp["conv_w"], lp["conv_b"], lp["w_gate"], lp["ba"], lp["bx"], lp["lam"], seq)
        hr = _lru(*lru_args, forward_out=_lru(*lru_args))
        h = _mixout(h, na, yr, gna, glru, hr, lp["w_na_proj"], lp["w_lru_proj"], lp["w_out"])
        if li < len(layers) - 1:
            h = _ffn(h, lp["norm_ffn2"], *lp["ffn2"])
    return _ffn_final(h, layers[-1]["norm_ffn2"], *layers[-1]["ffn2"], gf, batch, t)


def kernel(x_prompt, x_sample, meta_tokens, norm_ffn1, ffn1_w_gate, ffn1_w_up, ffn1_w_down, norm_mix, w_in, na_rel_bias, conv_w, conv_b, lru_wa, lru_ba, lru_wx, lru_bx, lru_lambda, w_na_proj, w_lru_proj, w_out, norm_ffn2, ffn2_w_gate, ffn2_w_up, ffn2_w_down, final_norm):
    p = dict(norm_ffn1=norm_ffn1, ffn1_w_gate=ffn1_w_gate, ffn1_w_up=ffn1_w_up, ffn1_w_down=ffn1_w_down,
             norm_mix=norm_mix, w_in=w_in, na_rel_bias=na_rel_bias, conv_w=conv_w, conv_b=conv_b,
             lru_wa=lru_wa, lru_ba=lru_ba, lru_wx=lru_wx, lru_bx=lru_bx, lru_lambda=lru_lambda,
             w_na_proj=w_na_proj, w_lru_proj=w_lru_proj, w_out=w_out, norm_ffn2=norm_ffn2,
             ffn2_w_gate=ffn2_w_gate, ffn2_w_up=ffn2_w_up, ffn2_w_down=ffn2_w_down)
    rows = x_prompt.shape[1] // GRID_W
    assert _attn_block_cases(rows) == _attn_block_cases(x_sample.shape[1] // GRID_W)
    layers = [_prepare_layer(l, p, rows) for l in range(norm_ffn1.shape[0])]
    y_prompt = _encode(x_prompt, meta_tokens, layers, final_norm)
    y_sample = _encode(x_sample, meta_tokens, layers, final_norm)
    return (y_prompt, y_sample)
```

```python
import functools

import numpy as np
import jax
import jax.numpy as jnp
from jax import lax
from jax.experimental import pallas as pl
from jax.experimental.pallas import tpu as pltpu

F32 = jnp.float32
BF16 = jnp.bfloat16

D_MODEL = 1024
N_META = 16
GRID_W = 64
NA_HEADS = 16
NA_HEAD_DIM = 64
NA_KH = 8
NA_KW = 16
LRU_BLOCK_DIM = 64
CONV_W = 4
CONV_LEFT = 2
LRU_C = 8.0
RMS_EPS = 1e-6
MASK_VALUE = -1e30

LANES = 128
SUBLANES = 8
N_PAIRS = NA_HEADS // 2
ATTN_QROWS = 4
ATTN_KROWS = ATTN_QROWS + NA_KH - 1
ATTN_KLOC = ATTN_KROWS * GRID_W
ATTN_KEXT = 768
LRU_CHUNK = 256
SCAN_GROUP = SUBLANES
LRU_TC = 16
LRU_MAX_TT = 704
VMEM_LIMIT = 56 * 1024 * 1024
MAX_ROW_TILE = 704
BF16_ROWS = 16


def _row_tile(n):
    for t in range(MAX_ROW_TILE, 0, -BF16_ROWS):
        if n % t == 0:
            return t
    raise ValueError(f"no row tile for {n} rows")


def _const_spec(shape):
    zeros = (0,) * len(shape)
    return pl.BlockSpec(shape, lambda *_: zeros, pipeline_mode=pl.Buffered(1))


def _params(sem):
    return pltpu.CompilerParams(dimension_semantics=sem, vmem_limit_bytes=VMEM_LIMIT)


def _rms(x, g):
    inv = lax.rsqrt(jnp.mean(x * x, axis=-1, keepdims=True) + RMS_EPS)
    return (x * inv) * g


def _gelu_tanh(x):
    c = np.float32(np.sqrt(2.0 / np.pi))
    return x * (0.5 * (1.0 + jnp.tanh(c * (x + np.float32(0.044715) * (x * x * x)))))


def _ffn_kernel(h_ref, g_ref, wg_ref, wu_ref, wd_ref, gf_ref, o_ref, *, final):
    x = h_ref[...]
    xn = _rms(x, g_ref[...]).astype(BF16)
    gate = jnp.dot(xn, wg_ref[...], preferred_element_type=F32)
    up = jnp.dot(xn, wu_ref[...], preferred_element_type=F32)
    act = (gate * jax.nn.sigmoid(gate) * up).astype(BF16)
    down = jnp.dot(act, wd_ref[...], preferred_element_type=F32)
    y = x + 0.5 * down
    o_ref[...] = _rms(y, gf_ref[...]) if final else y


def _ffn_specs(d, dff):
    return [_const_spec((1, d)), _const_spec((d, dff)), _const_spec((d, dff)), _const_spec((dff, d)),
            _const_spec((1, d))]


def _ffn(h, g, wg, wu, wd):
    n, d = h.shape
    tm = _row_tile(n)
    row = pl.BlockSpec((tm, d), lambda i: (i, 0))
    return pl.pallas_call(
        functools.partial(_ffn_kernel, final=False),
        grid=(n // tm,),
        in_specs=[row] + _ffn_specs(d, wg.shape[1]),
        out_specs=row,
        out_shape=jax.ShapeDtypeStruct((n, d), F32),
        input_output_aliases={0: 0},
        compiler_params=_params(("parallel",)),
        name="ffn",
    )(h, g, wg, wu, wd, g)


def _grid_rows_spec(t, seq, tm, d):
    assert seq % N_META == 0 and tm % N_META == 0
    return pl.BlockSpec((pl.Element(tm), pl.Element(d)),
                        lambda b, j: (pl.multiple_of(b * seq + N_META + j * tm, N_META), 0))


def _ffn_embed_kernel(x_ref, m_ref, g_ref, wg_ref, wu_ref, wd_ref, gf_ref, o_ref, xin_ref):
    tm = x_ref.shape[0]

    @pl.when(pl.program_id(1) == 0)
    def _():
        xin_ref[0:N_META, :] = m_ref[...]
        xin_ref[N_META:, :] = x_ref[0:tm - N_META, :]
        _ffn_kernel(xin_ref, g_ref, wg_ref, wu_ref, wd_ref, gf_ref, o_ref, final=False)

    @pl.when(pl.program_id(1) > 0)
    def _():
        _ffn_kernel(x_ref, g_ref, wg_ref, wu_ref, wd_ref, gf_ref, o_ref, final=False)


def _ffn_embed(x, meta_tokens, g, wg, wu, wd):
    batch, t, d = x.shape
    seq = N_META + t
    tm = _scan_time_block(seq)
    x_tile = pl.BlockSpec(
        (pl.Element(tm), pl.Element(d)),
        lambda b, j: (pl.multiple_of(b * t + jnp.maximum(j * tm - N_META, 0), N_META), 0))
    return pl.pallas_call(
        _ffn_embed_kernel,
        grid=(batch, seq // tm),
        in_specs=[x_tile, _const_spec((N_META, d))] + _ffn_specs(d, wg.shape[1]),
        out_specs=pl.BlockSpec((tm, d), lambda b, j: (b * (seq // tm) + j, 0)),
        out_shape=jax.ShapeDtypeStruct((batch * seq, d), F32),
        scratch_shapes=[pltpu.VMEM((tm, d), F32)],
        compiler_params=_params(("parallel", "parallel")),
        name="ffn_embed",
    )(x.reshape(batch * t, d), meta_tokens.astype(F32), g, wg, wu, wd, g)


def _ffn_final(h, g, wg, wu, wd, final_g, batch, t):
    n, d = h.shape
    seq = n // batch
    tm = _row_tile(t)
    out = pl.pallas_call(
        functools.partial(_ffn_kernel, final=True),
        grid=(batch, t // tm),
        in_specs=[_grid_rows_spec(t, seq, tm, d)] + _ffn_specs(d, wg.shape[1]),
        out_specs=pl.BlockSpec((tm, d), lambda b, j: (b * (t // tm) + j, 0)),
        out_shape=jax.ShapeDtypeStruct((batch * t, d), F32),
        compiler_params=_params(("parallel", "parallel")),
        name="ffn_final",
    )(h, g, wg, wu, wd, final_g)
    return out.reshape(batch, t, d)


def _seq_tile(seq):
    best = None
    for tq in range(BF16_ROWS, MAX_ROW_TILE // SCAN_GROUP + 1, BF16_ROWS):
        if seq % tq == 0:
            best = tq
    assert best is not None, seq
    return best


def _inproj_kernel(h_ref, g_ref, w_ref, q_ref, k_ref, v_ref, xr_ref, yr_ref, gna_ref, glru_ref):
    nseq, tq, d = h_ref.shape
    xn = _rms(h_ref[...].reshape(nseq * tq, d), g_ref[...]).astype(BF16)

    def proj(j):
        return jnp.dot(xn, w_ref[:, j * d:(j + 1) * d], preferred_element_type=F32)

    zq = (proj(0) * np.float32(NA_HEAD_DIM ** -0.5)).astype(BF16)
    zk = proj(1).astype(BF16)
    zv = proj(2).astype(BF16)
    for p in range(N_PAIRS):
        sl = slice(p * LANES, (p + 1) * LANES)
        for j in range(nseq):
            rows = slice(j * tq, (j + 1) * tq)
            q_ref[p, j] = zq[rows, sl]
            k_ref[p, j] = zk[rows, sl]
            v_ref[p, j] = zv[rows, sl]
    xr = proj(3)
    for s in range(d // LANES):
        for j in range(nseq):
            xr_ref[s, pl.ds(j, tq, stride=nseq), :] = xr[j * tq:(j + 1) * tq, s * LANES:(s + 1) * LANES]
    yr_ref[...] = proj(4).reshape(nseq, tq, d)
    gna_ref[...] = proj(5).reshape(nseq, tq, d)
    glru_ref[...] = proj(6).reshape(nseq, tq, d)


def _inproj(h, g, w_in, groups, seq):
    n, d = h.shape
    batch = groups * SCAN_GROUP
    tq = _seq_tile(seq)
    n_slabs = d // LANES
    tile = pl.BlockSpec((SCAN_GROUP, tq, d), lambda g_, i: (g_, i, 0))
    pair = pl.BlockSpec((N_PAIRS, SCAN_GROUP, tq, LANES), lambda g_, i: (0, g_, i, 0))
    slab = pl.BlockSpec((n_slabs, None, tq * SCAN_GROUP, LANES), lambda g_, i: (0, g_, i, 0))
    pair_shape = jax.ShapeDtypeStruct((N_PAIRS, batch, seq, LANES), BF16)
    tile_shape = jax.ShapeDtypeStruct((batch, seq, d), F32)
    slab_shape = jax.ShapeDtypeStruct((n_slabs, groups, seq * SCAN_GROUP, LANES), F32)
    return pl.pallas_call(
        _inproj_kernel,
        grid=(groups, seq // tq),
        in_specs=[tile, _const_spec((1, d)), _const_spec(w_in.shape)],
        out_specs=[pair, pair, pair, slab, tile, tile, tile],
        out_shape=[pair_shape, pair_shape, pair_shape, slab_shape, tile_shape, tile_shape, tile_shape],
        compiler_params=_params(("parallel", "parallel")),
        name="inproj",
    )(h.reshape(batch, seq, d), g, w_in)


def _attn_block_cases(rows):
    nb = rows // ATTN_QROWS
    assert rows % ATTN_QROWS == 0 and nb >= 3 and nb % 2 == 0, rows

    def case(m):
        ks = min(max(ATTN_QROWS * m - NA_KH // 2, 0), rows - ATTN_KROWS)
        out = []
        for q in range(ATTN_QROWS):
            r = ATTN_QROWS * m + q
            rs = min(max(r - NA_KH // 2, 0), rows - NA_KH)
            out.append((rs - ks, r - rs))
        return tuple(out)

    assert all(case(m) == case(1) for m in range(1, nb - 1))
    return case(0), case(1), case(nb - 1)


def _attn_bias_table(rel_bias, rows):
    c = np.arange(GRID_W)
    cs = np.clip(c - NA_KW // 2, 0, GRID_W - NA_KW)
    kc = np.arange(GRID_W)
    ok = (kc[None, :] >= cs[:, None]) & (kc[None, :] < cs[:, None] + NA_KW)
    dcol = kc[None, :] - c[:, None] + NA_KW - 1
    onehot = (dcol[:, :, None] == np.arange(2 * NA_KW - 1)).astype(np.float32)
    cols = jnp.einsum('hdj,ckj->hdck', rel_bias.astype(F32), onehot, precision=lax.Precision.HIGHEST)
    cols = jnp.where(ok[None, None], cols, np.float32(MASK_VALUE))
    mask = lambda n: jnp.full((NA_HEADS, n, GRID_W, GRID_W), np.float32(MASK_VALUE))
    types = []
    for cases in _attn_block_cases(rows):
        per_q = []
        for off, e in cases:
            band = cols[:, NA_KH - 1 - e:2 * NA_KH - 1 - e]
            per_q.append(jnp.concatenate([mask(off), band, mask(ATTN_KROWS - NA_KH - off)], axis=1))
        types.append(jnp.stack(per_q, axis=1))
    tbl = jnp.stack(types, axis=1)
    tbl = jnp.transpose(tbl, (0, 1, 2, 4, 3, 5))
    tbl = tbl.reshape(N_PAIRS, 2, 3, ATTN_QROWS * GRID_W, ATTN_KLOC)
    tbl = jnp.transpose(tbl, (0, 2, 1, 3, 4)).reshape(N_PAIRS, 3, 2 * ATTN_QROWS * GRID_W, ATTN_KLOC)
    lead = tbl.shape[:3]
    return jnp.concatenate([tbl, jnp.zeros(lead + (N_META,), F32),
                            jnp.full(lead + (ATTN_KEXT - ATTN_KLOC - N_META,), np.float32(MASK_VALUE))], axis=-1)


def _attn_kernel(q_ref, k_ref, v_ref, bias_ref, o_ref, kx_ref, vx_ref, s_ref, p_ref, den_ref):
    seq = q_ref.shape[0]
    rows = (seq - N_META) // GRID_W
    nb = rows // ATTN_QROWS
    nq = ATTN_QROWS * GRID_W
    nt = (((1,), (1,)), ((), ()))
    lo = lax.broadcasted_iota(jnp.int32, (1, LANES), 1) < NA_HEAD_DIM

    def stack_heads(x):
        zero = jnp.zeros_like(x)
        return jnp.concatenate([jnp.where(lo, x, zero), jnp.where(lo, zero, x)], axis=0)

    def pick_heads(o, n):
        return jnp.where(lo, o[:n], o[n:])

    km = k_ref[0:N_META, :]
    vm = v_ref[0:N_META, :]

    qm = stack_heads(q_ref[0:N_META, :])
    sm = lax.dot_general(qm, km, nt, preferred_element_type=F32)
    em = jnp.exp(sm - jnp.max(sm, axis=-1, keepdims=True))
    om = jnp.dot(em.astype(BF16), vm, preferred_element_type=F32)
    om = om / jnp.sum(em, axis=-1, keepdims=True)
    o_ref[0:N_META, :] = pick_heads(om, N_META).astype(o_ref.dtype)

    pad = jnp.zeros((ATTN_KEXT - ATTN_KLOC - N_META, LANES), BF16)
    for j in range(2):
        kx_ref[j, ATTN_KLOC:ATTN_KLOC + N_META, :] = km
        vx_ref[j, ATTN_KLOC:ATTN_KLOC + N_META, :] = vm
        kx_ref[j, ATTN_KLOC + N_META:, :] = pad
        vx_ref[j, ATTN_KLOC + N_META:, :] = pad

    def key_start(m):
        ks = jnp.clip(ATTN_QROWS * m - NA_KH // 2, 0, rows - ATTN_KROWS)
        return pl.multiple_of(N_META + ks * GRID_W, 16)

    def query_start(m):
        return pl.multiple_of(N_META + m * nq, 16)

    def scores(m, j):
        kx_ref[j, 0:ATTN_KLOC, :] = k_ref[pl.ds(key_start(m), ATTN_KLOC), :]
        qs = stack_heads(q_ref[pl.ds(query_start(m), nq), :])
        kind = jnp.where(m == 0, 0, jnp.where(m == nb - 1, 2, 1))
        s_ref[j] = lax.dot_general(qs, kx_ref[j], nt, preferred_element_type=F32) + bias_ref[kind]

    def softmax(j):
        s = s_ref[j]
        e = jnp.exp(s - jnp.max(s, axis=-1, keepdims=True))
        den_ref[j] = jnp.sum(e, axis=-1, keepdims=True)
        p_ref[j] = e.astype(BF16)

    def values(m, j):
        vx_ref[j, 0:ATTN_KLOC, :] = v_ref[pl.ds(key_start(m), ATTN_KLOC), :]
        o = jnp.dot(p_ref[j], vx_ref[j], preferred_element_type=F32) / den_ref[j]
        o_ref[pl.ds(query_start(m), nq), :] = pick_heads(o, nq).astype(o_ref.dtype)

    scores(0, 0)
    scores(1, 1)
    softmax(0)

    def two_blocks(i, carry):
        m = 2 * i + 2
        scores(m, 0)
        softmax(1)
        values(m - 2, 0)
        scores(m + 1, 1)
        softmax(0)
        values(m - 1, 1)
        return carry

    lax.fori_loop(0, nb // 2 - 1, two_blocks, 0)
    softmax(1)
    values(nb - 2, 0)
    values(nb - 1, 1)


def _attention(q, k, v, bias_tbl):
    _, batch, seq, _ = q.shape
    blk = pl.BlockSpec((None, None, seq, LANES), lambda p, b: (p, b, 0, 0))
    bias_spec = pl.BlockSpec((None,) + bias_tbl.shape[1:], lambda p, b: (p, 0, 0, 0))
    out = pl.pallas_call(
        _attn_kernel,
        grid=(N_PAIRS, batch),
        in_specs=[blk, blk, blk, bias_spec],
        out_specs=blk,
        out_shape=jax.ShapeDtypeStruct(q.shape, BF16),
        scratch_shapes=[pltpu.VMEM((2, ATTN_KEXT, LANES), BF16), pltpu.VMEM((2, ATTN_KEXT, LANES), BF16),
                        pltpu.VMEM((2, 2 * ATTN_QROWS * GRID_W, ATTN_KEXT), F32),
                        pltpu.VMEM((2, 2 * ATTN_QROWS * GRID_W, ATTN_KEXT), BF16),
                        pltpu.VMEM((2, 2 * ATTN_QROWS * GRID_W, 1), F32)],
        compiler_params=_params(("parallel", "parallel")),
        name="attention",
    )(q, k, v, bias_tbl)
    return out


def _scan_time_block(seq):
    best = None
    for tt in range(LRU_TC, LRU_MAX_TT + 1, LRU_TC):
        if seq % tt == 0:
            best = tt
    assert best is not None, seq
    return best


def _lru_kernel(xl_ref, x_ref, xn_ref, cw_ref, cb_ref, w_ref, ba_ref, bx_ref, lam_ref, *rest, nt, reverse):
    acc_ref, o_ref, x_s, carry = rest if reverse else (None,) + rest
    step = pl.program_id(2)
    n_slabs, n_rows, _ = x_ref.shape
    blk = (nt - 1 - step) if reverse else step
    has_left = blk > 0
    has_right = blk < nt - 1
    left_rows = xl_ref.shape[1]
    chunk_rows = LRU_TC * SCAN_GROUP
    n_chunks = n_rows // chunk_rows

    @pl.when(step == 0)
    def _():
        carry[...] = jnp.zeros_like(carry)

    for s in range(n_slabs):
        sl = slice(s * LANES, (s + 1) * LANES)
        x_s[0:left_rows, sl] = jnp.where(has_left, xl_ref[s], jnp.zeros_like(xl_ref[s]))
        x_s[left_rows:left_rows + n_rows, sl] = x_ref[s]
        x_s[left_rows + n_rows:, sl] = jnp.where(has_right, xn_ref[s], jnp.zeros_like(xn_ref[s]))

    lam = lam_ref[...]
    softplus = jnp.maximum(-lam, 0.0) + jnp.log1p(jnp.exp(-jnp.abs(lam)))
    neg_c_sp = np.float32(-LRU_C) * softplus
    cw = cw_ref[...]
    cb = cb_ref[...]
    ba = ba_ref[...]
    bx = bx_ref[...]

    def chunk(i, h):
        c = (n_chunks - 1 - i) if reverse else i
        row0 = pl.multiple_of(c * chunk_rows, chunk_rows)
        xc = cb
        for j in range(CONV_W):
            xc = xc + x_s[pl.ds(row0 + j * SCAN_GROUP, chunk_rows), :] * cw[j:j + 1, :]
        pre = jnp.dot(xc.astype(BF16), w_ref[...], preferred_element_type=F32)
        r = jax.nn.sigmoid(pre[:, :LRU_CHUNK] + ba)
        gi = jax.nn.sigmoid(pre[:, LRU_CHUNK:] + bx)
        log_a = neg_c_sp * r
        a = jnp.exp(log_a)
        y = -jnp.tanh(log_a) * (a * a + 1.0)
        root = jnp.where(y > 0.0, y * lax.rsqrt(y), 0.0)
        u = root * (gi * xc)
        hs = [None] * LRU_TC
        order = range(LRU_TC - 1, -1, -1) if reverse else range(LRU_TC)
        for k in order:
            rows = slice(k * SCAN_GROUP, (k + 1) * SCAN_GROUP)
            h = a[rows] * h + u[rows]
            hs[k] = h
        hc = jnp.concatenate(hs, axis=0)
        for s in range(n_slabs):
            part = hc[:, s * LANES:(s + 1) * LANES]
            if reverse:
                part = part + acc_ref[s, pl.ds(row0, chunk_rows), :]
            o_ref[s, pl.ds(row0, chunk_rows), :] = part
        return h

    carry[...] = lax.fori_loop(0, n_chunks, chunk, carry[...], unroll=4)


def _lru(xr, conv_w, conv_b, w_gate, ba, bx, lam, seq, forward_out=None):
    reverse = forward_out is not None
    n_slabs, groups, _, _ = xr.shape
    c = n_slabs * LANES
    tt = _scan_time_block(seq)
    nt = seq // tt
    n_chunks = c // LRU_CHUNK
    slabs = LRU_CHUNK // LANES
    d = 1 if reverse else 0
    left_steps = CONV_LEFT
    right_steps = CONV_W - 1 - CONV_LEFT
    assert tt % left_steps == 0 and right_steps == 1

    def tblk(i):
        return (nt - 1 - i) if reverse else i

    cur = pl.BlockSpec((slabs, None, tt * SCAN_GROUP, LANES), lambda g, ch, i: (ch, g, tblk(i), 0))
    left = pl.BlockSpec((slabs, None, left_steps * SCAN_GROUP, LANES),
                        lambda g, ch, i: (ch, g, jnp.maximum(tblk(i) * (tt // left_steps) - 1, 0), 0))
    right = pl.BlockSpec((slabs, None, right_steps * SCAN_GROUP, LANES),
                         lambda g, ch, i: (ch, g, jnp.minimum((tblk(i) + 1) * tt, seq - 1), 0))
    per_ch = lambda rows: pl.BlockSpec((rows, LRU_CHUNK), lambda g, ch, i: (0, ch))
    per_dir = pl.BlockSpec((None, 1, LRU_CHUNK), lambda g, ch, i: (d, 0, ch))
    return pl.pallas_call(
        functools.partial(_lru_kernel, nt=nt, reverse=reverse),
        grid=(groups, n_chunks, nt),
        in_specs=[left, cur, right, per_ch(CONV_W), per_ch(1),
                  pl.BlockSpec((None, None, LRU_CHUNK, 2 * LRU_CHUNK), lambda g, ch, i: (d, ch, 0, 0)),
                  per_dir, per_dir, per_dir] + ([cur] if reverse else []),
        out_specs=cur,
        out_shape=jax.ShapeDtypeStruct(xr.shape, F32),
        input_output_aliases={9: 0} if reverse else {},
        scratch_shapes=[pltpu.VMEM(((tt + CONV_W - 1) * SCAN_GROUP, LRU_CHUNK), F32),
                        pltpu.VMEM((SCAN_GROUP, LRU_CHUNK), F32)],
        compiler_params=_params(("parallel", "parallel", "arbitrary")),
        name="lru_bwd" if reverse else "lru_fwd",
    )(xr, xr, xr, conv_w, conv_b.reshape(1, c), w_gate,
      ba.reshape(2, 1, c), bx.reshape(2, 1, c), lam.reshape(2, 1, c), *([forward_out] if reverse else []))


def _lru_gate_weights(wa, wx):
    per = LRU_CHUNK // LRU_BLOCK_DIM

    def dense(w):
        n_dir, n_blk, bd, _ = w.shape
        w = w.reshape(n_dir, n_blk // per, per, bd, bd)
        eye = jnp.eye(per, dtype=w.dtype)
        full = jnp.einsum('dcpij,pq->dcpiqj', w, eye)
        return full.reshape(n_dir, n_blk // per, LRU_CHUNK, LRU_CHUNK)

    return jnp.concatenate([dense(wa), dense(wx)], axis=-1).astype(BF16)


def _mixout_kernel(h_ref, na_ref, yr_ref, gna_ref, glru_ref, hr_ref, wn_ref, wl_ref, wo_ref, o_ref):
    nseq, tq, d = h_ref.shape
    rows = lambda ref: ref[...].reshape(nseq * tq, d)

    na = jnp.concatenate(
        [jnp.concatenate([na_ref[p, j] for j in range(nseq)], axis=0) for p in range(N_PAIRS)], axis=-1)
    na_p = jnp.dot(na, wn_ref[...], preferred_element_type=F32)
    hr = jnp.concatenate(
        [jnp.concatenate([hr_ref[s, pl.ds(j, tq, stride=nseq), :] for j in range(nseq)], axis=0)
         for s in range(d // LANES)], axis=-1)
    lru_in = (_gelu_tanh(rows(yr_ref)) * hr).astype(BF16)
    lru_p = jnp.dot(lru_in, wl_ref[...], preferred_element_type=F32)
    merged = jax.nn.sigmoid(rows(gna_ref)) * na_p + jax.nn.sigmoid(rows(glru_ref)) * lru_p
    out = rows(h_ref) + jnp.dot(merged.astype(BF16), wo_ref[...], preferred_element_type=F32)
    o_ref[...] = out.reshape(nseq, tq, d)


def _mixout(h, na, yr, gna, glru, hr, wn, wl, wo):
    n, d = h.shape
    _, batch, seq, _ = na.shape
    groups = batch // SCAN_GROUP
    tq = _seq_tile(seq)
    n_slabs = d // LANES
    tile = pl.BlockSpec((SCAN_GROUP, tq, d), lambda g_, i: (g_, i, 0))
    pair = pl.BlockSpec((N_PAIRS, SCAN_GROUP, tq, LANES), lambda g_, i: (0, g_, i, 0))
    slab = pl.BlockSpec((n_slabs, None, tq * SCAN_GROUP, LANES), lambda g_, i: (0, g_, i, 0))
    out = pl.pallas_call(
        _mixout_kernel,
        grid=(groups, seq // tq),
        in_specs=[tile, pair, tile, tile, tile, slab,
                  _const_spec((d, d)), _const_spec((d, d)), _const_spec((d, d))],
        out_specs=tile,
        out_shape=jax.ShapeDtypeStruct((batch, seq, d), F32),
        input_output_aliases={0: 0},
        compiler_params=_params(("parallel", "parallel")),
        name="mixout",
    )(h.reshape(batch, seq, d), na, yr, gna, glru, hr, wn, wl, wo)
    return out.reshape(n, d)


def _prepare_layer(l, p, rows):
    d = D_MODEL
    row = lambda a: a[l].reshape(1, d).astype(F32)
    return dict(
        norm_ffn1=row(p["norm_ffn1"]), norm_mix=row(p["norm_mix"]), norm_ffn2=row(p["norm_ffn2"]),
        ffn1=(p["ffn1_w_gate"][l].astype(BF16), p["ffn1_w_up"][l].astype(BF16), p["ffn1_w_down"][l].astype(BF16)),
        ffn2=(p["ffn2_w_gate"][l].astype(BF16), p["ffn2_w_up"][l].astype(BF16), p["ffn2_w_down"][l].astype(BF16)),
        w_in=p["w_in"][l].astype(BF16),
        bias_tbl=_attn_bias_table(p["na_rel_bias"][l], rows),
        conv_w=p["conv_w"][l].astype(F32), conv_b=p["conv_b"][l].astype(F32),
        w_gate=_lru_gate_weights(p["lru_wa"][l], p["lru_wx"][l]),
        ba=p["lru_ba"][l].astype(F32), bx=p["lru_bx"][l].astype(F32), lam=p["lru_lambda"][l].astype(F32),
        w_na_proj=p["w_na_proj"][l].astype(BF16), w_lru_proj=p["w_lru_proj"][l].astype(BF16),
        w_out=p["w_out"][l].astype(BF16),
    )


def _encode(x, meta_tokens, layers, final_norm):
    batch, t, d = x.shape
    seq = N_META + t
    assert batch % SCAN_GROUP == 0 and seq % N_META == 0
    groups = batch // SCAN_GROUP
    gf = final_norm.reshape(1, d).astype(F32)
    h = None
    for li, lp in enumerate(layers):
        if li == 0:
            h = _ffn_embed(x, meta_tokens, lp["norm_ffn1"], *lp["ffn1"])
        else:
            h = _ffn(h, lp["norm_ffn1"], *lp["ffn1"])
        q, k, v, xr, yr, gna, glru = _inproj(h, lp["norm_mix"], lp["w_in"], groups, seq)
        na = _attention(q, k, v, lp["bias_tbl"])
        lru_args = (xr, lp["conv_w"], lp["conv_b"], lp["w_gate"], lp["ba"], lp["bx"], lp["lam"], seq)
        hr = _lru(*lru_args, forward_out=_lru(*lru_args))
        h = _mixout(h, na, yr, gna, glru, hr, lp["w_na_proj"], lp["w_lru_proj"], lp["w_out"])
        if li < len(layers) - 1:
            h = _ffn(h, lp["norm_ffn2"], *lp["ffn2"])
    return _ffn_final(h, layers[-1]["norm_ffn2"], *layers[-1]["ffn2"], gf, batch, t)


def kernel(x_prompt, x_sample, meta_tokens, norm_ffn1, ffn1_w_gate, ffn1_w_up, ffn1_w_down, norm_mix, w_in, na_rel_bias, conv_w, conv_b, lru_wa, lru_ba, lru_wx, lru_bx, lru_lambda, w_na_proj, w_lru_proj, w_out, norm_ffn2, ffn2_w_gate, ffn2_w_up, ffn2_w_down, final_norm):
    p = dict(norm_ffn1=norm_ffn1, ffn1_w_gate=ffn1_w_gate, ffn1_w_up=ffn1_w_up, ffn1_w_down=ffn1_w_down,
             norm_mix=norm_mix, w_in=w_in, na_rel_bias=na_rel_bias, conv_w=conv_w, conv_b=conv_b,
             lru_wa=lru_wa, lru_ba=lru_ba, lru_wx=lru_wx, lru_bx=lru_bx, lru_lambda=lru_lambda,
             w_na_proj=w_na_proj, w_lru_proj=w_lru_proj, w_out=w_out, norm_ffn2=norm_ffn2,
             ffn2_w_gate=ffn2_w_gate, ffn2_w_up=ffn2_w_up, ffn2_w_down=ffn2_w_down)
    rows = x_prompt.shape[1] // GRID_W
    assert _attn_block_cases(rows) == _attn_block_cases(x_sample.shape[1] // GRID_W)
    layers = [_prepare_layer(l, p, rows) for l in range(norm_ffn1.shape[0])]
    y_prompt = _encode(x_prompt, meta_tokens, layers, final_norm)
    y_sample = _encode(x_sample, meta_tokens, layers, final_norm)
    return (y_prompt, y_sample)
```

```python
import functools

import numpy as np
import jax
import jax.numpy as jnp
from jax import lax
from jax.experimental import pallas as pl
from jax.experimental.pallas import tpu as pltpu

F32 = jnp.float32
BF16 = jnp.bfloat16

D_MODEL = 1024
N_META = 16
GRID_W = 64
NA_HEADS = 16
NA_HEAD_DIM = 64
NA_KH = 8
NA_KW = 16
LRU_BLOCK_DIM = 64
CONV_W = 4
CONV_LEFT = 2
LRU_C = 8.0
RMS_EPS = 1e-6
MASK_VALUE = -1e30

LANES = 128
SUBLANES = 8
N_PAIRS = NA_HEADS // 2
ATTN_QROWS = 4
ATTN_KROWS = ATTN_QROWS + NA_KH - 1
ATTN_KLOC = ATTN_KROWS * GRID_W
ATTN_KEXT = 768
ATTN_BLOCKS_PER_STEP = 32
LRU_CHUNK = 256
SCAN_GROUP = SUBLANES
LRU_TC = 16
LRU_MAX_TT = 704
VMEM_LIMIT = 56 * 1024 * 1024
MAX_ROW_TILE = 704
BF16_ROWS = 16


def _row_tile(n):
    for t in range(MAX_ROW_TILE, 0, -BF16_ROWS):
        if n % t == 0:
            return t
    raise ValueError(f"no row tile for {n} rows")


def _const_spec(shape):
    zeros = (0,) * len(shape)
    return pl.BlockSpec(shape, lambda *_: zeros, pipeline_mode=pl.Buffered(1))


def _params(sem):
    return pltpu.CompilerParams(dimension_semantics=sem, vmem_limit_bytes=VMEM_LIMIT)


def _rms(x, g):
    inv = lax.rsqrt(jnp.mean(x * x, axis=-1, keepdims=True) + RMS_EPS)
    return (x * inv) * g


def _gelu_tanh(x):
    c = np.float32(np.sqrt(2.0 / np.pi))
    return x * (0.5 * (1.0 + jnp.tanh(c * (x + np.float32(0.044715) * (x * x * x)))))


def _ffn_kernel(h_ref, g_ref, wg_ref, wu_ref, wd_ref, gf_ref, o_ref, *, final):
    x = h_ref[...]
    xn = _rms(x, g_ref[...]).astype(BF16)
    gate = jnp.dot(xn, wg_ref[...], preferred_element_type=F32)
    up = jnp.dot(xn, wu_ref[...], preferred_element_type=F32)
    act = (gate * jax.nn.sigmoid(gate) * up).astype(BF16)
    down = jnp.dot(act, wd_ref[...], preferred_element_type=F32)
    y = x + 0.5 * down
    o_ref[...] = _rms(y, gf_ref[...]) if final else y


def _ffn_specs(d, dff):
    return [_const_spec((1, d)), _const_spec((d, dff)), _const_spec((d, dff)), _const_spec((dff, d)),
            _const_spec((1, d))]


def _ffn(h, g, wg, wu, wd):
    n, d = h.shape
    tm = _row_tile(n)
    row = pl.BlockSpec((tm, d), lambda i: (i, 0))
    return pl.pallas_call(
        functools.partial(_ffn_kernel, final=False),
        grid=(n // tm,),
        in_specs=[row] + _ffn_specs(d, wg.shape[1]),
        out_specs=row,
        out_shape=jax.ShapeDtypeStruct((n, d), F32),
        input_output_aliases={0: 0},
        compiler_params=_params(("parallel",)),
        name="ffn",
    )(h, g, wg, wu, wd, g)


def _grid_rows_spec(t, seq, tm, d):
    assert seq % N_META == 0 and tm % N_META == 0
    return pl.BlockSpec((pl.Element(tm), pl.Element(d)),
                        lambda b, j: (pl.multiple_of(b * seq + N_META + j * tm, N_META), 0))


def _ffn_embed_kernel(x_ref, m_ref, g_ref, wg_ref, wu_ref, wd_ref, gf_ref, o_ref, xin_ref):
    tm = x_ref.shape[0]

    @pl.when(pl.program_id(1) == 0)
    def _():
        xin_ref[0:N_META, :] = m_ref[...]
        xin_ref[N_META:, :] = x_ref[0:tm - N_META, :]
        _ffn_kernel(xin_ref, g_ref, wg_ref, wu_ref, wd_ref, gf_ref, o_ref, final=False)

    @pl.when(pl.program_id(1) > 0)
    def _():
        _ffn_kernel(x_ref, g_ref, wg_ref, wu_ref, wd_ref, gf_ref, o_ref, final=False)


def _ffn_embed(x, meta_tokens, g, wg, wu, wd):
    batch, t, d = x.shape
    seq = N_META + t
    tm = _scan_time_block(seq)
    x_tile = pl.BlockSpec(
        (pl.Element(tm), pl.Element(d)),
        lambda b, j: (pl.multiple_of(b * t + jnp.maximum(j * tm - N_META, 0), N_META), 0))
    return pl.pallas_call(
        _ffn_embed_kernel,
        grid=(batch, seq // tm),
        in_specs=[x_tile, _const_spec((N_META, d))] + _ffn_specs(d, wg.shape[1]),
        out_specs=pl.BlockSpec((tm, d), lambda b, j: (b * (seq // tm) + j, 0)),
        out_shape=jax.ShapeDtypeStruct((batch * seq, d), F32),
        scratch_shapes=[pltpu.VMEM((tm, d), F32)],
        compiler_params=_params(("parallel", "parallel")),
        name="ffn_embed",
    )(x.reshape(batch * t, d), meta_tokens.astype(F32), g, wg, wu, wd, g)


def _ffn_final(h, g, wg, wu, wd, final_g, batch, t):
    n, d = h.shape
    seq = n // batch
    tm = _row_tile(t)
    out = pl.pallas_call(
        functools.partial(_ffn_kernel, final=True),
        grid=(batch, t // tm),
        in_specs=[_grid_rows_spec(t, seq, tm, d)] + _ffn_specs(d, wg.shape[1]),
        out_specs=pl.BlockSpec((tm, d), lambda b, j: (b * (t // tm) + j, 0)),
        out_shape=jax.ShapeDtypeStruct((batch * t, d), F32),
        compiler_params=_params(("parallel", "parallel")),
        name="ffn_final",
    )(h, g, wg, wu, wd, final_g)
    return out.reshape(batch, t, d)


def _seq_tile(seq):
    best = None
    for tq in range(BF16_ROWS, MAX_ROW_TILE // SCAN_GROUP + 1, BF16_ROWS):
        if seq % tq == 0:
            best = tq
    assert best is not None, seq
    return best


def _inproj_kernel(h_ref, g_ref, w_ref, q_ref, k_ref, v_ref, xr_ref, yr_ref, gna_ref, glru_ref):
    nseq, tq, d = h_ref.shape
    xn = _rms(h_ref[...].reshape(nseq * tq, d), g_ref[...]).astype(BF16)

    def proj(j):
        return jnp.dot(xn, w_ref[:, j * d:(j + 1) * d], preferred_element_type=F32)

    zq = (proj(0) * np.float32(NA_HEAD_DIM ** -0.5)).astype(BF16)
    zk = proj(1).astype(BF16)
    zv = proj(2).astype(BF16)
    for p in range(N_PAIRS):
        sl = slice(p * LANES, (p + 1) * LANES)
        for j in range(nseq):
            rows = slice(j * tq, (j + 1) * tq)
            q_ref[p, j] = zq[rows, sl]
            k_ref[p, j] = zk[rows, sl]
            v_ref[p, j] = zv[rows, sl]
    xr = proj(3)
    for s in range(d // LANES):
        for j in range(nseq):
            xr_ref[s, pl.ds(j, tq, stride=nseq), :] = xr[j * tq:(j + 1) * tq, s * LANES:(s + 1) * LANES]
    yr_ref[...] = proj(4).reshape(nseq, tq, d).astype(BF16)
    gna_ref[...] = proj(5).reshape(nseq, tq, d).astype(BF16)
    glru_ref[...] = proj(6).reshape(nseq, tq, d).astype(BF16)


def _inproj(h, g, w_in, groups, seq):
    n, d = h.shape
    batch = groups * SCAN_GROUP
    tq = _seq_tile(seq)
    n_slabs = d // LANES
    tile = pl.BlockSpec((SCAN_GROUP, tq, d), lambda g_, i: (g_, i, 0))
    pair = pl.BlockSpec((N_PAIRS, SCAN_GROUP, tq, LANES), lambda g_, i: (0, g_, i, 0))
    slab = pl.BlockSpec((n_slabs, None, tq * SCAN_GROUP, LANES), lambda g_, i: (0, g_, i, 0))
    pair_shape = jax.ShapeDtypeStruct((N_PAIRS, batch, seq, LANES), BF16)
    tile_shape = jax.ShapeDtypeStruct((batch, seq, d), BF16)
    slab_shape = jax.ShapeDtypeStruct((n_slabs, groups, seq * SCAN_GROUP, LANES), F32)
    return pl.pallas_call(
        _inproj_kernel,
        grid=(groups, seq // tq),
        in_specs=[tile, _const_spec((1, d)), _const_spec(w_in.shape)],
        out_specs=[pair, pair, pair, slab, tile, tile, tile],
        out_shape=[pair_shape, pair_shape, pair_shape, slab_shape, tile_shape, tile_shape, tile_shape],
        compiler_params=_params(("parallel", "parallel")),
        name="inproj",
    )(h.reshape(batch, seq, d), g, w_in)


def _attn_block_cases(rows):
    nb = rows // ATTN_QROWS
    assert rows % ATTN_QROWS == 0 and nb >= 3 and nb % 2 == 0, rows

    def case(m):
        ks = min(max(ATTN_QROWS * m - NA_KH // 2, 0), rows - ATTN_KROWS)
        out = []
        for q in range(ATTN_QROWS):
            r = ATTN_QROWS * m + q
            rs = min(max(r - NA_KH // 2, 0), rows - NA_KH)
            out.append((rs - ks, r - rs))
        return tuple(out)

    assert all(case(m) == case(1) for m in range(1, nb - 1))
    return case(0), case(1), case(nb - 1)


def _attn_bias_table(rel_bias, rows):
    c = np.arange(GRID_W)
    cs = np.clip(c - NA_KW // 2, 0, GRID_W - NA_KW)
    kc = np.arange(GRID_W)
    ok = (kc[None, :] >= cs[:, None]) & (kc[None, :] < cs[:, None] + NA_KW)
    dcol = kc[None, :] - c[:, None] + NA_KW - 1
    onehot = (dcol[:, :, None] == np.arange(2 * NA_KW - 1)).astype(np.float32)
    cols = jnp.einsum('hdj,ckj->hdck', rel_bias.astype(F32), onehot, precision=lax.Precision.HIGHEST)
    cols = jnp.where(ok[None, None], cols, np.float32(MASK_VALUE))
    mask = lambda n: jnp.full((NA_HEADS, n, GRID_W, GRID_W), np.float32(MASK_VALUE))
    types = []
    for cases in _attn_block_cases(rows):
        per_q = []
        for off, e in cases:
            band = cols[:, NA_KH - 1 - e:2 * NA_KH - 1 - e]
            per_q.append(jnp.concatenate([mask(off), band, mask(ATTN_KROWS - NA_KH - off)], axis=1))
        types.append(jnp.stack(per_q, axis=1))
    tbl = jnp.stack(types, axis=1)
    tbl = jnp.transpose(tbl, (0, 1, 2, 4, 3, 5))
    tbl = tbl.reshape(N_PAIRS, 2, 3, ATTN_QROWS * GRID_W, ATTN_KLOC)
    tbl = jnp.transpose(tbl, (0, 2, 1, 3, 4)).reshape(N_PAIRS, 3, 2 * ATTN_QROWS * GRID_W, ATTN_KLOC)
    lead = tbl.shape[:3]
    return jnp.concatenate([tbl, jnp.zeros(lead + (N_META,), F32),
                            jnp.full(lead + (ATTN_KEXT - ATTN_KLOC - N_META,), np.float32(MASK_VALUE))], axis=-1)


def _attn_kernel(q_ref, k_ref, v_ref, bias_ref, o_ref, kx_ref, vx_ref, s_ref, p_ref, den_ref):
    n_seq, seq, _ = q_ref.shape
    rows = (seq - N_META) // GRID_W
    nb = rows // ATTN_QROWS
    n_blocks = n_seq * nb
    nq = ATTN_QROWS * GRID_W
    nt = (((1,), (1,)), ((), ()))
    lo = lax.broadcasted_iota(jnp.int32, (1, LANES), 1) < NA_HEAD_DIM

    def stack_heads(x):
        zero = jnp.zeros_like(x)
        return jnp.concatenate([jnp.where(lo, x, zero), jnp.where(lo, zero, x)], axis=0)

    def pick_heads(o, n):
        return jnp.where(lo, o[:n], o[n:])

    def meta_queries(sq, carry):
        km = k_ref[sq, 0:N_META, :]
        vm = v_ref[sq, 0:N_META, :]
        qm = stack_heads(q_ref[sq, 0:N_META, :])
        sm = lax.dot_general(qm, km, nt, preferred_element_type=F32)
        em = jnp.exp(sm - jnp.max(sm, axis=-1, keepdims=True))
        om = jnp.dot(em.astype(BF16), vm, preferred_element_type=F32)
        om = om / jnp.sum(em, axis=-1, keepdims=True)
        o_ref[sq, 0:N_META, :] = pick_heads(om, N_META).astype(o_ref.dtype)
        return carry

    lax.fori_loop(0, n_seq, meta_queries, 0)

    pad = jnp.zeros((ATTN_KEXT - ATTN_KLOC - N_META, LANES), BF16)
    for j in range(2):
        kx_ref[j, ATTN_KLOC + N_META:, :] = pad
        vx_ref[j, ATTN_KLOC + N_META:, :] = pad

    def locate(m):
        sq = m // nb
        local = m - sq * nb
        ks = jnp.clip(ATTN_QROWS * local - NA_KH // 2, 0, rows - ATTN_KROWS)
        q0 = pl.multiple_of(N_META + local * nq, 16)
        k0 = pl.multiple_of(N_META + ks * GRID_W, 16)
        return sq, local, q0, k0

    def scores(m, j):
        sq, local, q0, k0 = locate(m)
        kx_ref[j, 0:ATTN_KLOC, :] = k_ref[sq, pl.ds(k0, ATTN_KLOC), :]
        kx_ref[j, ATTN_KLOC:ATTN_KLOC + N_META, :] = k_ref[sq, 0:N_META, :]
        qs = stack_heads(q_ref[sq, pl.ds(q0, nq), :])
        kind = jnp.where(local == 0, 0, jnp.where(local == nb - 1, 2, 1))
        s_ref[j] = lax.dot_general(qs, kx_ref[j], nt, preferred_element_type=F32) + bias_ref[kind]

    def softmax(j):
        s = s_ref[j]
        e = jnp.exp(s - jnp.max(s, axis=-1, keepdims=True))
        den_ref[j] = jnp.sum(e, axis=-1, keepdims=True)
        p_ref[j] = e.astype(BF16)

    def values(m, j):
        sq, _, q0, k0 = locate(m)
        vx_ref[j, 0:ATTN_KLOC, :] = v_ref[sq, pl.ds(k0, ATTN_KLOC), :]
        vx_ref[j, ATTN_KLOC:ATTN_KLOC + N_META, :] = v_ref[sq, 0:N_META, :]
        o = jnp.dot(p_ref[j], vx_ref[j], preferred_element_type=F32) / den_ref[j]
        o_ref[sq, pl.ds(q0, nq), :] = pick_heads(o, nq).astype(o_ref.dtype)

    scores(0, 0)
    scores(1, 1)
    softmax(0)

    def two_blocks(i, carry):
        m = 2 * i + 2
        scores(m, 0)
        softmax(1)
        values(m - 2, 0)
        scores(m + 1, 1)
        softmax(0)
        values(m - 1, 1)
        return carry

    lax.fori_loop(0, n_blocks // 2 - 1, two_blocks, 0)
    softmax(1)
    values(n_blocks - 2, 0)
    values(n_blocks - 1, 1)


def _attention(q, k, v, bias_tbl):
    _, batch, seq, _ = q.shape
    blocks_per_seq = (seq - N_META) // (GRID_W * ATTN_QROWS)
    n_seq = max(1, min(batch, ATTN_BLOCKS_PER_STEP // blocks_per_seq))
    assert batch % n_seq == 0
    blk = pl.BlockSpec((None, n_seq, seq, LANES), lambda p, b: (p, b, 0, 0))
    bias_spec = pl.BlockSpec((None,) + bias_tbl.shape[1:], lambda p, b: (p, 0, 0, 0))
    out = pl.pallas_call(
        _attn_kernel,
        grid=(N_PAIRS, batch // n_seq),
        in_specs=[blk, blk, blk, bias_spec],
        out_specs=blk,
        out_shape=jax.ShapeDtypeStruct(q.shape, BF16),
        scratch_shapes=[pltpu.VMEM((2, ATTN_KEXT, LANES), BF16), pltpu.VMEM((2, ATTN_KEXT, LANES), BF16),
                        pltpu.VMEM((2, 2 * ATTN_QROWS * GRID_W, ATTN_KEXT), F32),
                        pltpu.VMEM((2, 2 * ATTN_QROWS * GRID_W, ATTN_KEXT), BF16),
                        pltpu.VMEM((2, 2 * ATTN_QROWS * GRID_W, 1), F32)],
        compiler_params=_params(("parallel", "parallel")),
        name="attention",
    )(q, k, v, bias_tbl)
    return out


def _scan_time_block(seq):
    best = None
    for tt in range(LRU_TC, LRU_MAX_TT + 1, LRU_TC):
        if seq % tt == 0:
            best = tt
    assert best is not None, seq
    return best


def _lru_kernel(xl_ref, x_ref, xn_ref, cw_ref, cb_ref, w_ref, ba_ref, bx_ref, lam_ref, *rest, nt, reverse):
    acc_ref, o_ref, x_s, carry = rest if reverse else (None,) + rest
    step = pl.program_id(2)
    n_slabs, n_rows, _ = x_ref.shape
    blk = (nt - 1 - step) if reverse else step
    has_left = blk > 0
    has_right = blk < nt - 1
    left_rows = xl_ref.shape[1]
    chunk_rows = LRU_TC * SCAN_GROUP
    n_chunks = n_rows // chunk_rows

    @pl.when(step == 0)
    def _():
        carry[...] = jnp.zeros_like(carry)

    for s in range(n_slabs):
        sl = slice(s * LANES, (s + 1) * LANES)
        x_s[0:left_rows, sl] = jnp.where(has_left, xl_ref[s], jnp.zeros_like(xl_ref[s]))
        x_s[left_rows:left_rows + n_rows, sl] = x_ref[s]
        x_s[left_rows + n_rows:, sl] = jnp.where(has_right, xn_ref[s], jnp.zeros_like(xn_ref[s]))

    lam = lam_ref[...]
    softplus = jnp.maximum(-lam, 0.0) + jnp.log1p(jnp.exp(-jnp.abs(lam)))
    neg_c_sp = np.float32(-LRU_C) * softplus
    cw = cw_ref[...]
    cb = cb_ref[...]
    ba = ba_ref[...]
    bx = bx_ref[...]

    def chunk(i, h):
        c = (n_chunks - 1 - i) if reverse else i
        row0 = pl.multiple_of(c * chunk_rows, chunk_rows)
        xc = cb
        for j in range(CONV_W):
            xc = xc + x_s[pl.ds(row0 + j * SCAN_GROUP, chunk_rows), :] * cw[j:j + 1, :]
        pre = jnp.dot(xc.astype(BF16), w_ref[...], preferred_element_type=F32)
        r = jax.nn.sigmoid(pre[:, :LRU_CHUNK] + ba)
        gi = jax.nn.sigmoid(pre[:, LRU_CHUNK:] + bx)
        log_a = neg_c_sp * r
        a = jnp.exp(log_a)
        y = -jnp.tanh(log_a) * (a * a + 1.0)
        root = jnp.where(y > 0.0, y * lax.rsqrt(y), 0.0)
        u = root * (gi * xc)
        hs = [None] * LRU_TC
        order = range(LRU_TC - 1, -1, -1) if reverse else range(LRU_TC)
        for k in order:
            rows = slice(k * SCAN_GROUP, (k + 1) * SCAN_GROUP)
            h = a[rows] * h + u[rows]
            hs[k] = h
        hc = jnp.concatenate(hs, axis=0)
        for s in range(n_slabs):
            part = hc[:, s * LANES:(s + 1) * LANES]
            if reverse:
                part = part + acc_ref[s, pl.ds(row0, chunk_rows), :]
            o_ref[s, pl.ds(row0, chunk_rows), :] = part
        return h

    carry[...] = lax.fori_loop(0, n_chunks, chunk, carry[...], unroll=4)


def _lru(xr, conv_w, conv_b, w_gate, ba, bx, lam, seq, forward_out=None):
    reverse = forward_out is not None
    n_slabs, groups, _, _ = xr.shape
    c = n_slabs * LANES
    tt = _scan_time_block(seq)
    nt = seq // tt
    n_chunks = c // LRU_CHUNK
    slabs = LRU_CHUNK // LANES
    d = 1 if reverse else 0
    left_steps = CONV_LEFT
    right_steps = CONV_W - 1 - CONV_LEFT
    assert tt % left_steps == 0 and right_steps == 1

    def tblk(i):
        return (nt - 1 - i) if reverse else i

    cur = pl.BlockSpec((slabs, None, tt * SCAN_GROUP, LANES), lambda g, ch, i: (ch, g, tblk(i), 0))
    left = pl.BlockSpec((slabs, None, left_steps * SCAN_GROUP, LANES),
                        lambda g, ch, i: (ch, g, jnp.maximum(tblk(i) * (tt // left_steps) - 1, 0), 0))
    right = pl.BlockSpec((slabs, None, right_steps * SCAN_GROUP, LANES),
                         lambda g, ch, i: (ch, g, jnp.minimum((tblk(i) + 1) * tt, seq - 1), 0))
    per_ch = lambda rows: pl.BlockSpec((rows, LRU_CHUNK), lambda g, ch, i: (0, ch))
    per_dir = pl.BlockSpec((None, 1, LRU_CHUNK), lambda g, ch, i: (d, 0, ch))
    return pl.pallas_call(
        functools.partial(_lru_kernel, nt=nt, reverse=reverse),
        grid=(groups, n_chunks, nt),
        in_specs=[left, cur, right, per_ch(CONV_W), per_ch(1),
                  pl.BlockSpec((None, None, LRU_CHUNK, 2 * LRU_CHUNK), lambda g, ch, i: (d, ch, 0, 0)),
                  per_dir, per_dir, per_dir] + ([cur] if reverse else []),
        out_specs=cur,
        out_shape=jax.ShapeDtypeStruct(xr.shape, F32),
        input_output_aliases={9: 0} if reverse else {},
        scratch_shapes=[pltpu.VMEM(((tt + CONV_W - 1) * SCAN_GROUP, LRU_CHUNK), F32),
                        pltpu.VMEM((SCAN_GROUP, LRU_CHUNK), F32)],
        compiler_params=_params(("parallel", "parallel", "arbitrary")),
        name="lru_bwd" if reverse else "lru_fwd",
    )(xr, xr, xr, conv_w, conv_b.reshape(1, c), w_gate,
      ba.reshape(2, 1, c), bx.reshape(2, 1, c), lam.reshape(2, 1, c), *([forward_out] if reverse else []))


def _lru_gate_weights(wa, wx):
    per = LRU_CHUNK // LRU_BLOCK_DIM

    def dense(w):
        n_dir, n_blk, bd, _ = w.shape
        w = w.reshape(n_dir, n_blk // per, per, bd, bd)
        eye = jnp.eye(per, dtype=w.dtype)
        full = jnp.einsum('dcpij,pq->dcpiqj', w, eye)
        return full.reshape(n_dir, n_blk // per, LRU_CHUNK, LRU_CHUNK)

    return jnp.concatenate([dense(wa), dense(wx)], axis=-1).astype(BF16)


def _mixout_kernel(h_ref, na_ref, yr_ref, gna_ref, glru_ref, hr_ref, wn_ref, wl_ref, wo_ref, o_ref):
    nseq, tq, d = h_ref.shape
    rows = lambda ref: ref[...].reshape(nseq * tq, d)

    na = jnp.concatenate(
        [jnp.concatenate([na_ref[p, j] for j in range(nseq)], axis=0) for p in range(N_PAIRS)], axis=-1)
    na_p = jnp.dot(na, wn_ref[...], preferred_element_type=F32)
    hr = jnp.concatenate(
        [jnp.concatenate([hr_ref[s, pl.ds(j, tq, stride=nseq), :] for j in range(nseq)], axis=0)
         for s in range(d // LANES)], axis=-1)
    lru_in = (_gelu_tanh(rows(yr_ref).astype(F32)) * hr).astype(BF16)
    lru_p = jnp.dot(lru_in, wl_ref[...], preferred_element_type=F32)
    merged = (jax.nn.sigmoid(rows(gna_ref).astype(F32)) * na_p
              + jax.nn.sigmoid(rows(glru_ref).astype(F32)) * lru_p)
    out = rows(h_ref) + jnp.dot(merged.astype(BF16), wo_ref[...], preferred_element_type=F32)
    o_ref[...] = out.reshape(nseq, tq, d)


def _mixout(h, na, yr, gna, glru, hr, wn, wl, wo):
    n, d = h.shape
    _, batch, seq, _ = na.shape
    groups = batch // SCAN_GROUP
    tq = _seq_tile(seq)
    n_slabs = d // LANES
    tile = pl.BlockSpec((SCAN_GROUP, tq, d), lambda g_, i: (g_, i, 0))
    pair = pl.BlockSpec((N_PAIRS, SCAN_GROUP, tq, LANES), lambda g_, i: (0, g_, i, 0))
    slab = pl.BlockSpec((n_slabs, None, tq * SCAN_GROUP, LANES), lambda g_, i: (0, g_, i, 0))
    out = pl.pallas_call(
        _mixout_kernel,
        grid=(groups, seq // tq),
        in_specs=[tile, pair, tile, tile, tile, slab,
                  _const_spec((d, d)), _const_spec((d, d)), _const_spec((d, d))],
        out_specs=tile,
        out_shape=jax.ShapeDtypeStruct((batch, seq, d), F32),
        input_output_aliases={0: 0},
        compiler_params=_params(("parallel", "parallel")),
        name="mixout",
    )(h.reshape(batch, seq, d), na, yr, gna, glru, hr, wn, wl, wo)
    return out.reshape(n, d)


def _prepare_layer(l, p, rows):
    d = D_MODEL
    row = lambda a: a[l].reshape(1, d).astype(F32)
    return dict(
        norm_ffn1=row(p["norm_ffn1"]), norm_mix=row(p["norm_mix"]), norm_ffn2=row(p["norm_ffn2"]),
        ffn1=(p["ffn1_w_gate"][l].astype(BF16), p["ffn1_w_up"][l].astype(BF16), p["ffn1_w_down"][l].astype(BF16)),
        ffn2=(p["ffn2_w_gate"][l].astype(BF16), p["ffn2_w_up"][l].astype(BF16), p["ffn2_w_down"][l].astype(BF16)),
        w_in=p["w_in"][l].astype(BF16),
        bias_tbl=_attn_bias_table(p["na_rel_bias"][l], rows),
        conv_w=p["conv_w"][l].astype(F32), conv_b=p["conv_b"][l].astype(F32),
        w_gate=_lru_gate_weights(p["lru_wa"][l], p["lru_wx"][l]),
        ba=p["lru_ba"][l].astype(F32), bx=p["lru_bx"][l].astype(F32), lam=p["lru_lambda"][l].astype(F32),
        w_na_proj=p["w_na_proj"][l].astype(BF16), w_lru_proj=p["w_lru_proj"][l].astype(BF16),
        w_out=p["w_out"][l].astype(BF16),
    )


def _encode(x, meta_tokens, layers, final_norm):
    batch, t, d = x.shape
    seq = N_META + t
    assert batch % SCAN_GROUP == 0 and seq % N_META == 0
    groups = batch // SCAN_GROUP
    gf = final_norm.reshape(1, d).astype(F32)
    h = None
    for li, lp in enumerate(layers):
        if li == 0:
            h = _ffn_embed(x, meta_tokens, lp["norm_ffn1"], *lp["ffn1"])
        else:
            h = _ffn(h, lp["norm_ffn1"], *lp["ffn1"])
        q, k, v, xr, yr, gna, glru = _inproj(h, lp["norm_mix"], lp["w_in"], groups, seq)
        na = _attention(q, k, v, lp["bias_tbl"])
        lru_args = (xr, lp["conv_w"], lp["conv_b"], lp["w_gate"], lp["ba"], lp["bx"], lp["lam"], seq)
        hr = _lru(*lru_args, forward_out=_lru(*lru_args))
        h = _mixout(h, na, yr, gna, glru, hr, lp["w_na_proj"], lp["w_lru_proj"], lp["w_out"])
        if li < len(layers) - 1:
            h = _ffn(h, lp["norm_ffn2"], *lp["ffn2"])
    return _ffn_final(h, layers[-1]["norm_ffn2"], *layers[-1]["ffn2"], gf, batch, t)


def kernel(x_prompt, x_sample, meta_tokens, norm_ffn1, ffn1_w_gate, ffn1_w_up, ffn1_w_down, norm_mix, w_in, na_rel_bias, conv_w, conv_b, lru_wa, lru_ba, lru_wx, lru_bx, lru_lambda, w_na_proj, w_lru_proj, w_out, norm_ffn2, ffn2_w_gate, ffn2_w_up, ffn2_w_down, final_norm):
    p = dict(norm_ffn1=norm_ffn1, ffn1_w_gate=ffn1_w_gate, ffn1_w_up=ffn1_w_up, ffn1_w_down=ffn1_w_down,
             norm_mix=norm_mix, w_in=w_in, na_rel_bias=na_rel_bias, conv_w=conv_w, conv_b=conv_b,
             lru_wa=lru_wa, lru_ba=lru_ba, lru_wx=lru_wx, lru_bx=lru_bx, lru_lambda=lru_lambda,
             w_na_proj=w_na_proj, w_lru_proj=w_lru_proj, w_out=w_out, norm_ffn2=norm_ffn2,
             ffn2_w_gate=ffn2_w_gate, ffn2_w_up=ffn2_w_up, ffn2_w_down=ffn2_w_down)
    rows = x_prompt.shape[1] // GRID_W
    assert _attn_block_cases(rows) == _attn_block_cases(x_sample.shape[1] // GRID_W)
    layers = [_prepare_layer(l, p, rows) for l in range(norm_ffn1.shape[0])]
    y_prompt = _encode(x_prompt, meta_tokens, layers, final_norm)
    y_sample = _encode(x_sample, meta_tokens, layers, final_norm)
    return (y_prompt, y_sample)
```

```python
import functools

import numpy as np
import jax
import jax.numpy as jnp
from jax import lax
from jax.experimental import pallas as pl
from jax.experimental.pallas import tpu as pltpu

F32 = jnp.float32
BF16 = jnp.bfloat16

D_MODEL = 1024
N_META = 16
GRID_W = 64
NA_HEADS = 16
NA_HEAD_DIM = 64
NA_KH = 8
NA_KW = 16
LRU_BLOCK_DIM = 64
CONV_W = 4
CONV_LEFT = 2
LRU_C = 8.0
RMS_EPS = 1e-6
MASK_VALUE = -1e30

LANES = 128
SUBLANES = 8
BF16_ROWS = 16
MXU_TILE = 256
VMEM_LIMIT = 56 * 1024 * 1024
MAX_ROW_TILE = 704
N_PAIRS = NA_HEADS // 2
ATTN_QROWS = 4
ATTN_KROWS = ATTN_QROWS + NA_KH - 1
ATTN_KLOC = ATTN_KROWS * GRID_W
ATTN_KEXT = -(-(ATTN_KLOC + N_META) // MXU_TILE) * MXU_TILE
ATTN_BLOCKS_PER_STEP = 32
LRU_CHUNK = MXU_TILE
SCAN_GROUP = SUBLANES
LRU_TC = 16
LRU_MAX_TT = 704


def _row_tile(n):
    for t in range(MAX_ROW_TILE, 0, -BF16_ROWS):
        if n % t == 0:
            return t
    raise ValueError(f"no row tile for {n} rows")


def _const_spec(shape):
    zeros = (0,) * len(shape)
    return pl.BlockSpec(shape, lambda *_: zeros, pipeline_mode=pl.Buffered(1))


def _params(sem):
    return pltpu.CompilerParams(dimension_semantics=sem, vmem_limit_bytes=VMEM_LIMIT)


def _rms(x, g):
    inv = lax.rsqrt(jnp.mean(x * x, axis=-1, keepdims=True) + RMS_EPS)
    return (x * inv) * g


def _gelu_tanh(x):
    c = np.float32(np.sqrt(2.0 / np.pi))
    return x * (0.5 * (1.0 + jnp.tanh(c * (x + np.float32(0.044715) * (x * x * x)))))


def _ffn_kernel(h_ref, g_ref, wg_ref, wu_ref, wd_ref, gf_ref, o_ref, *, final):
    x = h_ref[...]
    xn = _rms(x, g_ref[...]).astype(BF16)
    gate = jnp.dot(xn, wg_ref[...], preferred_element_type=F32)
    up = jnp.dot(xn, wu_ref[...], preferred_element_type=F32)
    act = (gate * jax.nn.sigmoid(gate) * up).astype(BF16)
    down = jnp.dot(act, wd_ref[...], preferred_element_type=F32)
    y = x + 0.5 * down
    o_ref[...] = _rms(y, gf_ref[...]) if final else y


def _ffn_specs(d, dff):
    return [_const_spec((1, d)), _const_spec((d, dff)), _const_spec((d, dff)), _const_spec((dff, d)),
            _const_spec((1, d))]


def _ffn(h, g, wg, wu, wd):
    n, d = h.shape
    tm = _row_tile(n)
    row = pl.BlockSpec((tm, d), lambda i: (i, 0))
    return pl.pallas_call(
        functools.partial(_ffn_kernel, final=False),
        grid=(n // tm,),
        in_specs=[row] + _ffn_specs(d, wg.shape[1]),
        out_specs=row,
        out_shape=jax.ShapeDtypeStruct((n, d), F32),
        input_output_aliases={0: 0},
        compiler_params=_params(("parallel",)),
        name="ffn",
    )(h, g, wg, wu, wd, g)


def _grid_rows_spec(t, seq, tm, d):
    assert seq % N_META == 0 and tm % N_META == 0
    return pl.BlockSpec((pl.Element(tm), pl.Element(d)),
                        lambda b, j: (pl.multiple_of(b * seq + N_META + j * tm, N_META), 0))


def _ffn_embed_kernel(x_ref, m_ref, g_ref, wg_ref, wu_ref, wd_ref, gf_ref, o_ref, xin_ref):
    tm = x_ref.shape[0]

    @pl.when(pl.program_id(1) == 0)
    def _():
        xin_ref[0:N_META, :] = m_ref[...]
        xin_ref[N_META:, :] = x_ref[0:tm - N_META, :]
        _ffn_kernel(xin_ref, g_ref, wg_ref, wu_ref, wd_ref, gf_ref, o_ref, final=False)

    @pl.when(pl.program_id(1) > 0)
    def _():
        _ffn_kernel(x_ref, g_ref, wg_ref, wu_ref, wd_ref, gf_ref, o_ref, final=False)


def _ffn_embed(x, meta_tokens, g, wg, wu, wd):
    batch, t, d = x.shape
    seq = N_META + t
    tm = _row_tile(seq)
    x_tile = pl.BlockSpec(
        (pl.Element(tm), pl.Element(d)),
        lambda b, j: (pl.multiple_of(b * t + jnp.maximum(j * tm - N_META, 0), N_META), 0))
    return pl.pallas_call(
        _ffn_embed_kernel,
        grid=(batch, seq // tm),
        in_specs=[x_tile, _const_spec((N_META, d))] + _ffn_specs(d, wg.shape[1]),
        out_specs=pl.BlockSpec((tm, d), lambda b, j: (b * (seq // tm) + j, 0)),
        out_shape=jax.ShapeDtypeStruct((batch * seq, d), F32),
        scratch_shapes=[pltpu.VMEM((tm, d), F32)],
        compiler_params=_params(("parallel", "parallel")),
        name="ffn_embed",
    )(x.reshape(batch * t, d), meta_tokens.astype(F32), g, wg, wu, wd, g)


def _ffn_final(h, g, wg, wu, wd, final_g, batch, t):
    n, d = h.shape
    seq = n // batch
    tm = _row_tile(t)
    out = pl.pallas_call(
        functools.partial(_ffn_kernel, final=True),
        grid=(batch, t // tm),
        in_specs=[_grid_rows_spec(t, seq, tm, d)] + _ffn_specs(d, wg.shape[1]),
        out_specs=pl.BlockSpec((tm, d), lambda b, j: (b * (t // tm) + j, 0)),
        out_shape=jax.ShapeDtypeStruct((batch * t, d), F32),
        compiler_params=_params(("parallel", "parallel")),
        name="ffn_final",
    )(h, g, wg, wu, wd, final_g)
    return out.reshape(batch, t, d)


def _seq_tile(seq):
    best = None
    for tq in range(BF16_ROWS, MAX_ROW_TILE // SCAN_GROUP + 1, BF16_ROWS):
        if seq % tq == 0:
            best = tq
    assert best is not None, seq
    return best


def _inproj_kernel(h_ref, g_ref, w_ref, q_ref, k_ref, v_ref, xr_ref, yr_ref, gna_ref, glru_ref):
    nseq, tq, d = h_ref.shape
    xn = _rms(h_ref[...].reshape(nseq * tq, d), g_ref[...]).astype(BF16)

    def proj(j):
        return jnp.dot(xn, w_ref[:, j * d:(j + 1) * d], preferred_element_type=F32)

    zq = (proj(0) * np.float32(NA_HEAD_DIM ** -0.5)).astype(BF16)
    zk = proj(1).astype(BF16)
    zv = proj(2).astype(BF16)
    for p in range(N_PAIRS):
        sl = slice(p * LANES, (p + 1) * LANES)
        for j in range(nseq):
            rows = slice(j * tq, (j + 1) * tq)
            q_ref[p, j] = zq[rows, sl]
            k_ref[p, j] = zk[rows, sl]
            v_ref[p, j] = zv[rows, sl]
    xr = proj(3)
    for s in range(d // LANES):
        for j in range(nseq):
            xr_ref[s, pl.ds(j, tq, stride=nseq), :] = xr[j * tq:(j + 1) * tq, s * LANES:(s + 1) * LANES]
    yr_ref[...] = proj(4).reshape(nseq, tq, d).astype(BF16)
    gna_ref[...] = proj(5).reshape(nseq, tq, d).astype(BF16)
    glru_ref[...] = proj(6).reshape(nseq, tq, d).astype(BF16)


def _inproj(h, g, w_in, groups, seq):
    n, d = h.shape
    batch = groups * SCAN_GROUP
    tq = _seq_tile(seq)
    n_slabs = d // LANES
    tile = pl.BlockSpec((SCAN_GROUP, tq, d), lambda g_, i: (g_, i, 0))
    pair = pl.BlockSpec((N_PAIRS, SCAN_GROUP, tq, LANES), lambda g_, i: (0, g_, i, 0))
    slab = pl.BlockSpec((n_slabs, None, tq * SCAN_GROUP, LANES), lambda g_, i: (0, g_, i, 0))
    pair_shape = jax.ShapeDtypeStruct((N_PAIRS, batch, seq, LANES), BF16)
    tile_shape = jax.ShapeDtypeStruct((batch, seq, d), BF16)
    slab_shape = jax.ShapeDtypeStruct((n_slabs, groups, seq * SCAN_GROUP, LANES), F32)
    return pl.pallas_call(
        _inproj_kernel,
        grid=(groups, seq // tq),
        in_specs=[tile, _const_spec((1, d)), _const_spec(w_in.shape)],
        out_specs=[pair, pair, pair, slab, tile, tile, tile],
        out_shape=[pair_shape, pair_shape, pair_shape, slab_shape, tile_shape, tile_shape, tile_shape],
        compiler_params=_params(("parallel", "parallel")),
        name="inproj",
    )(h.reshape(batch, seq, d), g, w_in)


def _attn_block_cases(rows):
    nb = rows // ATTN_QROWS
    assert rows % ATTN_QROWS == 0 and nb >= 3 and nb % 2 == 0, rows

    def case(m):
        ks = min(max(ATTN_QROWS * m - NA_KH // 2, 0), rows - ATTN_KROWS)
        out = []
        for q in range(ATTN_QROWS):
            r = ATTN_QROWS * m + q
            rs = min(max(r - NA_KH // 2, 0), rows - NA_KH)
            out.append((rs - ks, r - rs))
        return tuple(out)

    assert all(case(m) == case(1) for m in range(1, nb - 1))
    return case(0), case(1), case(nb - 1)


def _attn_bias_table(rel_bias, rows):
    c = np.arange(GRID_W)
    cs = np.clip(c - NA_KW // 2, 0, GRID_W - NA_KW)
    kc = np.arange(GRID_W)
    ok = (kc[None, :] >= cs[:, None]) & (kc[None, :] < cs[:, None] + NA_KW)
    dcol = kc[None, :] - c[:, None] + NA_KW - 1
    onehot = (dcol[:, :, None] == np.arange(2 * NA_KW - 1)).astype(np.float32)
    cols = jnp.einsum('hdj,ckj->hdck', rel_bias.astype(F32), onehot, precision=lax.Precision.HIGHEST)
    cols = jnp.where(ok[None, None], cols, np.float32(MASK_VALUE))
    mask = lambda n: jnp.full((NA_HEADS, n, GRID_W, GRID_W), np.float32(MASK_VALUE))
    types = []
    for cases in _attn_block_cases(rows):
        per_q = []
        for off, e in cases:
            band = cols[:, NA_KH - 1 - e:2 * NA_KH - 1 - e]
            per_q.append(jnp.concatenate([mask(off), band, mask(ATTN_KROWS - NA_KH - off)], axis=1))
        types.append(jnp.stack(per_q, axis=1))
    tbl = jnp.stack(types, axis=1)
    tbl = jnp.transpose(tbl, (0, 1, 2, 4, 3, 5))
    tbl = tbl.reshape(N_PAIRS, 2, 3, ATTN_QROWS * GRID_W, ATTN_KLOC)
    tbl = jnp.transpose(tbl, (0, 2, 1, 3, 4)).reshape(N_PAIRS, 3, 2 * ATTN_QROWS * GRID_W, ATTN_KLOC)
    lead = tbl.shape[:3]
    return jnp.concatenate([tbl, jnp.zeros(lead + (N_META,), F32),
                            jnp.full(lead + (ATTN_KEXT - ATTN_KLOC - N_META,), np.float32(MASK_VALUE))], axis=-1)


def _attn_kernel(q_ref, k_ref, v_ref, bias_ref, o_ref, kx_ref, vx_ref, s_ref, p_ref, den_ref):
    n_seq, seq, _ = q_ref.shape
    rows = (seq - N_META) // GRID_W
    nb = rows // ATTN_QROWS
    n_blocks = n_seq * nb
    nq = ATTN_QROWS * GRID_W
    nt = (((1,), (1,)), ((), ()))
    lo = lax.broadcasted_iota(jnp.int32, (1, LANES), 1) < NA_HEAD_DIM

    def stack_heads(x):
        zero = jnp.zeros_like(x)
        return jnp.concatenate([jnp.where(lo, x, zero), jnp.where(lo, zero, x)], axis=0)

    def pick_heads(o, n):
        return jnp.where(lo, o[:n], o[n:])

    for sq in range(n_seq):
        km = k_ref[sq, 0:N_META, :]
        vm = v_ref[sq, 0:N_META, :]
        qm = stack_heads(q_ref[sq, 0:N_META, :])
        sm = lax.dot_general(qm, km, nt, preferred_element_type=F32)
        em = jnp.exp(sm - jnp.max(sm, axis=-1, keepdims=True))
        om = jnp.dot(em.astype(BF16), vm, preferred_element_type=F32)
        om = om / jnp.sum(em, axis=-1, keepdims=True)
        o_ref[sq, 0:N_META, :] = pick_heads(om, N_META).astype(o_ref.dtype)

    pad = jnp.zeros((ATTN_KEXT - ATTN_KLOC - N_META, LANES), BF16)
    for j in range(2):
        kx_ref[j, ATTN_KLOC + N_META:, :] = pad
        vx_ref[j, ATTN_KLOC + N_META:, :] = pad

    def locate(m):
        sq = m // nb
        local = m - sq * nb
        ks = jnp.clip(ATTN_QROWS * local - NA_KH // 2, 0, rows - ATTN_KROWS)
        q0 = pl.multiple_of(N_META + local * nq, BF16_ROWS)
        k0 = pl.multiple_of(N_META + ks * GRID_W, BF16_ROWS)
        return sq, local, q0, k0

    def scores(m, j):
        sq, local, q0, k0 = locate(m)
        kx_ref[j, 0:ATTN_KLOC, :] = k_ref[sq, pl.ds(k0, ATTN_KLOC), :]
        kx_ref[j, ATTN_KLOC:ATTN_KLOC + N_META, :] = k_ref[sq, 0:N_META, :]
        qs = stack_heads(q_ref[sq, pl.ds(q0, nq), :])
        kind = jnp.where(local == 0, 0, jnp.where(local == nb - 1, 2, 1))
        s_ref[j] = lax.dot_general(qs, kx_ref[j], nt, preferred_element_type=F32) + bias_ref[kind]

    def softmax(j):
        s = s_ref[j]
        e = jnp.exp(s - jnp.max(s, axis=-1, keepdims=True))
        den_ref[j] = jnp.sum(e, axis=-1, keepdims=True)
        p_ref[j] = e.astype(BF16)

    def values(m, j):
        sq, _, q0, k0 = locate(m)
        vx_ref[j, 0:ATTN_KLOC, :] = v_ref[sq, pl.ds(k0, ATTN_KLOC), :]
        vx_ref[j, ATTN_KLOC:ATTN_KLOC + N_META, :] = v_ref[sq, 0:N_META, :]
        o = jnp.dot(p_ref[j], vx_ref[j], preferred_element_type=F32) / den_ref[j]
        o_ref[sq, pl.ds(q0, nq), :] = pick_heads(o, nq).astype(o_ref.dtype)

    scores(0, 0)
    scores(1, 1)
    softmax(0)

    def two_blocks(i, carry):
        m = 2 * i + 2
        scores(m, 0)
        softmax(1)
        values(m - 2, 0)
        scores(m + 1, 1)
        softmax(0)
        values(m - 1, 1)
        return carry

    lax.fori_loop(0, n_blocks // 2 - 1, two_blocks, 0)
    softmax(1)
    values(n_blocks - 2, 0)
    values(n_blocks - 1, 1)


def _attention(q, k, v, bias_tbl):
    _, batch, seq, _ = q.shape
    blocks_per_seq = (seq - N_META) // (GRID_W * ATTN_QROWS)
    n_seq = max(1, min(batch, ATTN_BLOCKS_PER_STEP // blocks_per_seq))
    assert batch % n_seq == 0
    blk = pl.BlockSpec((None, n_seq, seq, LANES), lambda p, b: (p, b, 0, 0))
    bias_spec = pl.BlockSpec((None,) + bias_tbl.shape[1:], lambda p, b: (p, 0, 0, 0))
    out = pl.pallas_call(
        _attn_kernel,
        grid=(N_PAIRS, batch // n_seq),
        in_specs=[blk, blk, blk, bias_spec],
        out_specs=blk,
        out_shape=jax.ShapeDtypeStruct(q.shape, BF16),
        scratch_shapes=[pltpu.VMEM((2, ATTN_KEXT, LANES), BF16), pltpu.VMEM((2, ATTN_KEXT, LANES), BF16),
                        pltpu.VMEM((2, 2 * ATTN_QROWS * GRID_W, ATTN_KEXT), F32),
                        pltpu.VMEM((2, 2 * ATTN_QROWS * GRID_W, ATTN_KEXT), BF16),
                        pltpu.VMEM((2, 2 * ATTN_QROWS * GRID_W, 1), F32)],
        compiler_params=_params(("parallel", "parallel")),
        name="attention",
    )(q, k, v, bias_tbl)
    return out


def _scan_time_block(seq):
    best = None
    for tt in range(LRU_TC, LRU_MAX_TT + 1, LRU_TC):
        if seq % tt == 0:
            best = tt
    assert best is not None, seq
    return best


def _lru_kernel(xl_ref, x_ref, xn_ref, cw_ref, cb_ref, w_ref, ba_ref, bx_ref, lam_ref, *rest, nt, reverse):
    acc_ref, o_ref, x_s, carry = rest if reverse else (None,) + rest
    step = pl.program_id(2)
    n_slabs, n_rows, _ = x_ref.shape
    blk = (nt - 1 - step) if reverse else step
    has_left = blk > 0
    has_right = blk < nt - 1
    left_rows = xl_ref.shape[1]
    chunk_rows = LRU_TC * SCAN_GROUP
    n_chunks = n_rows // chunk_rows

    @pl.when(step == 0)
    def _():
        carry[...] = jnp.zeros_like(carry)

    for s in range(n_slabs):
        sl = slice(s * LANES, (s + 1) * LANES)
        x_s[0:left_rows, sl] = jnp.where(has_left, xl_ref[s], jnp.zeros_like(xl_ref[s]))
        x_s[left_rows:left_rows + n_rows, sl] = x_ref[s]
        x_s[left_rows + n_rows:, sl] = jnp.where(has_right, xn_ref[s], jnp.zeros_like(xn_ref[s]))

    lam = lam_ref[...]
    softplus = jnp.maximum(-lam, 0.0) + jnp.log1p(jnp.exp(-jnp.abs(lam)))
    neg_c_sp = np.float32(-LRU_C) * softplus
    cw = cw_ref[...]
    cb = cb_ref[...]
    ba = ba_ref[...]
    bx = bx_ref[...]

    def chunk(i, h):
        c = (n_chunks - 1 - i) if reverse else i
        row0 = pl.multiple_of(c * chunk_rows, chunk_rows)
        xc = cb
        for j in range(CONV_W):
            xc = xc + x_s[pl.ds(row0 + j * SCAN_GROUP, chunk_rows), :] * cw[j:j + 1, :]
        pre = jnp.dot(xc.astype(BF16), w_ref[...], preferred_element_type=F32)
        r = jax.nn.sigmoid(pre[:, :LRU_CHUNK] + ba)
        gi = jax.nn.sigmoid(pre[:, LRU_CHUNK:] + bx)
        log_a = neg_c_sp * r
        a = jnp.exp(log_a)
        y = -jnp.tanh(log_a) * (a * a + 1.0)
        root = jnp.where(y > 0.0, y * lax.rsqrt(y), 0.0)
        u = root * (gi * xc)
        hs = [None] * LRU_TC
        order = range(LRU_TC - 1, -1, -1) if reverse else range(LRU_TC)
        for k in order:
            rows = slice(k * SCAN_GROUP, (k + 1) * SCAN_GROUP)
            h = a[rows] * h + u[rows]
            hs[k] = h
        hc = jnp.concatenate(hs, axis=0)
        for s in range(n_slabs):
            part = hc[:, s * LANES:(s + 1) * LANES]
            if reverse:
                part = part + acc_ref[s, pl.ds(row0, chunk_rows), :]
            o_ref[s, pl.ds(row0, chunk_rows), :] = part
        return h

    carry[...] = lax.fori_loop(0, n_chunks, chunk, carry[...], unroll=4)


def _lru(xr, conv_w, conv_b, w_gate, ba, bx, lam, seq, forward_out=None):
    reverse = forward_out is not None
    n_slabs, groups, _, _ = xr.shape
    c = n_slabs * LANES
    tt = _scan_time_block(seq)
    nt = seq // tt
    n_chunks = c // LRU_CHUNK
    slabs = LRU_CHUNK // LANES
    d = 1 if reverse else 0
    left_steps = CONV_LEFT
    right_steps = CONV_W - 1 - CONV_LEFT
    assert tt % left_steps == 0 and right_steps == 1

    def tblk(i):
        return (nt - 1 - i) if reverse else i

    cur = pl.BlockSpec((slabs, None, tt * SCAN_GROUP, LANES), lambda g, ch, i: (ch, g, tblk(i), 0))
    left = pl.BlockSpec((slabs, None, left_steps * SCAN_GROUP, LANES),
                        lambda g, ch, i: (ch, g, jnp.maximum(tblk(i) * (tt // left_steps) - 1, 0), 0))
    right = pl.BlockSpec((slabs, None, right_steps * SCAN_GROUP, LANES),
                         lambda g, ch, i: (ch, g, jnp.minimum((tblk(i) + 1) * tt, seq - 1), 0))
    per_ch = lambda rows: pl.BlockSpec((rows, LRU_CHUNK), lambda g, ch, i: (0, ch))
    per_dir = pl.BlockSpec((None, 1, LRU_CHUNK), lambda g, ch, i: (d, 0, ch))
    return pl.pallas_call(
        functools.partial(_lru_kernel, nt=nt, reverse=reverse),
        grid=(groups, n_chunks, nt),
        in_specs=[left, cur, right, per_ch(CONV_W), per_ch(1),
                  pl.BlockSpec((None, None, LRU_CHUNK, 2 * LRU_CHUNK), lambda g, ch, i: (d, ch, 0, 0)),
                  per_dir, per_dir, per_dir] + ([cur] if reverse else []),
        out_specs=cur,
        out_shape=jax.ShapeDtypeStruct(xr.shape, F32),
        input_output_aliases={9: 0} if reverse else {},
        scratch_shapes=[pltpu.VMEM(((tt + CONV_W - 1) * SCAN_GROUP, LRU_CHUNK), F32),
                        pltpu.VMEM((SCAN_GROUP, LRU_CHUNK), F32)],
        compiler_params=_params(("parallel", "parallel", "arbitrary")),
        name="lru_bwd" if reverse else "lru_fwd",
    )(xr, xr, xr, conv_w, conv_b.reshape(1, c), w_gate,
      ba.reshape(2, 1, c), bx.reshape(2, 1, c), lam.reshape(2, 1, c), *([forward_out] if reverse else []))


def _lru_gate_weights(wa, wx):
    per = LRU_CHUNK // LRU_BLOCK_DIM

    def dense(w):
        n_dir, n_blk, bd, _ = w.shape
        w = w.reshape(n_dir, n_blk // per, per, bd, bd)
        eye = jnp.eye(per, dtype=w.dtype)
        full = jnp.einsum('dcpij,pq->dcpiqj', w, eye)
        return full.reshape(n_dir, n_blk // per, LRU_CHUNK, LRU_CHUNK)

    return jnp.concatenate([dense(wa), dense(wx)], axis=-1).astype(BF16)


def _mixout_kernel(h_ref, na_ref, yr_ref, gna_ref, glru_ref, hr_ref, wn_ref, wl_ref, wo_ref, o_ref):
    nseq, tq, d = h_ref.shape
    rows = lambda ref: ref[...].reshape(nseq * tq, d)

    na = jnp.concatenate(
        [jnp.concatenate([na_ref[p, j] for j in range(nseq)], axis=0) for p in range(N_PAIRS)], axis=-1)
    na_p = jnp.dot(na, wn_ref[...], preferred_element_type=F32)
    hr = jnp.concatenate(
        [jnp.concatenate([hr_ref[s, pl.ds(j, tq, stride=nseq), :] for j in range(nseq)], axis=0)
         for s in range(d // LANES)], axis=-1)
    lru_in = (_gelu_tanh(rows(yr_ref).astype(F32)) * hr).astype(BF16)
    lru_p = jnp.dot(lru_in, wl_ref[...], preferred_element_type=F32)
    merged = (jax.nn.sigmoid(rows(gna_ref).astype(F32)) * na_p
              + jax.nn.sigmoid(rows(glru_ref).astype(F32)) * lru_p)
    out = rows(h_ref) + jnp.dot(merged.astype(BF16), wo_ref[...], preferred_element_type=F32)
    o_ref[...] = out.reshape(nseq, tq, d)


def _mixout(h, na, yr, gna, glru, hr, wn, wl, wo):
    n, d = h.shape
    _, batch, seq, _ = na.shape
    groups = batch // SCAN_GROUP
    tq = _seq_tile(seq)
    n_slabs = d // LANES
    tile = pl.BlockSpec((SCAN_GROUP, tq, d), lambda g_, i: (g_, i, 0))
    pair = pl.BlockSpec((N_PAIRS, SCAN_GROUP, tq, LANES), lambda g_, i: (0, g_, i, 0))
    slab = pl.BlockSpec((n_slabs, None, tq * SCAN_GROUP, LANES), lambda g_, i: (0, g_, i, 0))
    out = pl.pallas_call(
        _mixout_kernel,
        grid=(groups, seq // tq),
        in_specs=[tile, pair, tile, tile, tile, slab,
                  _const_spec((d, d)), _const_spec((d, d)), _const_spec((d, d))],
        out_specs=tile,
        out_shape=jax.ShapeDtypeStruct((batch, seq, d), F32),
        input_output_aliases={0: 0},
        compiler_params=_params(("parallel", "parallel")),
        name="mixout",
    )(h.reshape(batch, seq, d), na, yr, gna, glru, hr, wn, wl, wo)
    return out.reshape(n, d)


def _prepare_layer(l, p, rows):
    d = D_MODEL
    row = lambda a: a[l].reshape(1, d).astype(F32)
    return dict(
        norm_ffn1=row(p["norm_ffn1"]), norm_mix=row(p["norm_mix"]), norm_ffn2=row(p["norm_ffn2"]),
        ffn1=(p["ffn1_w_gate"][l].astype(BF16), p["ffn1_w_up"][l].astype(BF16), p["ffn1_w_down"][l].astype(BF16)),
        ffn2=(p["ffn2_w_gate"][l].astype(BF16), p["ffn2_w_up"][l].astype(BF16), p["ffn2_w_down"][l].astype(BF16)),
        w_in=p["w_in"][l].astype(BF16),
        bias_tbl=_attn_bias_table(p["na_rel_bias"][l], rows),
        conv_w=p["conv_w"][l].astype(F32), conv_b=p["conv_b"][l].astype(F32),
        w_gate=_lru_gate_weights(p["lru_wa"][l], p["lru_wx"][l]),
        ba=p["lru_ba"][l].astype(F32), bx=p["lru_bx"][l].astype(F32), lam=p["lru_lambda"][l].astype(F32),
        w_na_proj=p["w_na_proj"][l].astype(BF16), w_lru_proj=p["w_lru_proj"][l].astype(BF16),
        w_out=p["w_out"][l].astype(BF16),
    )


def _encode(x, meta_tokens, layers, final_norm):
    batch, t, d = x.shape
    seq = N_META + t
    assert batch % SCAN_GROUP == 0 and seq % N_META == 0
    groups = batch // SCAN_GROUP
    gf = final_norm.reshape(1, d).astype(F32)
    h = None
    for li, lp in enumerate(layers):
        if li == 0:
            h = _ffn_embed(x, meta_tokens, lp["norm_ffn1"], *lp["ffn1"])
        else:
            h = _ffn(h, lp["norm_ffn1"], *lp["ffn1"])
        q, k, v, xr, yr, gna, glru = _inproj(h, lp["norm_mix"], lp["w_in"], groups, seq)
        na = _attention(q, k, v, lp["bias_tbl"])
        lru_args = (xr, lp["conv_w"], lp["conv_b"], lp["w_gate"], lp["ba"], lp["bx"], lp["lam"], seq)
        hr = _lru(*lru_args, forward_out=_lru(*lru_args))
        h = _mixout(h, na, yr, gna, glru, hr, lp["w_na_proj"], lp["w_lru_proj"], lp["w_out"])
        if li < len(layers) - 1:
            h = _ffn(h, lp["norm_ffn2"], *lp["ffn2"])
    return _ffn_final(h, layers[-1]["norm_ffn2"], *layers[-1]["ffn2"], gf, batch, t)


def kernel(x_prompt, x_sample, meta_tokens, norm_ffn1, ffn1_w_gate, ffn1_w_up, ffn1_w_down, norm_mix, w_in, na_rel_bias, conv_w, conv_b, lru_wa, lru_ba, lru_wx, lru_bx, lru_lambda, w_na_proj, w_lru_proj, w_out, norm_ffn2, ffn2_w_gate, ffn2_w_up, ffn2_w_down, final_norm):
    p = dict(norm_ffn1=norm_ffn1, ffn1_w_gate=ffn1_w_gate, ffn1_w_up=ffn1_w_up, ffn1_w_down=ffn1_w_down,
             norm_mix=norm_mix, w_in=w_in, na_rel_bias=na_rel_bias, conv_w=conv_w, conv_b=conv_b,
             lru_wa=lru_wa, lru_ba=lru_ba, lru_wx=lru_wx, lru_bx=lru_bx, lru_lambda=lru_lambda,
             w_na_proj=w_na_proj, w_lru_proj=w_lru_proj, w_out=w_out, norm_ffn2=norm_ffn2,
             ffn2_w_gate=ffn2_w_gate, ffn2_w_up=ffn2_w_up, ffn2_w_down=ffn2_w_down)
    rows = x_prompt.shape[1] // GRID_W
    assert _attn_block_cases(rows) == _attn_block_cases(x_sample.shape[1] // GRID_W)
    layers = [_prepare_layer(l, p, rows) for l in range(norm_ffn1.shape[0])]
    y_prompt = _encode(x_prompt, meta_tokens, layers, final_norm)
    y_sample = _encode(x_sample, meta_tokens, layers, final_norm)
    return (y_prompt, y_sample)
```

```python
import functools

import numpy as np
import jax
import jax.numpy as jnp
from jax import lax
from jax.experimental import pallas as pl
from jax.experimental.pallas import tpu as pltpu

F32 = jnp.float32
BF16 = jnp.bfloat16

D_MODEL = 1024
N_META = 16
GRID_W = 64
NA_HEADS = 16
NA_HEAD_DIM = 64
NA_KH = 8
NA_KW = 16
LRU_BLOCK_DIM = 64
CONV_W = 4
CONV_LEFT = 2
LRU_C = 8.0
RMS_EPS = 1e-6
MASK_VALUE = -1e30

LANES = 128
SUBLANES = 8
BF16_ROWS = 16
MXU_TILE = 256
VMEM_LIMIT = 56 * 1024 * 1024
MAX_ROW_TILE = 704
N_PAIRS = NA_HEADS // 2
ATTN_QROWS = 4
ATTN_KROWS = ATTN_QROWS + NA_KH - 1
ATTN_KLOC = ATTN_KROWS * GRID_W
ATTN_KEXT = -(-(ATTN_KLOC + N_META) // MXU_TILE) * MXU_TILE
ATTN_BLOCKS_PER_STEP = 32
LRU_CHUNK = MXU_TILE
SCAN_GROUP = SUBLANES
LRU_TC = 16
LRU_MAX_TT = 704


def _row_tile(n):
    for t in range(MAX_ROW_TILE, 0, -BF16_ROWS):
        if n % t == 0:
            return t
    raise ValueError(f"no row tile for {n} rows")


def _const_spec(shape):
    zeros = (0,) * len(shape)
    return pl.BlockSpec(shape, lambda *_: zeros, pipeline_mode=pl.Buffered(1))


def _params(sem):
    return pltpu.CompilerParams(dimension_semantics=sem, vmem_limit_bytes=VMEM_LIMIT)


def _rms(x, g):
    inv = lax.rsqrt(jnp.mean(x * x, axis=-1, keepdims=True) + RMS_EPS)
    return (x * inv) * g


def _gelu_tanh(x):
    c = np.float32(np.sqrt(2.0 / np.pi))
    return x * (0.5 * (1.0 + jnp.tanh(c * (x + np.float32(0.044715) * (x * x * x)))))


def _ffn_kernel(h_ref, g_ref, wg_ref, wu_ref, wd_ref, gf_ref, o_ref, *, final):
    x = h_ref[...]
    xn = _rms(x, g_ref[...]).astype(BF16)
    gate = jnp.dot(xn, wg_ref[...], preferred_element_type=F32)
    up = jnp.dot(xn, wu_ref[...], preferred_element_type=F32)
    act = (gate * jax.nn.sigmoid(gate) * up).astype(BF16)
    down = jnp.dot(act, wd_ref[...], preferred_element_type=F32)
    y = x + 0.5 * down
    o_ref[...] = _rms(y, gf_ref[...]) if final else y


def _ffn_specs(d, dff):
    return [_const_spec((1, d)), _const_spec((d, dff)), _const_spec((d, dff)), _const_spec((dff, d)),
            _const_spec((1, d))]


def _ffn(h, g, wg, wu, wd):
    n, d = h.shape
    tm = _row_tile(n)
    row = pl.BlockSpec((tm, d), lambda i: (i, 0))
    return pl.pallas_call(
        functools.partial(_ffn_kernel, final=False),
        grid=(n // tm,),
        in_specs=[row] + _ffn_specs(d, wg.shape[1]),
        out_specs=row,
        out_shape=jax.ShapeDtypeStruct((n, d), F32),
        input_output_aliases={0: 0},
        compiler_params=_params(("parallel",)),
        name="ffn",
    )(h, g, wg, wu, wd, g)


def _grid_rows_spec(t, seq, tm, d):
    assert seq % N_META == 0 and tm % N_META == 0
    return pl.BlockSpec((pl.Element(tm), pl.Element(d)),
                        lambda b, j: (pl.multiple_of(b * seq + N_META + j * tm, N_META), 0))


def _ffn_embed_kernel(x_ref, m_ref, g_ref, wg_ref, wu_ref, wd_ref, gf_ref, o_ref, xin_ref):
    tm = x_ref.shape[0]

    @pl.when(pl.program_id(1) == 0)
    def _():
        xin_ref[0:N_META, :] = m_ref[...]
        xin_ref[N_META:, :] = x_ref[0:tm - N_META, :]
        _ffn_kernel(xin_ref, g_ref, wg_ref, wu_ref, wd_ref, gf_ref, o_ref, final=False)

    @pl.when(pl.program_id(1) > 0)
    def _():
        _ffn_kernel(x_ref, g_ref, wg_ref, wu_ref, wd_ref, gf_ref, o_ref, final=False)


def _ffn_embed(x, meta_tokens, g, wg, wu, wd):
    batch, t, d = x.shape
    seq = N_META + t
    tm = _row_tile(seq)
    x_tile = pl.BlockSpec(
        (pl.Element(tm), pl.Element(d)),
        lambda b, j: (pl.multiple_of(b * t + jnp.maximum(j * tm - N_META, 0), N_META), 0))
    return pl.pallas_call(
        _ffn_embed_kernel,
        grid=(batch, seq // tm),
        in_specs=[x_tile, _const_spec((N_META, d))] + _ffn_specs(d, wg.shape[1]),
        out_specs=pl.BlockSpec((tm, d), lambda b, j: (b * (seq // tm) + j, 0)),
        out_shape=jax.ShapeDtypeStruct((batch * seq, d), F32),
        scratch_shapes=[pltpu.VMEM((tm, d), F32)],
        compiler_params=_params(("parallel", "parallel")),
        name="ffn_embed",
    )(x.reshape(batch * t, d), meta_tokens.astype(F32), g, wg, wu, wd, g)


def _ffn_final(h, g, wg, wu, wd, final_g, batch, t):
    n, d = h.shape
    seq = n // batch
    tm = _row_tile(t)
    out = pl.pallas_call(
        functools.partial(_ffn_kernel, final=True),
        grid=(batch, t // tm),
        in_specs=[_grid_rows_spec(t, seq, tm, d)] + _ffn_specs(d, wg.shape[1]),
        out_specs=pl.BlockSpec((tm, d), lambda b, j: (b * (t // tm) + j, 0)),
        out_shape=jax.ShapeDtypeStruct((batch * t, d), F32),
        compiler_params=_params(("parallel", "parallel")),
        name="ffn_final",
    )(h, g, wg, wu, wd, final_g)
    return out.reshape(batch, t, d)


def _seq_tile(seq):
    best = None
    for tq in range(BF16_ROWS, MAX_ROW_TILE // SCAN_GROUP + 1, BF16_ROWS):
        if seq % tq == 0:
            best = tq
    assert best is not None, seq
    return best


def _inproj_kernel(h_ref, g_ref, w_ref, q_ref, k_ref, v_ref, xr_ref, yr_ref, gna_ref, glru_ref):
    nseq, tq, d = h_ref.shape
    xn = _rms(h_ref[...].reshape(nseq * tq, d), g_ref[...]).astype(BF16)

    def proj(j):
        return jnp.dot(xn, w_ref[:, j * d:(j + 1) * d], preferred_element_type=F32)

    zq = (proj(0) * np.float32(NA_HEAD_DIM ** -0.5)).astype(BF16)
    zk = proj(1).astype(BF16)
    zv = proj(2).astype(BF16)
    for p in range(N_PAIRS):
        sl = slice(p * LANES, (p + 1) * LANES)
        for j in range(nseq):
            rows = slice(j * tq, (j + 1) * tq)
            q_ref[p, j] = zq[rows, sl]
            k_ref[p, j] = zk[rows, sl]
            v_ref[p, j] = zv[rows, sl]
    xr = proj(3)
    for s in range(d // LANES):
        for j in range(nseq):
            xr_ref[s, pl.ds(j, tq, stride=nseq), :] = xr[j * tq:(j + 1) * tq, s * LANES:(s + 1) * LANES]
    yr_ref[...] = proj(4).reshape(nseq, tq, d).astype(BF16)
    gna_ref[...] = proj(5).reshape(nseq, tq, d).astype(BF16)
    glru_ref[...] = proj(6).reshape(nseq, tq, d).astype(BF16)


def _inproj(h, g, w_in, groups, seq):
    n, d = h.shape
    batch = groups * SCAN_GROUP
    tq = _seq_tile(seq)
    n_slabs = d // LANES
    tile = pl.BlockSpec((SCAN_GROUP, tq, d), lambda g_, i: (g_, i, 0))
    pair = pl.BlockSpec((N_PAIRS, SCAN_GROUP, tq, LANES), lambda g_, i: (0, g_, i, 0))
    slab = pl.BlockSpec((n_slabs, None, tq * SCAN_GROUP, LANES), lambda g_, i: (0, g_, i, 0))
    pair_shape = jax.ShapeDtypeStruct((N_PAIRS, batch, seq, LANES), BF16)
    tile_shape = jax.ShapeDtypeStruct((batch, seq, d), BF16)
    slab_shape = jax.ShapeDtypeStruct((n_slabs, groups, seq * SCAN_GROUP, LANES), F32)
    return pl.pallas_call(
        _inproj_kernel,
        grid=(groups, seq // tq),
        in_specs=[tile, _const_spec((1, d)), _const_spec(w_in.shape)],
        out_specs=[pair, pair, pair, slab, tile, tile, tile],
        out_shape=[pair_shape, pair_shape, pair_shape, slab_shape, tile_shape, tile_shape, tile_shape],
        compiler_params=_params(("parallel", "parallel")),
        name="inproj",
    )(h.reshape(batch, seq, d), g, w_in)


def _attn_block_cases(rows):
    nb = rows // ATTN_QROWS
    assert rows % ATTN_QROWS == 0 and nb >= 3 and nb % 2 == 0, rows

    def case(m):
        ks = min(max(ATTN_QROWS * m - NA_KH // 2, 0), rows - ATTN_KROWS)
        out = []
        for q in range(ATTN_QROWS):
            r = ATTN_QROWS * m + q
            rs = min(max(r - NA_KH // 2, 0), rows - NA_KH)
            out.append((rs - ks, r - rs))
        return tuple(out)

    assert all(case(m) == case(1) for m in range(1, nb - 1))
    return case(0), case(1), case(nb - 1)


def _attn_bias_table(rel_bias, rows):
    c = np.arange(GRID_W)
    cs = np.clip(c - NA_KW // 2, 0, GRID_W - NA_KW)
    kc = np.arange(GRID_W)
    ok = (kc[None, :] >= cs[:, None]) & (kc[None, :] < cs[:, None] + NA_KW)
    dcol = kc[None, :] - c[:, None] + NA_KW - 1
    onehot = (dcol[:, :, None] == np.arange(2 * NA_KW - 1)).astype(np.float32)
    cols = jnp.einsum('hdj,ckj->hdck', rel_bias.astype(F32), onehot, precision=lax.Precision.HIGHEST)
    cols = jnp.where(ok[None, None], cols, np.float32(MASK_VALUE))
    masked = jnp.full((NA_HEADS, GRID_W, GRID_W), np.float32(MASK_VALUE))
    tail = jnp.concatenate([jnp.zeros((NA_HEADS, GRID_W, N_META), F32),
                            jnp.full((NA_HEADS, GRID_W, ATTN_KEXT - ATTN_KLOC - N_META), np.float32(MASK_VALUE))],
                           axis=-1)
    per_case = []
    for cases in _attn_block_cases(rows):
        for off, e in cases:
            pieces = [cols[:, i - off + NA_KH - 1 - e] if 0 <= i - off < NA_KH else masked
                      for i in range(ATTN_KROWS)]
            per_case.append(jnp.concatenate(pieces + [tail], axis=-1))
    tbl = jnp.stack(per_case, axis=1)
    tbl = tbl.reshape(N_PAIRS, 2, 3, ATTN_QROWS, GRID_W, ATTN_KEXT)
    return jnp.transpose(tbl, (0, 2, 1, 3, 4, 5)).reshape(N_PAIRS, 3, 2 * ATTN_QROWS * GRID_W, ATTN_KEXT)


def _attn_kernel(q_ref, k_ref, v_ref, bias_ref, o_ref, kx_ref, vx_ref, s_ref, p_ref, den_ref):
    n_seq, seq, _ = q_ref.shape
    rows = (seq - N_META) // GRID_W
    nb = rows // ATTN_QROWS
    n_blocks = n_seq * nb
    nq = ATTN_QROWS * GRID_W
    nt = (((1,), (1,)), ((), ()))
    lo = lax.broadcasted_iota(jnp.int32, (1, LANES), 1) < NA_HEAD_DIM

    def stack_heads(x):
        zero = jnp.zeros_like(x)
        return jnp.concatenate([jnp.where(lo, x, zero), jnp.where(lo, zero, x)], axis=0)

    def pick_heads(o, n):
        return jnp.where(lo, o[:n], o[n:])

    for sq in range(n_seq):
        km = k_ref[sq, 0:N_META, :]
        vm = v_ref[sq, 0:N_META, :]
        qm = stack_heads(q_ref[sq, 0:N_META, :])
        sm = lax.dot_general(qm, km, nt, preferred_element_type=F32)
        em = jnp.exp(sm - jnp.max(sm, axis=-1, keepdims=True))
        om = jnp.dot(em.astype(BF16), vm, preferred_element_type=F32)
        om = om / jnp.sum(em, axis=-1, keepdims=True)
        o_ref[sq, 0:N_META, :] = pick_heads(om, N_META).astype(o_ref.dtype)

    pad = jnp.zeros((ATTN_KEXT - ATTN_KLOC - N_META, LANES), BF16)
    for j in range(2):
        kx_ref[j, ATTN_KLOC + N_META:, :] = pad
        vx_ref[j, ATTN_KLOC + N_META:, :] = pad

    def locate(m):
        sq = m // nb
        local = m - sq * nb
        ks = jnp.clip(ATTN_QROWS * local - NA_KH // 2, 0, rows - ATTN_KROWS)
        q0 = pl.multiple_of(N_META + local * nq, BF16_ROWS)
        k0 = pl.multiple_of(N_META + ks * GRID_W, BF16_ROWS)
        return sq, local, q0, k0

    def scores(m, j):
        sq, local, q0, k0 = locate(m)
        kx_ref[j, 0:ATTN_KLOC, :] = k_ref[sq, pl.ds(k0, ATTN_KLOC), :]
        kx_ref[j, ATTN_KLOC:ATTN_KLOC + N_META, :] = k_ref[sq, 0:N_META, :]
        qs = stack_heads(q_ref[sq, pl.ds(q0, nq), :])
        kind = jnp.where(local == 0, 0, jnp.where(local == nb - 1, 2, 1))
        s_ref[j] = lax.dot_general(qs, kx_ref[j], nt, preferred_element_type=F32) + bias_ref[kind]

    def softmax(j):
        s = s_ref[j]
        e = jnp.exp(s - jnp.max(s, axis=-1, keepdims=True))
        den_ref[j] = jnp.sum(e, axis=-1, keepdims=True)
        p_ref[j] = e.astype(BF16)

    def values(m, j):
        sq, _, q0, k0 = locate(m)
        vx_ref[j, 0:ATTN_KLOC, :] = v_ref[sq, pl.ds(k0, ATTN_KLOC), :]
        vx_ref[j, ATTN_KLOC:ATTN_KLOC + N_META, :] = v_ref[sq, 0:N_META, :]
        o = jnp.dot(p_ref[j], vx_ref[j], preferred_element_type=F32) / den_ref[j]
        o_ref[sq, pl.ds(q0, nq), :] = pick_heads(o, nq).astype(o_ref.dtype)

    scores(0, 0)
    scores(1, 1)
    softmax(0)

    def two_blocks(i, carry):
        m = 2 * i + 2
        scores(m, 0)
        softmax(1)
        values(m - 2, 0)
        scores(m + 1, 1)
        softmax(0)
        values(m - 1, 1)
        return carry

    lax.fori_loop(0, n_blocks // 2 - 1, two_blocks, 0)
    softmax(1)
    values(n_blocks - 2, 0)
    values(n_blocks - 1, 1)


def _attention(q, k, v, bias_tbl):
    _, batch, seq, _ = q.shape
    blocks_per_seq = (seq - N_META) // (GRID_W * ATTN_QROWS)
    n_seq = max(1, min(batch, ATTN_BLOCKS_PER_STEP // blocks_per_seq))
    assert batch % n_seq == 0
    blk = pl.BlockSpec((None, n_seq, seq, LANES), lambda p, b: (p, b, 0, 0))
    bias_spec = pl.BlockSpec((None,) + bias_tbl.shape[1:], lambda p, b: (p, 0, 0, 0))
    out = pl.pallas_call(
        _attn_kernel,
        grid=(N_PAIRS, batch // n_seq),
        in_specs=[blk, blk, blk, bias_spec],
        out_specs=blk,
        out_shape=jax.ShapeDtypeStruct(q.shape, BF16),
        scratch_shapes=[pltpu.VMEM((2, ATTN_KEXT, LANES), BF16), pltpu.VMEM((2, ATTN_KEXT, LANES), BF16),
                        pltpu.VMEM((2, 2 * ATTN_QROWS * GRID_W, ATTN_KEXT), F32),
                        pltpu.VMEM((2, 2 * ATTN_QROWS * GRID_W, ATTN_KEXT), BF16),
                        pltpu.VMEM((2, 2 * ATTN_QROWS * GRID_W, 1), F32)],
        compiler_params=_params(("parallel", "parallel")),
        name="attention",
    )(q, k, v, bias_tbl)
    return out


def _scan_time_block(seq):
    best = None
    for tt in range(LRU_TC, LRU_MAX_TT + 1, LRU_TC):
        if seq % tt == 0:
            best = tt
    assert best is not None, seq
    return best


def _lru_kernel(xl_ref, x_ref, xn_ref, cw_ref, cb_ref, w_ref, ba_ref, bx_ref, lam_ref, *rest, nt, reverse):
    acc_ref, o_ref, x_s, carry = rest if reverse else (None,) + rest
    step = pl.program_id(2)
    n_slabs, n_rows, _ = x_ref.shape
    blk = (nt - 1 - step) if reverse else step
    has_left = blk > 0
    has_right = blk < nt - 1
    left_rows = xl_ref.shape[1]
    chunk_rows = LRU_TC * SCAN_GROUP
    n_chunks = n_rows // chunk_rows

    @pl.when(step == 0)
    def _():
        carry[...] = jnp.zeros_like(carry)

    for s in range(n_slabs):
        sl = slice(s * LANES, (s + 1) * LANES)
        x_s[0:left_rows, sl] = jnp.where(has_left, xl_ref[s], jnp.zeros_like(xl_ref[s]))
        x_s[left_rows:left_rows + n_rows, sl] = x_ref[s]
        x_s[left_rows + n_rows:, sl] = jnp.where(has_right, xn_ref[s], jnp.zeros_like(xn_ref[s]))

    lam = lam_ref[...]
    softplus = jnp.maximum(-lam, 0.0) + jnp.log1p(jnp.exp(-jnp.abs(lam)))
    neg_c_sp = np.float32(-LRU_C) * softplus
    cw = cw_ref[...]
    cb = cb_ref[...]
    ba = ba_ref[...]
    bx = bx_ref[...]

    def chunk(i, h):
        c = (n_chunks - 1 - i) if reverse else i
        row0 = pl.multiple_of(c * chunk_rows, chunk_rows)
        xc = cb
        for j in range(CONV_W):
            xc = xc + x_s[pl.ds(row0 + j * SCAN_GROUP, chunk_rows), :] * cw[j:j + 1, :]
        pre = jnp.dot(xc.astype(BF16), w_ref[...], preferred_element_type=F32)
        r = jax.nn.sigmoid(pre[:, :LRU_CHUNK] + ba)
        gi = jax.nn.sigmoid(pre[:, LRU_CHUNK:] + bx)
        log_a = neg_c_sp * r
        a = jnp.exp(log_a)
        y = -jnp.tanh(log_a) * (a * a + 1.0)
        root = jnp.where(y > 0.0, y * lax.rsqrt(y), 0.0)
        u = root * (gi * xc)
        hs = [None] * LRU_TC
        order = range(LRU_TC - 1, -1, -1) if reverse else range(LRU_TC)
        for k in order:
            rows = slice(k * SCAN_GROUP, (k + 1) * SCAN_GROUP)
            h = a[rows] * h + u[rows]
            hs[k] = h
        hc = jnp.concatenate(hs, axis=0)
        for s in range(n_slabs):
            part = hc[:, s * LANES:(s + 1) * LANES]
            if reverse:
                part = part + acc_ref[s, pl.ds(row0, chunk_rows), :]
            o_ref[s, pl.ds(row0, chunk_rows), :] = part
        return h

    carry[...] = lax.fori_loop(0, n_chunks, chunk, carry[...], unroll=4)


def _lru(xr, conv_w, conv_b, w_gate, ba, bx, lam, seq, forward_out=None):
    reverse = forward_out is not None
    n_slabs, groups, _, _ = xr.shape
    c = n_slabs * LANES
    tt = _scan_time_block(seq)
    nt = seq // tt
    n_chunks = c // LRU_CHUNK
    slabs = LRU_CHUNK // LANES
    d = 1 if reverse else 0
    left_steps = CONV_LEFT
    right_steps = CONV_W - 1 - CONV_LEFT
    assert tt % left_steps == 0 and right_steps == 1

    def tblk(i):
        return (nt - 1 - i) if reverse else i

    cur = pl.BlockSpec((slabs, None, tt * SCAN_GROUP, LANES), lambda g, ch, i: (ch, g, tblk(i), 0))
    left = pl.BlockSpec((slabs, None, left_steps * SCAN_GROUP, LANES),
                        lambda g, ch, i: (ch, g, jnp.maximum(tblk(i) * (tt // left_steps) - 1, 0), 0))
    right = pl.BlockSpec((slabs, None, right_steps * SCAN_GROUP, LANES),
                         lambda g, ch, i: (ch, g, jnp.minimum((tblk(i) + 1) * tt, seq - 1), 0))
    per_ch = lambda rows: pl.BlockSpec((rows, LRU_CHUNK), lambda g, ch, i: (0, ch))
    per_dir = pl.BlockSpec((None, 1, LRU_CHUNK), lambda g, ch, i: (d, 0, ch))
    return pl.pallas_call(
        functools.partial(_lru_kernel, nt=nt, reverse=reverse),
        grid=(groups, n_chunks, nt),
        in_specs=[left, cur, right, per_ch(CONV_W), per_ch(1),
                  pl.BlockSpec((None, None, LRU_CHUNK, 2 * LRU_CHUNK), lambda g, ch, i: (d, ch, 0, 0)),
                  per_dir, per_dir, per_dir] + ([cur] if reverse else []),
        out_specs=cur,
        out_shape=jax.ShapeDtypeStruct(xr.shape, F32),
        input_output_aliases={9: 0} if reverse else {},
        scratch_shapes=[pltpu.VMEM(((tt + CONV_W - 1) * SCAN_GROUP, LRU_CHUNK), F32),
                        pltpu.VMEM((SCAN_GROUP, LRU_CHUNK), F32)],
        compiler_params=_params(("parallel", "parallel", "arbitrary")),
        name="lru_bwd" if reverse else "lru_fwd",
    )(xr, xr, xr, conv_w, conv_b.reshape(1, c), w_gate,
      ba.reshape(2, 1, c), bx.reshape(2, 1, c), lam.reshape(2, 1, c), *([forward_out] if reverse else []))


def _lru_gate_weights(wa, wx):
    per = LRU_CHUNK // LRU_BLOCK_DIM

    def dense(w):
        n_dir, n_blk, bd, _ = w.shape
        w = w.reshape(n_dir, n_blk // per, per, bd, bd)
        eye = jnp.eye(per, dtype=w.dtype)
        full = jnp.einsum('dcpij,pq->dcpiqj', w, eye)
        return full.reshape(n_dir, n_blk // per, LRU_CHUNK, LRU_CHUNK)

    return jnp.concatenate([dense(wa), dense(wx)], axis=-1).astype(BF16)


def _mixout_kernel(h_ref, na_ref, yr_ref, gna_ref, glru_ref, hr_ref, wn_ref, wl_ref, wo_ref, o_ref):
    nseq, tq, d = h_ref.shape
    rows = lambda ref: ref[...].reshape(nseq * tq, d)

    na = jnp.concatenate(
        [jnp.concatenate([na_ref[p, j] for j in range(nseq)], axis=0) for p in range(N_PAIRS)], axis=-1)
    na_p = jnp.dot(na, wn_ref[...], preferred_element_type=F32)
    hr = jnp.concatenate(
        [jnp.concatenate([hr_ref[s, pl.ds(j, tq, stride=nseq), :] for j in range(nseq)], axis=0)
         for s in range(d // LANES)], axis=-1)
    lru_in = (_gelu_tanh(rows(yr_ref).astype(F32)) * hr).astype(BF16)
    lru_p = jnp.dot(lru_in, wl_ref[...], preferred_element_type=F32)
    merged = (jax.nn.sigmoid(rows(gna_ref).astype(F32)) * na_p
              + jax.nn.sigmoid(rows(glru_ref).astype(F32)) * lru_p)
    out = rows(h_ref) + jnp.dot(merged.astype(BF16), wo_ref[...], preferred_element_type=F32)
    o_ref[...] = out.reshape(nseq, tq, d)


def _mixout(h, na, yr, gna, glru, hr, wn, wl, wo):
    n, d = h.shape
    _, batch, seq, _ = na.shape
    groups = batch // SCAN_GROUP
    tq = _seq_tile(seq)
    n_slabs = d // LANES
    tile = pl.BlockSpec((SCAN_GROUP, tq, d), lambda g_, i: (g_, i, 0))
    pair = pl.BlockSpec((N_PAIRS, SCAN_GROUP, tq, LANES), lambda g_, i: (0, g_, i, 0))
    slab = pl.BlockSpec((n_slabs, None, tq * SCAN_GROUP, LANES), lambda g_, i: (0, g_, i, 0))
    out = pl.pallas_call(
        _mixout_kernel,
        grid=(groups, seq // tq),
        in_specs=[tile, pair, tile, tile, tile, slab,
                  _const_spec((d, d)), _const_spec((d, d)), _const_spec((d, d))],
        out_specs=tile,
        out_shape=jax.ShapeDtypeStruct((batch, seq, d), F32),
        input_output_aliases={0: 0},
        compiler_params=_params(("parallel", "parallel")),
        name="mixout",
    )(h.reshape(batch, seq, d), na, yr, gna, glru, hr, wn, wl, wo)
    return out.reshape(n, d)


def _prepare_layer(l, p, rows):
    d = D_MODEL
    row = lambda a: a[l].reshape(1, d).astype(F32)
    return dict(
        norm_ffn1=row(p["norm_ffn1"]), norm_mix=row(p["norm_mix"]), norm_ffn2=row(p["norm_ffn2"]),
        ffn1=(p["ffn1_w_gate"][l].astype(BF16), p["ffn1_w_up"][l].astype(BF16), p["ffn1_w_down"][l].astype(BF16)),
        ffn2=(p["ffn2_w_gate"][l].astype(BF16), p["ffn2_w_up"][l].astype(BF16), p["ffn2_w_down"][l].astype(BF16)),
        w_in=p["w_in"][l].astype(BF16),
        bias_tbl=_attn_bias_table(p["na_rel_bias"][l], rows),
        conv_w=p["conv_w"][l].astype(F32), conv_b=p["conv_b"][l].astype(F32),
        w_gate=_lru_gate_weights(p["lru_wa"][l], p["lru_wx"][l]),
        ba=p["lru_ba"][l].astype(F32), bx=p["lru_bx"][l].astype(F32), lam=p["lru_lambda"][l].astype(F32),
        w_na_proj=p["w_na_proj"][l].astype(BF16), w_lru_proj=p["w_lru_proj"][l].astype(BF16),
        w_out=p["w_out"][l].astype(BF16),
    )


def _encode(x, meta_tokens, layers, final_norm):
    batch, t, d = x.shape
    seq = N_META + t
    assert batch % SCAN_GROUP == 0 and seq % N_META == 0
    groups = batch // SCAN_GROUP
    gf = final_norm.reshape(1, d).astype(F32)
    h = None
    for li, lp in enumerate(layers):
        if li == 0:
            h = _ffn_embed(x, meta_tokens, lp["norm_ffn1"], *lp["ffn1"])
        else:
            h = _ffn(h, lp["norm_ffn1"], *lp["ffn1"])
        q, k, v, xr, yr, gna, glru = _inproj(h, lp["norm_mix"], lp["w_in"], groups, seq)
        na = _attention(q, k, v, lp["bias_tbl"])
        lru_args = (xr, lp["conv_w"], lp["conv_b"], lp["w_gate"], lp["ba"], lp["bx"], lp["lam"], seq)
        hr = _lru(*lru_args, forward_out=_lru(*lru_args))
        h = _mixout(h, na, yr, gna, glru, hr, lp["w_na_proj"], lp["w_lru_proj"], lp["w_out"])
        if li < len(layers) - 1:
            h = _ffn(h, lp["norm_ffn2"], *lp["ffn2"])
    return _ffn_final(h, layers[-1]["norm_ffn2"], *layers[-1]["ffn2"], gf, batch, t)


def kernel(x_prompt, x_sample, meta_tokens, norm_ffn1, ffn1_w_gate, ffn1_w_up, ffn1_w_down, norm_mix, w_in, na_rel_bias, conv_w, conv_b, lru_wa, lru_ba, lru_wx, lru_bx, lru_lambda, w_na_proj, w_lru_proj, w_out, norm_ffn2, ffn2_w_gate, ffn2_w_up, ffn2_w_down, final_norm):
    p = dict(norm_ffn1=norm_ffn1, ffn1_w_gate=ffn1_w_gate, ffn1_w_up=ffn1_w_up, ffn1_w_down=ffn1_w_down,
             norm_mix=norm_mix, w_in=w_in, na_rel_bias=na_rel_bias, conv_w=conv_w, conv_b=conv_b,
             lru_wa=lru_wa, lru_ba=lru_ba, lru_wx=lru_wx, lru_bx=lru_bx, lru_lambda=lru_lambda,
             w_na_proj=w_na_proj, w_lru_proj=w_lru_proj, w_out=w_out, norm_ffn2=norm_ffn2,
             ffn2_w_gate=ffn2_w_gate, ffn2_w_up=ffn2_w_up, ffn2_w_down=ffn2_w_down)
    rows = x_prompt.shape[1] // GRID_W
    assert _attn_block_cases(rows) == _attn_block_cases(x_sample.shape[1] // GRID_W)
    layers = [_prepare_layer(l, p, rows) for l in range(norm_ffn1.shape[0])]
    y_prompt = _encode(x_prompt, meta_tokens, layers, final_norm)
    y_sample = _encode(x_sample, meta_tokens, layers, final_norm)
    return (y_prompt, y_sample)
```

```python
import functools

import numpy as np
import jax
import jax.numpy as jnp
from jax import lax
from jax.experimental import pallas as pl
from jax.experimental.pallas import tpu as pltpu

F32 = jnp.float32
BF16 = jnp.bfloat16

D_MODEL = 1024
N_META = 16
GRID_W = 64
NA_HEADS = 16
NA_HEAD_DIM = 64
NA_KH = 8
NA_KW = 16
LRU_BLOCK_DIM = 64
CONV_W = 4
CONV_LEFT = 2
LRU_C = 8.0
RMS_EPS = 1e-6
MASK_VALUE = -1e30

LANES = 128
SUBLANES = 8
BF16_ROWS = 16
MXU_TILE = 256
VMEM_LIMIT = 56 * 1024 * 1024
MAX_ROW_TILE = 704
N_PAIRS = NA_HEADS // 2
ATTN_QROWS = 4
ATTN_KROWS = ATTN_QROWS + NA_KH - 1
ATTN_KLOC = ATTN_KROWS * GRID_W
ATTN_KEXT = -(-(ATTN_KLOC + N_META) // MXU_TILE) * MXU_TILE
ATTN_BLOCKS_PER_STEP = 64
LRU_CHUNK = MXU_TILE
SCAN_GROUP = SUBLANES
LRU_TC = 16
LRU_MAX_TT = 704


def _row_tile(n):
    for t in range(MAX_ROW_TILE, 0, -BF16_ROWS):
        if n % t == 0:
            return t
    raise ValueError(f"no row tile for {n} rows")


def _const_spec(shape):
    zeros = (0,) * len(shape)
    return pl.BlockSpec(shape, lambda *_: zeros, pipeline_mode=pl.Buffered(1))


def _params(sem):
    return pltpu.CompilerParams(dimension_semantics=sem, vmem_limit_bytes=VMEM_LIMIT)


def _rms(x, g):
    inv = lax.rsqrt(jnp.mean(x * x, axis=-1, keepdims=True) + RMS_EPS)
    return (x * inv) * g


def _gelu_tanh(x):
    c = np.float32(np.sqrt(2.0 / np.pi))
    return x * (0.5 * (1.0 + jnp.tanh(c * (x + np.float32(0.044715) * (x * x * x)))))


def _ffn_kernel(h_ref, g_ref, wg_ref, wu_ref, wd_ref, gf_ref, o_ref, *, final):
    x = h_ref[...]
    xn = _rms(x, g_ref[...]).astype(BF16)
    gate = jnp.dot(xn, wg_ref[...], preferred_element_type=F32)
    up = jnp.dot(xn, wu_ref[...], preferred_element_type=F32)
    act = (gate * jax.nn.sigmoid(gate) * up).astype(BF16)
    down = jnp.dot(act, wd_ref[...], preferred_element_type=F32)
    y = x + 0.5 * down
    o_ref[...] = _rms(y, gf_ref[...]) if final else y


def _ffn_specs(d, dff):
    return [_const_spec((1, d)), _const_spec((d, dff)), _const_spec((d, dff)), _const_spec((dff, d)),
            _const_spec((1, d))]


def _ffn(h, g, wg, wu, wd):
    n, d = h.shape
    tm = _row_tile(n)
    row = pl.BlockSpec((tm, d), lambda i: (i, 0))
    return pl.pallas_call(
        functools.partial(_ffn_kernel, final=False),
        grid=(n // tm,),
        in_specs=[row] + _ffn_specs(d, wg.shape[1]),
        out_specs=row,
        out_shape=jax.ShapeDtypeStruct((n, d), F32),
        input_output_aliases={0: 0},
        compiler_params=_params(("parallel",)),
        name="ffn",
    )(h, g, wg, wu, wd, g)


def _grid_rows_spec(t, seq, tm, d):
    assert seq % N_META == 0 and tm % N_META == 0
    return pl.BlockSpec((pl.Element(tm), pl.Element(d)),
                        lambda b, j: (pl.multiple_of(b * seq + N_META + j * tm, N_META), 0))


def _ffn_embed_kernel(x_ref, m_ref, g_ref, wg_ref, wu_ref, wd_ref, gf_ref, o_ref, xin_ref):
    tm = x_ref.shape[0]

    @pl.when(pl.program_id(1) == 0)
    def _():
        xin_ref[0:N_META, :] = m_ref[...]
        xin_ref[N_META:, :] = x_ref[0:tm - N_META, :]
        _ffn_kernel(xin_ref, g_ref, wg_ref, wu_ref, wd_ref, gf_ref, o_ref, final=False)

    @pl.when(pl.program_id(1) > 0)
    def _():
        _ffn_kernel(x_ref, g_ref, wg_ref, wu_ref, wd_ref, gf_ref, o_ref, final=False)


def _ffn_embed(x, meta_tokens, g, wg, wu, wd):
    batch, t, d = x.shape
    seq = N_META + t
    tm = _row_tile(seq)
    x_tile = pl.BlockSpec(
        (pl.Element(tm), pl.Element(d)),
        lambda b, j: (pl.multiple_of(b * t + jnp.maximum(j * tm - N_META, 0), N_META), 0))
    return pl.pallas_call(
        _ffn_embed_kernel,
        grid=(batch, seq // tm),
        in_specs=[x_tile, _const_spec((N_META, d))] + _ffn_specs(d, wg.shape[1]),
        out_specs=pl.BlockSpec((tm, d), lambda b, j: (b * (seq // tm) + j, 0)),
        out_shape=jax.ShapeDtypeStruct((batch * seq, d), F32),
        scratch_shapes=[pltpu.VMEM((tm, d), F32)],
        compiler_params=_params(("parallel", "parallel")),
        name="ffn_embed",
    )(x.reshape(batch * t, d), meta_tokens.astype(F32), g, wg, wu, wd, g)


def _ffn_final(h, g, wg, wu, wd, final_g, batch, t):
    n, d = h.shape
    seq = n // batch
    tm = _row_tile(t)
    out = pl.pallas_call(
        functools.partial(_ffn_kernel, final=True),
        grid=(batch, t // tm),
        in_specs=[_grid_rows_spec(t, seq, tm, d)] + _ffn_specs(d, wg.shape[1]),
        out_specs=pl.BlockSpec((tm, d), lambda b, j: (b * (t // tm) + j, 0)),
        out_shape=jax.ShapeDtypeStruct((batch * t, d), F32),
        compiler_params=_params(("parallel", "parallel")),
        name="ffn_final",
    )(h, g, wg, wu, wd, final_g)
    return out.reshape(batch, t, d)


def _seq_tile(seq):
    best = None
    for tq in range(BF16_ROWS, MAX_ROW_TILE // SCAN_GROUP + 1, BF16_ROWS):
        if seq % tq == 0:
            best = tq
    assert best is not None, seq
    return best


def _inproj_kernel(h_ref, g_ref, w_ref, q_ref, k_ref, v_ref, xr_ref, yr_ref, gna_ref, glru_ref):
    nseq, tq, d = h_ref.shape
    xn = _rms(h_ref[...].reshape(nseq * tq, d), g_ref[...]).astype(BF16)

    def proj(j):
        return jnp.dot(xn, w_ref[:, j * d:(j + 1) * d], preferred_element_type=F32)

    zq = (proj(0) * np.float32(NA_HEAD_DIM ** -0.5)).astype(BF16)
    zk = proj(1).astype(BF16)
    zv = proj(2).astype(BF16)
    for p in range(N_PAIRS):
        sl = slice(p * LANES, (p + 1) * LANES)
        for j in range(nseq):
            rows = slice(j * tq, (j + 1) * tq)
            q_ref[p, j] = zq[rows, sl]
            k_ref[p, j] = zk[rows, sl]
            v_ref[p, j] = zv[rows, sl]
    xr = proj(3)
    for s in range(d // LANES):
        for j in range(nseq):
            xr_ref[s, pl.ds(j, tq, stride=nseq), :] = xr[j * tq:(j + 1) * tq, s * LANES:(s + 1) * LANES]
    yr_ref[...] = proj(4).reshape(nseq, tq, d).astype(BF16)
    gna_ref[...] = proj(5).reshape(nseq, tq, d).astype(BF16)
    glru_ref[...] = proj(6).reshape(nseq, tq, d).astype(BF16)


def _inproj(h, g, w_in, groups, seq):
    n, d = h.shape
    batch = groups * SCAN_GROUP
    tq = _seq_tile(seq)
    n_slabs = d // LANES
    tile = pl.BlockSpec((SCAN_GROUP, tq, d), lambda g_, i: (g_, i, 0))
    pair = pl.BlockSpec((N_PAIRS, SCAN_GROUP, tq, LANES), lambda g_, i: (0, g_, i, 0))
    slab = pl.BlockSpec((n_slabs, None, tq * SCAN_GROUP, LANES), lambda g_, i: (0, g_, i, 0))
    pair_shape = jax.ShapeDtypeStruct((N_PAIRS, batch, seq, LANES), BF16)
    tile_shape = jax.ShapeDtypeStruct((batch, seq, d), BF16)
    slab_shape = jax.ShapeDtypeStruct((n_slabs, groups, seq * SCAN_GROUP, LANES), F32)
    return pl.pallas_call(
        _inproj_kernel,
        grid=(groups, seq // tq),
        in_specs=[tile, _const_spec((1, d)), _const_spec(w_in.shape)],
        out_specs=[pair, pair, pair, slab, tile, tile, tile],
        out_shape=[pair_shape, pair_shape, pair_shape, slab_shape, tile_shape, tile_shape, tile_shape],
        compiler_params=_params(("parallel", "parallel")),
        name="inproj",
    )(h.reshape(batch, seq, d), g, w_in)


def _attn_block_cases(rows):
    nb = rows // ATTN_QROWS
    assert rows % ATTN_QROWS == 0 and nb >= 3 and nb % 2 == 0, rows

    def case(m):
        ks = min(max(ATTN_QROWS * m - NA_KH // 2, 0), rows - ATTN_KROWS)
        out = []
        for q in range(ATTN_QROWS):
            r = ATTN_QROWS * m + q
            rs = min(max(r - NA_KH // 2, 0), rows - NA_KH)
            out.append((rs - ks, r - rs))
        return tuple(out)

    assert all(case(m) == case(1) for m in range(1, nb - 1))
    return case(0), case(1), case(nb - 1)


def _attn_bias_table(rel_bias, rows):
    c = np.arange(GRID_W)
    cs = np.clip(c - NA_KW // 2, 0, GRID_W - NA_KW)
    kc = np.arange(GRID_W)
    ok = (kc[None, :] >= cs[:, None]) & (kc[None, :] < cs[:, None] + NA_KW)
    dcol = kc[None, :] - c[:, None] + NA_KW - 1
    onehot = (dcol[:, :, None] == np.arange(2 * NA_KW - 1)).astype(np.float32)
    cols = jnp.einsum('hdj,ckj->hdck', rel_bias.astype(F32), onehot, precision=lax.Precision.HIGHEST)
    cols = jnp.where(ok[None, None], cols, np.float32(MASK_VALUE))
    masked = jnp.full((NA_HEADS, GRID_W, GRID_W), np.float32(MASK_VALUE))
    tail = jnp.concatenate([jnp.zeros((NA_HEADS, GRID_W, N_META), F32),
                            jnp.full((NA_HEADS, GRID_W, ATTN_KEXT - ATTN_KLOC - N_META), np.float32(MASK_VALUE))],
                           axis=-1)
    per_case = []
    for cases in _attn_block_cases(rows):
        for off, e in cases:
            pieces = [cols[:, i - off + NA_KH - 1 - e] if 0 <= i - off < NA_KH else masked
                      for i in range(ATTN_KROWS)]
            per_case.append(jnp.concatenate(pieces + [tail], axis=-1))
    tbl = jnp.stack(per_case, axis=1)
    tbl = tbl.reshape(N_PAIRS, 2, 3, ATTN_QROWS, GRID_W, ATTN_KEXT)
    return jnp.transpose(tbl, (0, 2, 1, 3, 4, 5)).reshape(N_PAIRS, 3, 2 * ATTN_QROWS * GRID_W, ATTN_KEXT)


def _attn_kernel(q_ref, k_ref, v_ref, bias_ref, o_ref, kx_ref, vx_ref, s_ref, p_ref, den_ref):
    n_seq, seq, _ = q_ref.shape
    rows = (seq - N_META) // GRID_W
    nb = rows // ATTN_QROWS
    n_blocks = n_seq * nb
    nq = ATTN_QROWS * GRID_W
    nt = (((1,), (1,)), ((), ()))
    lo = lax.broadcasted_iota(jnp.int32, (1, LANES), 1) < NA_HEAD_DIM

    def stack_heads(x):
        zero = jnp.zeros_like(x)
        return jnp.concatenate([jnp.where(lo, x, zero), jnp.where(lo, zero, x)], axis=0)

    def pick_heads(o, n):
        return jnp.where(lo, o[:n], o[n:])

    for sq in range(n_seq):
        km = k_ref[sq, 0:N_META, :]
        vm = v_ref[sq, 0:N_META, :]
        qm = stack_heads(q_ref[sq, 0:N_META, :])
        sm = lax.dot_general(qm, km, nt, preferred_element_type=F32)
        em = jnp.exp(sm - jnp.max(sm, axis=-1, keepdims=True))
        om = jnp.dot(em.astype(BF16), vm, preferred_element_type=F32)
        om = om / jnp.sum(em, axis=-1, keepdims=True)
        o_ref[sq, 0:N_META, :] = pick_heads(om, N_META).astype(o_ref.dtype)

    pad = jnp.zeros((ATTN_KEXT - ATTN_KLOC - N_META, LANES), BF16)
    for j in range(2):
        kx_ref[j, ATTN_KLOC + N_META:, :] = pad
        vx_ref[j, ATTN_KLOC + N_META:, :] = pad

    def locate(m):
        sq = m // nb
        local = m - sq * nb
        ks = jnp.clip(ATTN_QROWS * local - NA_KH // 2, 0, rows - ATTN_KROWS)
        q0 = pl.multiple_of(N_META + local * nq, BF16_ROWS)
        k0 = pl.multiple_of(N_META + ks * GRID_W, BF16_ROWS)
        return sq, local, q0, k0

    def scores(m, j):
        sq, local, q0, k0 = locate(m)
        kx_ref[j, 0:ATTN_KLOC, :] = k_ref[sq, pl.ds(k0, ATTN_KLOC), :]
        kx_ref[j, ATTN_KLOC:ATTN_KLOC + N_META, :] = k_ref[sq, 0:N_META, :]
        qs = stack_heads(q_ref[sq, pl.ds(q0, nq), :])
        kind = jnp.where(local == 0, 0, jnp.where(local == nb - 1, 2, 1))
        s_ref[j] = lax.dot_general(qs, kx_ref[j], nt, preferred_element_type=F32) + bias_ref[kind]

    def softmax(j):
        s = s_ref[j]
        e = jnp.exp(s - jnp.max(s, axis=-1, keepdims=True))
        den_ref[j] = jnp.sum(e, axis=-1, keepdims=True)
        p_ref[j] = e.astype(BF16)

    def values(m, j):
        sq, _, q0, k0 = locate(m)
        vx_ref[j, 0:ATTN_KLOC, :] = v_ref[sq, pl.ds(k0, ATTN_KLOC), :]
        vx_ref[j, ATTN_KLOC:ATTN_KLOC + N_META, :] = v_ref[sq, 0:N_META, :]
        o = jnp.dot(p_ref[j], vx_ref[j], preferred_element_type=F32) / den_ref[j]
        o_ref[sq, pl.ds(q0, nq), :] = pick_heads(o, nq).astype(o_ref.dtype)

    scores(0, 0)
    scores(1, 1)
    softmax(0)

    def two_blocks(i, carry):
        m = 2 * i + 2
        scores(m, 0)
        softmax(1)
        values(m - 2, 0)
        scores(m + 1, 1)
        softmax(0)
        values(m - 1, 1)
        return carry

    lax.fori_loop(0, n_blocks // 2 - 1, two_blocks, 0)
    softmax(1)
    values(n_blocks - 2, 0)
    values(n_blocks - 1, 1)


def _attention(q, k, v, bias_tbl):
    _, batch, seq, _ = q.shape
    blocks_per_seq = (seq - N_META) // (GRID_W * ATTN_QROWS)
    n_seq = max(1, min(batch, ATTN_BLOCKS_PER_STEP // blocks_per_seq))
    assert batch % n_seq == 0
    blk = pl.BlockSpec((None, n_seq, seq, LANES), lambda p, b: (p, b, 0, 0))
    bias_spec = pl.BlockSpec((None,) + bias_tbl.shape[1:], lambda p, b: (p, 0, 0, 0))
    out = pl.pallas_call(
        _attn_kernel,
        grid=(N_PAIRS, batch // n_seq),
        in_specs=[blk, blk, blk, bias_spec],
        out_specs=blk,
        out_shape=jax.ShapeDtypeStruct(q.shape, BF16),
        scratch_shapes=[pltpu.VMEM((2, ATTN_KEXT, LANES), BF16), pltpu.VMEM((2, ATTN_KEXT, LANES), BF16),
                        pltpu.VMEM((2, 2 * ATTN_QROWS * GRID_W, ATTN_KEXT), F32),
                        pltpu.VMEM((2, 2 * ATTN_QROWS * GRID_W, ATTN_KEXT), BF16),
                        pltpu.VMEM((2, 2 * ATTN_QROWS * GRID_W, 1), F32)],
        compiler_params=_params(("parallel", "parallel")),
        name="attention",
    )(q, k, v, bias_tbl)
    return out


def _scan_time_block(seq):
    best = None
    for tt in range(LRU_TC, LRU_MAX_TT + 1, LRU_TC):
        if seq % tt == 0:
            best = tt
    assert best is not None, seq
    return best


def _lru_kernel(xl_ref, x_ref, xn_ref, cw_ref, cb_ref, w_ref, ba_ref, bx_ref, lam_ref, *rest, nt, reverse):
    acc_ref, o_ref, x_s, carry = rest if reverse else (None,) + rest
    step = pl.program_id(2)
    n_slabs, n_rows, _ = x_ref.shape
    blk = (nt - 1 - step) if reverse else step
    has_left = blk > 0
    has_right = blk < nt - 1
    left_rows = xl_ref.shape[1]
    chunk_rows = LRU_TC * SCAN_GROUP
    n_chunks = n_rows // chunk_rows

    @pl.when(step == 0)
    def _():
        carry[...] = jnp.zeros_like(carry)

    for s in range(n_slabs):
        sl = slice(s * LANES, (s + 1) * LANES)
        x_s[0:left_rows, sl] = jnp.where(has_left, xl_ref[s], jnp.zeros_like(xl_ref[s]))
        x_s[left_rows:left_rows + n_rows, sl] = x_ref[s]
        x_s[left_rows + n_rows:, sl] = jnp.where(has_right, xn_ref[s], jnp.zeros_like(xn_ref[s]))

    lam = lam_ref[...]
    softplus = jnp.maximum(-lam, 0.0) + jnp.log1p(jnp.exp(-jnp.abs(lam)))
    neg_c_sp = np.float32(-LRU_C) * softplus
    cw = cw_ref[...]
    cb = cb_ref[...]
    ba = ba_ref[...]
    bx = bx_ref[...]

    def chunk(i, h):
        c = (n_chunks - 1 - i) if reverse else i
        row0 = pl.multiple_of(c * chunk_rows, chunk_rows)
        xc = cb
        for j in range(CONV_W):
            xc = xc + x_s[pl.ds(row0 + j * SCAN_GROUP, chunk_rows), :] * cw[j:j + 1, :]
        pre = jnp.dot(xc.astype(BF16), w_ref[...], preferred_element_type=F32)
        r = jax.nn.sigmoid(pre[:, :LRU_CHUNK] + ba)
        gi = jax.nn.sigmoid(pre[:, LRU_CHUNK:] + bx)
        log_a = neg_c_sp * r
        a = jnp.exp(log_a)
        y = -jnp.tanh(log_a) * (a * a + 1.0)
        root = jnp.where(y > 0.0, y * lax.rsqrt(y), 0.0)
        u = root * (gi * xc)
        hs = [None] * LRU_TC
        order = range(LRU_TC - 1, -1, -1) if reverse else range(LRU_TC)
        for k in order:
            rows = slice(k * SCAN_GROUP, (k + 1) * SCAN_GROUP)
            h = a[rows] * h + u[rows]
            hs[k] = h
        hc = jnp.concatenate(hs, axis=0)
        for s in range(n_slabs):
            part = hc[:, s * LANES:(s + 1) * LANES]
            if reverse:
                part = part + acc_ref[s, pl.ds(row0, chunk_rows), :]
            o_ref[s, pl.ds(row0, chunk_rows), :] = part
        return h

    carry[...] = lax.fori_loop(0, n_chunks, chunk, carry[...], unroll=4)


def _lru(xr, conv_w, conv_b, w_gate, ba, bx, lam, seq, forward_out=None):
    reverse = forward_out is not None
    n_slabs, groups, _, _ = xr.shape
    c = n_slabs * LANES
    tt = _scan_time_block(seq)
    nt = seq // tt
    n_chunks = c // LRU_CHUNK
    slabs = LRU_CHUNK // LANES
    d = 1 if reverse else 0
    left_steps = CONV_LEFT
    right_steps = CONV_W - 1 - CONV_LEFT
    assert tt % left_steps == 0 and right_steps == 1

    def tblk(i):
        return (nt - 1 - i) if reverse else i

    cur = pl.BlockSpec((slabs, None, tt * SCAN_GROUP, LANES), lambda g, ch, i: (ch, g, tblk(i), 0))
    left = pl.BlockSpec((slabs, None, left_steps * SCAN_GROUP, LANES),
                        lambda g, ch, i: (ch, g, jnp.maximum(tblk(i) * (tt // left_steps) - 1, 0), 0))
    right = pl.BlockSpec((slabs, None, right_steps * SCAN_GROUP, LANES),
                         lambda g, ch, i: (ch, g, jnp.minimum((tblk(i) + 1) * tt, seq - 1), 0))
    per_ch = lambda rows: pl.BlockSpec((rows, LRU_CHUNK), lambda g, ch, i: (0, ch))
    per_dir = pl.BlockSpec((None, 1, LRU_CHUNK), lambda g, ch, i: (d, 0, ch))
    return pl.pallas_call(
        functools.partial(_lru_kernel, nt=nt, reverse=reverse),
        grid=(groups, n_chunks, nt),
        in_specs=[left, cur, right, per_ch(CONV_W), per_ch(1),
                  pl.BlockSpec((None, None, LRU_CHUNK, 2 * LRU_CHUNK), lambda g, ch, i: (d, ch, 0, 0)),
                  per_dir, per_dir, per_dir] + ([cur] if reverse else []),
        out_specs=cur,
        out_shape=jax.ShapeDtypeStruct(xr.shape, F32),
        input_output_aliases={9: 0} if reverse else {},
        scratch_shapes=[pltpu.VMEM(((tt + CONV_W - 1) * SCAN_GROUP, LRU_CHUNK), F32),
                        pltpu.VMEM((SCAN_GROUP, LRU_CHUNK), F32)],
        compiler_params=_params(("parallel", "parallel", "arbitrary")),
        name="lru_bwd" if reverse else "lru_fwd",
    )(xr, xr, xr, conv_w, conv_b.reshape(1, c), w_gate,
      ba.reshape(2, 1, c), bx.reshape(2, 1, c), lam.reshape(2, 1, c), *([forward_out] if reverse else []))


def _lru_gate_weights(wa, wx):
    per = LRU_CHUNK // LRU_BLOCK_DIM

    def dense(w):
        n_dir, n_blk, bd, _ = w.shape
        w = w.reshape(n_dir, n_blk // per, per, bd, bd)
        eye = jnp.eye(per, dtype=w.dtype)
        full = jnp.einsum('dcpij,pq->dcpiqj', w, eye)
        return full.reshape(n_dir, n_blk // per, LRU_CHUNK, LRU_CHUNK)

    return jnp.concatenate([dense(wa), dense(wx)], axis=-1).astype(BF16)


def _mixout_kernel(h_ref, na_ref, yr_ref, gna_ref, glru_ref, hr_ref, wn_ref, wl_ref, wo_ref, o_ref):
    nseq, tq, d = h_ref.shape
    rows = lambda ref: ref[...].reshape(nseq * tq, d)

    na = jnp.concatenate(
        [jnp.concatenate([na_ref[p, j] for j in range(nseq)], axis=0) for p in range(N_PAIRS)], axis=-1)
    na_p = jnp.dot(na, wn_ref[...], preferred_element_type=F32)
    hr = jnp.concatenate(
        [jnp.concatenate([hr_ref[s, pl.ds(j, tq, stride=nseq), :] for j in range(nseq)], axis=0)
         for s in range(d // LANES)], axis=-1)
    lru_in = (_gelu_tanh(rows(yr_ref).astype(F32)) * hr).astype(BF16)
    lru_p = jnp.dot(lru_in, wl_ref[...], preferred_element_type=F32)
    merged = (jax.nn.sigmoid(rows(gna_ref).astype(F32)) * na_p
              + jax.nn.sigmoid(rows(glru_ref).astype(F32)) * lru_p)
    out = rows(h_ref) + jnp.dot(merged.astype(BF16), wo_ref[...], preferred_element_type=F32)
    o_ref[...] = out.reshape(nseq, tq, d)


def _mixout(h, na, yr, gna, glru, hr, wn, wl, wo):
    n, d = h.shape
    _, batch, seq, _ = na.shape
    groups = batch // SCAN_GROUP
    tq = _seq_tile(seq)
    n_slabs = d // LANES
    tile = pl.BlockSpec((SCAN_GROUP, tq, d), lambda g_, i: (g_, i, 0))
    pair = pl.BlockSpec((N_PAIRS, SCAN_GROUP, tq, LANES), lambda g_, i: (0, g_, i, 0))
    slab = pl.BlockSpec((n_slabs, None, tq * SCAN_GROUP, LANES), lambda g_, i: (0, g_, i, 0))
    out = pl.pallas_call(
        _mixout_kernel,
        grid=(groups, seq // tq),
        in_specs=[tile, pair, tile, tile, tile, slab,
                  _const_spec((d, d)), _const_spec((d, d)), _const_spec((d, d))],
        out_specs=tile,
        out_shape=jax.ShapeDtypeStruct((batch, seq, d), F32),
        input_output_aliases={0: 0},
        compiler_params=_params(("parallel", "parallel")),
        name="mixout",
    )(h.reshape(batch, seq, d), na, yr, gna, glru, hr, wn, wl, wo)
    return out.reshape(n, d)


def _prepare_layer(l, p, rows):
    d = D_MODEL
    row = lambda a: a[l].reshape(1, d).astype(F32)
    return dict(
        norm_ffn1=row(p["norm_ffn1"]), norm_mix=row(p["norm_mix"]), norm_ffn2=row(p["norm_ffn2"]),
        ffn1=(p["ffn1_w_gate"][l].astype(BF16), p["ffn1_w_up"][l].astype(BF16), p["ffn1_w_down"][l].astype(BF16)),
        ffn2=(p["ffn2_w_gate"][l].astype(BF16), p["ffn2_w_up"][l].astype(BF16), p["ffn2_w_down"][l].astype(BF16)),
        w_in=p["w_in"][l].astype(BF16),
        bias_tbl=_attn_bias_table(p["na_rel_bias"][l], rows),
        conv_w=p["conv_w"][l].astype(F32), conv_b=p["conv_b"][l].astype(F32),
        w_gate=_lru_gate_weights(p["lru_wa"][l], p["lru_wx"][l]),
        ba=p["lru_ba"][l].astype(F32), bx=p["lru_bx"][l].astype(F32), lam=p["lru_lambda"][l].astype(F32),
        w_na_proj=p["w_na_proj"][l].astype(BF16), w_lru_proj=p["w_lru_proj"][l].astype(BF16),
        w_out=p["w_out"][l].astype(BF16),
    )


def _encode(x, meta_tokens, layers, final_norm):
    batch, t, d = x.shape
    seq = N_META + t
    assert batch % SCAN_GROUP == 0 and seq % N_META == 0
    groups = batch // SCAN_GROUP
    gf = final_norm.reshape(1, d).astype(F32)
    h = None
    for li, lp in enumerate(layers):
        if li == 0:
            h = _ffn_embed(x, meta_tokens, lp["norm_ffn1"], *lp["ffn1"])
        else:
            h = _ffn(h, lp["norm_ffn1"], *lp["ffn1"])
        q, k, v, xr, yr, gna, glru = _inproj(h, lp["norm_mix"], lp["w_in"], groups, seq)
        na = _attention(q, k, v, lp["bias_tbl"])
        lru_args = (xr, lp["conv_w"], lp["conv_b"], lp["w_gate"], lp["ba"], lp["bx"], lp["lam"], seq)
        hr = _lru(*lru_args, forward_out=_lru(*lru_args))
        h = _mixout(h, na, yr, gna, glru, hr, lp["w_na_proj"], lp["w_lru_proj"], lp["w_out"])
        if li < len(layers) - 1:
            h = _ffn(h, lp["norm_ffn2"], *lp["ffn2"])
    return _ffn_final(h, layers[-1]["norm_ffn2"], *layers[-1]["ffn2"], gf, batch, t)


def kernel(x_prompt, x_sample, meta_tokens, norm_ffn1, ffn1_w_gate, ffn1_w_up, ffn1_w_down, norm_mix, w_in, na_rel_bias, conv_w, conv_b, lru_wa, lru_ba, lru_wx, lru_bx, lru_lambda, w_na_proj, w_lru_proj, w_out, norm_ffn2, ffn2_w_gate, ffn2_w_up, ffn2_w_down, final_norm):
    p = dict(norm_ffn1=norm_ffn1, ffn1_w_gate=ffn1_w_gate, ffn1_w_up=ffn1_w_up, ffn1_w_down=ffn1_w_down,
             norm_mix=norm_mix, w_in=w_in, na_rel_bias=na_rel_bias, conv_w=conv_w, conv_b=conv_b,
             lru_wa=lru_wa, lru_ba=lru_ba, lru_wx=lru_wx, lru_bx=lru_bx, lru_lambda=lru_lambda,
             w_na_proj=w_na_proj, w_lru_proj=w_lru_proj, w_out=w_out, norm_ffn2=norm_ffn2,
             ffn2_w_gate=ffn2_w_gate, ffn2_w_up=ffn2_w_up, ffn2_w_down=ffn2_w_down)
    rows = x_prompt.shape[1] // GRID_W
    assert _attn_block_cases(rows) == _attn_block_cases(x_sample.shape[1] // GRID_W)
    layers = [_prepare_layer(l, p, rows) for l in range(norm_ffn1.shape[0])]
    y_prompt = _encode(x_prompt, meta_tokens, layers, final_norm)
    y_sample = _encode(x_sample, meta_tokens, layers, final_norm)
    return (y_prompt, y_sample)
```

```python
import functools

import numpy as np
import jax
import jax.numpy as jnp
from jax import lax
from jax.experimental import pallas as pl
from jax.experimental.pallas import tpu as pltpu

F32 = jnp.float32
BF16 = jnp.bfloat16

D_MODEL = 1024
N_META = 16
GRID_W = 64
NA_HEADS = 16
NA_HEAD_DIM = 64
NA_KH = 8
NA_KW = 16
LRU_BLOCK_DIM = 64
CONV_W = 4
CONV_LEFT = 2
LRU_C = 8.0
RMS_EPS = 1e-6
MASK_VALUE = -1e30

LANES = 128
SUBLANES = 8
BF16_ROWS = 16
MXU_TILE = 256
VMEM_LIMIT = 56 * 1024 * 1024
MAX_ROW_TILE = 704
N_PAIRS = NA_HEADS // 2
ATTN_QROWS = 4
ATTN_KROWS = ATTN_QROWS + NA_KH - 1
ATTN_KLOC = ATTN_KROWS * GRID_W
ATTN_KEXT = -(-(ATTN_KLOC + N_META) // MXU_TILE) * MXU_TILE
ATTN_BLOCKS_PER_STEP = 64
LRU_CHUNK = MXU_TILE
SCAN_GROUP = SUBLANES
LRU_TC = 16
LRU_MAX_TT = 704


def _row_tile(n):
    for t in range(MAX_ROW_TILE, 0, -BF16_ROWS):
        if n % t == 0:
            return t
    raise ValueError(f"no row tile for {n} rows")


def _const_spec(shape):
    zeros = (0,) * len(shape)
    return pl.BlockSpec(shape, lambda *_: zeros, pipeline_mode=pl.Buffered(1))


def _params(sem):
    return pltpu.CompilerParams(dimension_semantics=sem, vmem_limit_bytes=VMEM_LIMIT)


def _rms(x, g):
    inv = lax.rsqrt(jnp.mean(x * x, axis=-1, keepdims=True) + RMS_EPS)
    return (x * inv) * g


def _gelu_tanh(x):
    c = np.float32(np.sqrt(2.0 / np.pi))
    return x * (0.5 * (1.0 + jnp.tanh(c * (x + np.float32(0.044715) * (x * x * x)))))


def _ffn_kernel(h_ref, g_ref, wg_ref, wu_ref, wd_ref, gf_ref, o_ref, *, final):
    x = h_ref[...]
    xn = _rms(x, g_ref[...]).astype(BF16)
    gate = jnp.dot(xn, wg_ref[...], preferred_element_type=F32)
    up = jnp.dot(xn, wu_ref[...], preferred_element_type=F32)
    act = (gate * jax.nn.sigmoid(gate) * up).astype(BF16)
    down = jnp.dot(act, wd_ref[...], preferred_element_type=F32)
    y = x + 0.5 * down
    o_ref[...] = _rms(y, gf_ref[...]) if final else y


def _ffn_specs(d, dff):
    return [_const_spec((1, d)), _const_spec((d, dff)), _const_spec((d, dff)), _const_spec((dff, d)),
            _const_spec((1, d))]


def _ffn(h, g, wg, wu, wd):
    n, d = h.shape
    tm = _row_tile(n)
    row = pl.BlockSpec((tm, d), lambda i: (i, 0))
    return pl.pallas_call(
        functools.partial(_ffn_kernel, final=False),
        grid=(n // tm,),
        in_specs=[row] + _ffn_specs(d, wg.shape[1]),
        out_specs=row,
        out_shape=jax.ShapeDtypeStruct((n, d), F32),
        input_output_aliases={0: 0},
        compiler_params=_params(("parallel",)),
        name="ffn",
    )(h, g, wg, wu, wd, g)


def _grid_rows_spec(t, seq, tm, d):
    assert seq % N_META == 0 and tm % N_META == 0
    return pl.BlockSpec((pl.Element(tm), pl.Element(d)),
                        lambda b, j: (pl.multiple_of(b * seq + N_META + j * tm, N_META), 0))


def _ffn_embed_kernel(x_ref, m_ref, g_ref, wg_ref, wu_ref, wd_ref, gf_ref, o_ref, xin_ref):
    tm = x_ref.shape[0]

    @pl.when(pl.program_id(1) == 0)
    def _():
        xin_ref[0:N_META, :] = m_ref[...]
        xin_ref[N_META:, :] = x_ref[0:tm - N_META, :]
        _ffn_kernel(xin_ref, g_ref, wg_ref, wu_ref, wd_ref, gf_ref, o_ref, final=False)

    @pl.when(pl.program_id(1) > 0)
    def _():
        _ffn_kernel(x_ref, g_ref, wg_ref, wu_ref, wd_ref, gf_ref, o_ref, final=False)


def _ffn_embed(x, meta_tokens, g, wg, wu, wd):
    batch, t, d = x.shape
    seq = N_META + t
    tm = _row_tile(seq)
    x_tile = pl.BlockSpec(
        (pl.Element(tm), pl.Element(d)),
        lambda b, j: (pl.multiple_of(b * t + jnp.maximum(j * tm - N_META, 0), N_META), 0))
    return pl.pallas_call(
        _ffn_embed_kernel,
        grid=(batch, seq // tm),
        in_specs=[x_tile, _const_spec((N_META, d))] + _ffn_specs(d, wg.shape[1]),
        out_specs=pl.BlockSpec((tm, d), lambda b, j: (b * (seq // tm) + j, 0)),
        out_shape=jax.ShapeDtypeStruct((batch * seq, d), F32),
        scratch_shapes=[pltpu.VMEM((tm, d), F32)],
        compiler_params=_params(("parallel", "parallel")),
        name="ffn_embed",
    )(x.reshape(batch * t, d), meta_tokens.astype(F32), g, wg, wu, wd, g)


def _ffn_final(h, g, wg, wu, wd, final_g, batch, t):
    n, d = h.shape
    seq = n // batch
    tm = _row_tile(t)
    out = pl.pallas_call(
        functools.partial(_ffn_kernel, final=True),
        grid=(batch, t // tm),
        in_specs=[_grid_rows_spec(t, seq, tm, d)] + _ffn_specs(d, wg.shape[1]),
        out_specs=pl.BlockSpec((tm, d), lambda b, j: (b * (t // tm) + j, 0)),
        out_shape=jax.ShapeDtypeStruct((batch * t, d), F32),
        compiler_params=_params(("parallel", "parallel")),
        name="ffn_final",
    )(h, g, wg, wu, wd, final_g)
    return out.reshape(batch, t, d)


def _seq_tile(seq):
    best = None
    for tq in range(BF16_ROWS, MAX_ROW_TILE // SCAN_GROUP + 1, BF16_ROWS):
        if seq % tq == 0:
            best = tq
    assert best is not None, seq
    return best


def _inproj_kernel(h_ref, g_ref, w_ref, q_ref, k_ref, v_ref, xr_ref, yr_ref, gna_ref, glru_ref):
    nseq, tq, d = h_ref.shape
    xn = _rms(h_ref[...].reshape(nseq * tq, d), g_ref[...]).astype(BF16)

    def proj(j):
        return jnp.dot(xn, w_ref[:, j * d:(j + 1) * d], preferred_element_type=F32)

    zq = (proj(0) * np.float32(NA_HEAD_DIM ** -0.5)).astype(BF16)
    zk = proj(1).astype(BF16)
    zv = proj(2).astype(BF16)
    for p in range(N_PAIRS):
        sl = slice(p * LANES, (p + 1) * LANES)
        for j in range(nseq):
            rows = slice(j * tq, (j + 1) * tq)
            q_ref[p, j] = zq[rows, sl]
            k_ref[p, j] = zk[rows, sl]
            v_ref[p, j] = zv[rows, sl]
    xr = proj(3)
    for s in range(d // LANES):
        for j in range(nseq):
            xr_ref[s, pl.ds(j, tq, stride=nseq), :] = xr[j * tq:(j + 1) * tq, s * LANES:(s + 1) * LANES]
    yr_ref[...] = proj(4).reshape(nseq, tq, d).astype(BF16)
    gna_ref[...] = proj(5).reshape(nseq, tq, d).astype(BF16)
    glru_ref[...] = proj(6).reshape(nseq, tq, d).astype(BF16)


def _inproj(h, g, w_in, groups, seq):
    n, d = h.shape
    batch = groups * SCAN_GROUP
    tq = _seq_tile(seq)
    n_slabs = d // LANES
    tile = pl.BlockSpec((SCAN_GROUP, tq, d), lambda g_, i: (g_, i, 0))
    pair = pl.BlockSpec((N_PAIRS, SCAN_GROUP, tq, LANES), lambda g_, i: (0, g_, i, 0))
    slab = pl.BlockSpec((n_slabs, None, tq * SCAN_GROUP, LANES), lambda g_, i: (0, g_, i, 0))
    pair_shape = jax.ShapeDtypeStruct((N_PAIRS, batch, seq, LANES), BF16)
    tile_shape = jax.ShapeDtypeStruct((batch, seq, d), BF16)
    slab_shape = jax.ShapeDtypeStruct((n_slabs, groups, seq * SCAN_GROUP, LANES), F32)
    return pl.pallas_call(
        _inproj_kernel,
        grid=(groups, seq // tq),
        in_specs=[tile, _const_spec((1, d)), _const_spec(w_in.shape)],
        out_specs=[pair, pair, pair, slab, tile, tile, tile],
        out_shape=[pair_shape, pair_shape, pair_shape, slab_shape, tile_shape, tile_shape, tile_shape],
        compiler_params=_params(("parallel", "parallel")),
        name="inproj",
    )(h.reshape(batch, seq, d), g, w_in)


def _attn_block_cases(rows):
    nb = rows // ATTN_QROWS
    assert rows % ATTN_QROWS == 0 and nb >= 3 and nb % 2 == 0, rows

    def case(m):
        ks = min(max(ATTN_QROWS * m - NA_KH // 2, 0), rows - ATTN_KROWS)
        out = []
        for q in range(ATTN_QROWS):
            r = ATTN_QROWS * m + q
            rs = min(max(r - NA_KH // 2, 0), rows - NA_KH)
            out.append((rs - ks, r - rs))
        return tuple(out)

    assert all(case(m) == case(1) for m in range(1, nb - 1))
    return case(0), case(1), case(nb - 1)


def _attn_bias_table(rel_bias, rows):
    c = np.arange(GRID_W)
    cs = np.clip(c - NA_KW // 2, 0, GRID_W - NA_KW)
    kc = np.arange(GRID_W)
    ok = (kc[None, :] >= cs[:, None]) & (kc[None, :] < cs[:, None] + NA_KW)
    dcol = kc[None, :] - c[:, None] + NA_KW - 1
    onehot = (dcol[:, :, None] == np.arange(2 * NA_KW - 1)).astype(np.float32)
    cols = jnp.einsum('hdj,ckj->hdck', rel_bias.astype(F32), onehot, precision=lax.Precision.HIGHEST)
    cols = jnp.where(ok[None, None], cols, np.float32(MASK_VALUE))
    masked = jnp.full((NA_HEADS, GRID_W, GRID_W), np.float32(MASK_VALUE))
    tail = jnp.concatenate([jnp.zeros((NA_HEADS, GRID_W, N_META), F32),
                            jnp.full((NA_HEADS, GRID_W, ATTN_KEXT - ATTN_KLOC - N_META), np.float32(MASK_VALUE))],
                           axis=-1)
    per_case = []
    for cases in _attn_block_cases(rows):
        for off, e in cases:
            pieces = [cols[:, i - off + NA_KH - 1 - e] if 0 <= i - off < NA_KH else masked
                      for i in range(ATTN_KROWS)]
            per_case.append(jnp.concatenate(pieces + [tail], axis=-1))
    tbl = jnp.stack(per_case, axis=1)
    tbl = tbl.reshape(N_PAIRS, 2, 3, ATTN_QROWS, GRID_W, ATTN_KEXT)
    return jnp.transpose(tbl, (0, 2, 1, 3, 4, 5)).reshape(N_PAIRS, 3, 2 * ATTN_QROWS * GRID_W, ATTN_KEXT)


def _attn_kernel(q_ref, k_ref, v_ref, bias_ref, o_ref, kx_ref, vx_ref, s_ref, p_ref, den_ref):
    n_seq, seq, _ = q_ref.shape
    rows = (seq - N_META) // GRID_W
    nb = rows // ATTN_QROWS
    n_blocks = n_seq * nb
    nq = ATTN_QROWS * GRID_W
    nt = (((1,), (1,)), ((), ()))
    lo = lax.broadcasted_iota(jnp.int32, (1, LANES), 1) < NA_HEAD_DIM

    def stack_heads(x):
        zero = jnp.zeros_like(x)
        return jnp.concatenate([jnp.where(lo, x, zero), jnp.where(lo, zero, x)], axis=0)

    def pick_heads(o, n):
        return jnp.where(lo, o[:n], o[n:])

    for sq in range(n_seq):
        km = k_ref[sq, 0:N_META, :]
        vm = v_ref[sq, 0:N_META, :]
        qm = stack_heads(q_ref[sq, 0:N_META, :])
        sm = lax.dot_general(qm, km, nt, preferred_element_type=F32)
        em = jnp.exp(sm - jnp.max(sm, axis=-1, keepdims=True))
        om = jnp.dot(em.astype(BF16), vm, preferred_element_type=F32)
        om = om / jnp.sum(em, axis=-1, keepdims=True)
        o_ref[sq, 0:N_META, :] = pick_heads(om, N_META).astype(o_ref.dtype)

    pad = jnp.zeros((ATTN_KEXT - ATTN_KLOC - N_META, LANES), BF16)
    for j in range(2):
        kx_ref[j, ATTN_KLOC + N_META:, :] = pad
        vx_ref[j, ATTN_KLOC + N_META:, :] = pad

    def locate(m):
        sq = m // nb
        local = m - sq * nb
        ks = jnp.clip(ATTN_QROWS * local - NA_KH // 2, 0, rows - ATTN_KROWS)
        q0 = pl.multiple_of(N_META + local * nq, BF16_ROWS)
        k0 = pl.multiple_of(N_META + ks * GRID_W, BF16_ROWS)
        return sq, local, q0, k0

    def scores(m, j):
        sq, local, q0, k0 = locate(m)
        kx_ref[j, 0:ATTN_KLOC, :] = k_ref[sq, pl.ds(k0, ATTN_KLOC), :]
        kx_ref[j, ATTN_KLOC:ATTN_KLOC + N_META, :] = k_ref[sq, 0:N_META, :]
        qs = stack_heads(q_ref[sq, pl.ds(q0, nq), :])
        kind = jnp.where(local == 0, 0, jnp.where(local == nb - 1, 2, 1))
        s_ref[j] = lax.dot_general(qs, kx_ref[j], nt, preferred_element_type=F32) + bias_ref[kind]

    def softmax(j):
        s = s_ref[j]
        e = jnp.exp(s - jnp.max(s, axis=-1, keepdims=True))
        den_ref[j] = jnp.sum(e, axis=-1, keepdims=True)
        p_ref[j] = e.astype(BF16)

    def values(m, j):
        sq, _, q0, k0 = locate(m)
        vx_ref[j, 0:ATTN_KLOC, :] = v_ref[sq, pl.ds(k0, ATTN_KLOC), :]
        vx_ref[j, ATTN_KLOC:ATTN_KLOC + N_META, :] = v_ref[sq, 0:N_META, :]
        o = jnp.dot(p_ref[j], vx_ref[j], preferred_element_type=F32) / den_ref[j]
        o_ref[sq, pl.ds(q0, nq), :] = pick_heads(o, nq).astype(o_ref.dtype)

    scores(0, 0)
    scores(1, 1)
    softmax(0)

    def two_blocks(i, carry):
        m = 2 * i + 2
        scores(m, 0)
        softmax(1)
        values(m - 2, 0)
        scores(m + 1, 1)
        softmax(0)
        values(m - 1, 1)
        return carry

    lax.fori_loop(0, n_blocks // 2 - 1, two_blocks, 0)
    softmax(1)
    values(n_blocks - 2, 0)
    values(n_blocks - 1, 1)


def _attention(q, k, v, bias_tbl):
    _, batch, seq, _ = q.shape
    blocks_per_seq = (seq - N_META) // (GRID_W * ATTN_QROWS)
    n_seq = max(1, min(batch, ATTN_BLOCKS_PER_STEP // blocks_per_seq))
    assert batch % n_seq == 0
    blk = pl.BlockSpec((None, n_seq, seq, LANES), lambda p, b: (p, b, 0, 0))
    bias_spec = pl.BlockSpec((None,) + bias_tbl.shape[1:], lambda p, b: (p, 0, 0, 0))
    out = pl.pallas_call(
        _attn_kernel,
        grid=(N_PAIRS, batch // n_seq),
        in_specs=[blk, blk, blk, bias_spec],
        out_specs=blk,
        out_shape=jax.ShapeDtypeStruct(q.shape, BF16),
        scratch_shapes=[pltpu.VMEM((2, ATTN_KEXT, LANES), BF16), pltpu.VMEM((2, ATTN_KEXT, LANES), BF16),
                        pltpu.VMEM((2, 2 * ATTN_QROWS * GRID_W, ATTN_KEXT), F32),
                        pltpu.VMEM((2, 2 * ATTN_QROWS * GRID_W, ATTN_KEXT), BF16),
                        pltpu.VMEM((2, 2 * ATTN_QROWS * GRID_W, 1), F32)],
        compiler_params=_params(("parallel", "parallel")),
        name="attention",
    )(q, k, v, bias_tbl)
    return out


def _scan_time_block(seq):
    best = None
    for tt in range(LRU_TC, LRU_MAX_TT + 1, LRU_TC):
        if seq % tt == 0:
            best = tt
    assert best is not None, seq
    return best


def _lru_kernel(xl_ref, x_ref, xn_ref, par_ref, w_ref, *rest, nt, reverse):
    acc_ref, o_ref, x_s, carry = rest if reverse else (None,) + rest
    step = pl.program_id(2)
    n_slabs, n_rows, _ = x_ref.shape
    blk = (nt - 1 - step) if reverse else step
    has_left = blk > 0
    has_right = blk < nt - 1
    left_rows = xl_ref.shape[1]
    chunk_rows = LRU_TC * SCAN_GROUP
    n_chunks = n_rows // chunk_rows

    @pl.when(step == 0)
    def _():
        carry[...] = jnp.zeros_like(carry)

    for s in range(n_slabs):
        sl = slice(s * LANES, (s + 1) * LANES)
        x_s[0:left_rows, sl] = jnp.where(has_left, xl_ref[s], jnp.zeros_like(xl_ref[s]))
        x_s[left_rows:left_rows + n_rows, sl] = x_ref[s]
        x_s[left_rows + n_rows:, sl] = jnp.where(has_right, xn_ref[s], jnp.zeros_like(xn_ref[s]))

    cw = par_ref[0:CONV_W, :]
    cb = par_ref[CONV_W:CONV_W + 1, :]
    ba = par_ref[CONV_W + 1:CONV_W + 2, :]
    bx = par_ref[CONV_W + 2:CONV_W + 3, :]
    lam = par_ref[CONV_W + 3:CONV_W + 4, :]
    softplus = jnp.maximum(-lam, 0.0) + jnp.log1p(jnp.exp(-jnp.abs(lam)))
    neg_c_sp = np.float32(-LRU_C) * softplus

    def chunk(i, h):
        c = (n_chunks - 1 - i) if reverse else i
        row0 = pl.multiple_of(c * chunk_rows, chunk_rows)
        xc = cb
        for j in range(CONV_W):
            xc = xc + x_s[pl.ds(row0 + j * SCAN_GROUP, chunk_rows), :] * cw[j:j + 1, :]
        pre = jnp.dot(xc.astype(BF16), w_ref[...], preferred_element_type=F32)
        r = jax.nn.sigmoid(pre[:, :LRU_CHUNK] + ba)
        gi = jax.nn.sigmoid(pre[:, LRU_CHUNK:] + bx)
        log_a = neg_c_sp * r
        a = jnp.exp(log_a)
        y = -jnp.tanh(log_a) * (a * a + 1.0)
        root = jnp.where(y > 0.0, y * lax.rsqrt(y), 0.0)
        u = root * (gi * xc)
        hs = [None] * LRU_TC
        order = range(LRU_TC - 1, -1, -1) if reverse else range(LRU_TC)
        for k in order:
            rows = slice(k * SCAN_GROUP, (k + 1) * SCAN_GROUP)
            h = a[rows] * h + u[rows]
            hs[k] = h
        hc = jnp.concatenate(hs, axis=0)
        for s in range(n_slabs):
            part = hc[:, s * LANES:(s + 1) * LANES]
            if reverse:
                part = part + acc_ref[s, pl.ds(row0, chunk_rows), :]
            o_ref[s, pl.ds(row0, chunk_rows), :] = part
        return h

    carry[...] = lax.fori_loop(0, n_chunks, chunk, carry[...], unroll=4)


def _lru(xr, conv_w, conv_b, w_gate, ba, bx, lam, seq, forward_out=None):
    reverse = forward_out is not None
    n_slabs, groups, _, _ = xr.shape
    c = n_slabs * LANES
    tt = _scan_time_block(seq)
    nt = seq // tt
    n_chunks = c // LRU_CHUNK
    slabs = LRU_CHUNK // LANES
    d = 1 if reverse else 0
    left_steps = CONV_LEFT
    right_steps = CONV_W - 1 - CONV_LEFT
    assert tt % left_steps == 0 and right_steps == 1

    def tblk(i):
        return (nt - 1 - i) if reverse else i

    cur = pl.BlockSpec((slabs, None, tt * SCAN_GROUP, LANES), lambda g, ch, i: (ch, g, tblk(i), 0))
    left = pl.BlockSpec((slabs, None, left_steps * SCAN_GROUP, LANES),
                        lambda g, ch, i: (ch, g, jnp.maximum(tblk(i) * (tt // left_steps) - 1, 0), 0))
    right = pl.BlockSpec((slabs, None, right_steps * SCAN_GROUP, LANES),
                         lambda g, ch, i: (ch, g, jnp.minimum((tblk(i) + 1) * tt, seq - 1), 0))
    params = jnp.concatenate([conv_w, conv_b[None], ba[d][None], bx[d][None], lam[d][None]], axis=0)
    assert params.shape == (CONV_W + 4, c)
    return pl.pallas_call(
        functools.partial(_lru_kernel, nt=nt, reverse=reverse),
        grid=(groups, n_chunks, nt),
        in_specs=[left, cur, right, pl.BlockSpec((CONV_W + 4, LRU_CHUNK), lambda g, ch, i: (0, ch)),
                  pl.BlockSpec((None, None, LRU_CHUNK, 2 * LRU_CHUNK), lambda g, ch, i: (d, ch, 0, 0))]
                 + ([cur] if reverse else []),
        out_specs=cur,
        out_shape=jax.ShapeDtypeStruct(xr.shape, F32),
        input_output_aliases={5: 0} if reverse else {},
        scratch_shapes=[pltpu.VMEM(((tt + CONV_W - 1) * SCAN_GROUP, LRU_CHUNK), F32),
                        pltpu.VMEM((SCAN_GROUP, LRU_CHUNK), F32)],
        compiler_params=_params(("parallel", "parallel", "arbitrary")),
        name="lru_bwd" if reverse else "lru_fwd",
    )(xr, xr, xr, params, w_gate, *([forward_out] if reverse else []))


def _lru_gate_weights(wa, wx):
    per = LRU_CHUNK // LRU_BLOCK_DIM

    def dense(w):
        n_dir, n_blk, bd, _ = w.shape
        w = w.reshape(n_dir, n_blk // per, per, bd, bd)
        eye = jnp.eye(per, dtype=w.dtype)
        full = jnp.einsum('dcpij,pq->dcpiqj', w, eye)
        return full.reshape(n_dir, n_blk // per, LRU_CHUNK, LRU_CHUNK)

    return jnp.concatenate([dense(wa), dense(wx)], axis=-1).astype(BF16)


def _mixout_kernel(h_ref, na_ref, yr_ref, gna_ref, glru_ref, hr_ref, wn_ref, wl_ref, wo_ref, o_ref):
    nseq, tq, d = h_ref.shape
    rows = lambda ref: ref[...].reshape(nseq * tq, d)

    na = jnp.concatenate(
        [jnp.concatenate([na_ref[p, j] for j in range(nseq)], axis=0) for p in range(N_PAIRS)], axis=-1)
    na_p = jnp.dot(na, wn_ref[...], preferred_element_type=F32)
    hr = jnp.concatenate(
        [jnp.concatenate([hr_ref[s, pl.ds(j, tq, stride=nseq), :] for j in range(nseq)], axis=0)
         for s in range(d // LANES)], axis=-1)
    lru_in = (_gelu_tanh(rows(yr_ref).astype(F32)) * hr).astype(BF16)
    lru_p = jnp.dot(lru_in, wl_ref[...], preferred_element_type=F32)
    merged = (jax.nn.sigmoid(rows(gna_ref).astype(F32)) * na_p
              + jax.nn.sigmoid(rows(glru_ref).astype(F32)) * lru_p)
    out = rows(h_ref) + jnp.dot(merged.astype(BF16), wo_ref[...], preferred_element_type=F32)
    o_ref[...] = out.reshape(nseq, tq, d)


def _mixout(h, na, yr, gna, glru, hr, wn, wl, wo):
    n, d = h.shape
    _, batch, seq, _ = na.shape
    groups = batch // SCAN_GROUP
    tq = _seq_tile(seq)
    n_slabs = d // LANES
    tile = pl.BlockSpec((SCAN_GROUP, tq, d), lambda g_, i: (g_, i, 0))
    pair = pl.BlockSpec((N_PAIRS, SCAN_GROUP, tq, LANES), lambda g_, i: (0, g_, i, 0))
    slab = pl.BlockSpec((n_slabs, None, tq * SCAN_GROUP, LANES), lambda g_, i: (0, g_, i, 0))
    out = pl.pallas_call(
        _mixout_kernel,
        grid=(groups, seq // tq),
        in_specs=[tile, pair, tile, tile, tile, slab,
                  _const_spec((d, d)), _const_spec((d, d)), _const_spec((d, d))],
        out_specs=tile,
        out_shape=jax.ShapeDtypeStruct((batch, seq, d), F32),
        input_output_aliases={0: 0},
        compiler_params=_params(("parallel", "parallel")),
        name="mixout",
    )(h.reshape(batch, seq, d), na, yr, gna, glru, hr, wn, wl, wo)
    return out.reshape(n, d)


def _prepare_layer(l, p, rows):
    d = D_MODEL
    row = lambda a: a[l].reshape(1, d).astype(F32)
    return dict(
        norm_ffn1=row(p["norm_ffn1"]), norm_mix=row(p["norm_mix"]), norm_ffn2=row(p["norm_ffn2"]),
        ffn1=(p["ffn1_w_gate"][l].astype(BF16), p["ffn1_w_up"][l].astype(BF16), p["ffn1_w_down"][l].astype(BF16)),
        ffn2=(p["ffn2_w_gate"][l].astype(BF16), p["ffn2_w_up"][l].astype(BF16), p["ffn2_w_down"][l].astype(BF16)),
        w_in=p["w_in"][l].astype(BF16),
        bias_tbl=_attn_bias_table(p["na_rel_bias"][l], rows),
        conv_w=p["conv_w"][l].astype(F32), conv_b=p["conv_b"][l].astype(F32),
        w_gate=_lru_gate_weights(p["lru_wa"][l], p["lru_wx"][l]),
        ba=p["lru_ba"][l].astype(F32), bx=p["lru_bx"][l].astype(F32), lam=p["lru_lambda"][l].astype(F32),
        w_na_proj=p["w_na_proj"][l].astype(BF16), w_lru_proj=p["w_lru_proj"][l].astype(BF16),
        w_out=p["w_out"][l].astype(BF16),
    )


def _encode(x, meta_tokens, layers, final_norm):
    batch, t, d = x.shape
    seq = N_META + t
    assert batch % SCAN_GROUP == 0 and seq % N_META == 0
    groups = batch // SCAN_GROUP
    gf = final_norm.reshape(1, d).astype(F32)
    h = None
    for li, lp in enumerate(layers):
        if li == 0:
            h = _ffn_embed(x, meta_tokens, lp["norm_ffn1"], *lp["ffn1"])
        else:
            h = _ffn(h, lp["norm_ffn1"], *lp["ffn1"])
        q, k, v, xr, yr, gna, glru = _inproj(h, lp["norm_mix"], lp["w_in"], groups, seq)
        na = _attention(q, k, v, lp["bias_tbl"])
        lru_args = (xr, lp["conv_w"], lp["conv_b"], lp["w_gate"], lp["ba"], lp["bx"], lp["lam"], seq)
        hr = _lru(*lru_args, forward_out=_lru(*lru_args))
        h = _mixout(h, na, yr, gna, glru, hr, lp["w_na_proj"], lp["w_lru_proj"], lp["w_out"])
        if li < len(layers) - 1:
            h = _ffn(h, lp["norm_ffn2"], *lp["ffn2"])
    return _ffn_final(h, layers[-1]["norm_ffn2"], *layers[-1]["ffn2"], gf, batch, t)


def kernel(x_prompt, x_sample, meta_tokens, norm_ffn1, ffn1_w_gate, ffn1_w_up, ffn1_w_down, norm_mix, w_in, na_rel_bias, conv_w, conv_b, lru_wa, lru_ba, lru_wx, lru_bx, lru_lambda, w_na_proj, w_lru_proj, w_out, norm_ffn2, ffn2_w_gate, ffn2_w_up, ffn2_w_down, final_norm):
    p = dict(norm_ffn1=norm_ffn1, ffn1_w_gate=ffn1_w_gate, ffn1_w_up=ffn1_w_up, ffn1_w_down=ffn1_w_down,
             norm_mix=norm_mix, w_in=w_in, na_rel_bias=na_rel_bias, conv_w=conv_w, conv_b=conv_b,
             lru_wa=lru_wa, lru_ba=lru_ba, lru_wx=lru_wx, lru_bx=lru_bx, lru_lambda=lru_lambda,
             w_na_proj=w_na_proj, w_lru_proj=w_lru_proj, w_out=w_out, norm_ffn2=norm_ffn2,
             ffn2_w_gate=ffn2_w_gate, ffn2_w_up=ffn2_w_up, ffn2_w_down=ffn2_w_down)
    rows = x_prompt.shape[1] // GRID_W
    assert _attn_block_cases(rows) == _attn_block_cases(x_sample.shape[1] // GRID_W)
    layers = [_prepare_layer(l, p, rows) for l in range(norm_ffn1.shape[0])]
    y_prompt = _encode(x_prompt, meta_tokens, layers, final_norm)
    y_sample = _encode(x_sample, meta_tokens, layers, final_norm)
    return (y_prompt, y_sample)
```

```python
import functools

import numpy as np
import jax
import jax.numpy as jnp
from jax import lax
from jax.experimental import pallas as pl
from jax.experimental.pallas import tpu as pltpu

F32 = jnp.float32
BF16 = jnp.bfloat16

D_MODEL = 1024
N_META = 16
GRID_W = 64
NA_HEADS = 16
NA_HEAD_DIM = 64
NA_KH = 8
NA_KW = 16
LRU_BLOCK_DIM = 64
CONV_W = 4
CONV_LEFT = 2
LRU_C = 8.0
RMS_EPS = 1e-6
MASK_VALUE = -1e30

LANES = 128
SUBLANES = 8
BF16_ROWS = 16
MXU_TILE = 256
VMEM_LIMIT = 56 * 1024 * 1024
MAX_ROW_TILE = 704
N_PAIRS = NA_HEADS // 2
ATTN_QROWS = 4
ATTN_KROWS = ATTN_QROWS + NA_KH - 1
ATTN_KLOC = ATTN_KROWS * GRID_W
ATTN_KEXT = -(-(ATTN_KLOC + N_META) // MXU_TILE) * MXU_TILE
ATTN_BLOCKS_PER_STEP = 64
LRU_CHUNK = MXU_TILE
SCAN_GROUP = SUBLANES
LRU_TC = 16
LRU_MAX_TT = 704


def _row_tile(n):
    for t in range(MAX_ROW_TILE, 0, -BF16_ROWS):
        if n % t == 0:
            return t
    raise ValueError(f"no row tile for {n} rows")


def _const_spec(shape):
    zeros = (0,) * len(shape)
    return pl.BlockSpec(shape, lambda *_: zeros, pipeline_mode=pl.Buffered(1))


def _params(sem):
    return pltpu.CompilerParams(dimension_semantics=sem, vmem_limit_bytes=VMEM_LIMIT)


def _rms(x, g):
    inv = lax.rsqrt(jnp.mean(x * x, axis=-1, keepdims=True) + RMS_EPS)
    return (x * inv) * g


def _gelu_tanh(x):
    c = np.float32(np.sqrt(2.0 / np.pi))
    return x * (0.5 * (1.0 + jnp.tanh(c * (x + np.float32(0.044715) * (x * x * x)))))


def _ffn_kernel(h_ref, g_ref, wg_ref, wu_ref, wd_ref, gf_ref, o_ref, *, final):
    x = h_ref[...]
    xn = _rms(x, g_ref[...]).astype(BF16)
    gate = jnp.dot(xn, wg_ref[...], preferred_element_type=F32)
    up = jnp.dot(xn, wu_ref[...], preferred_element_type=F32)
    act = (gate * jax.nn.sigmoid(gate) * up).astype(BF16)
    down = jnp.dot(act, wd_ref[...], preferred_element_type=F32)
    y = x + 0.5 * down
    o_ref[...] = _rms(y, gf_ref[...]) if final else y


def _ffn_specs(d, dff):
    return [_const_spec((1, d)), _const_spec((d, dff)), _const_spec((d, dff)), _const_spec((dff, d)),
            _const_spec((1, d))]


def _ffn(h, g, wg, wu, wd):
    n, d = h.shape
    tm = _row_tile(n)
    row = pl.BlockSpec((tm, d), lambda i: (i, 0))
    return pl.pallas_call(
        functools.partial(_ffn_kernel, final=False),
        grid=(n // tm,),
        in_specs=[row] + _ffn_specs(d, wg.shape[1]),
        out_specs=row,
        out_shape=jax.ShapeDtypeStruct((n, d), F32),
        input_output_aliases={0: 0},
        compiler_params=_params(("parallel",)),
        name="ffn",
    )(h, g, wg, wu, wd, g)


def _grid_rows_spec(t, seq, tm, d):
    assert seq % N_META == 0 and tm % N_META == 0
    return pl.BlockSpec((pl.Element(tm), pl.Element(d)),
                        lambda b, j: (pl.multiple_of(b * seq + N_META + j * tm, N_META), 0))


def _ffn_embed_kernel(x_ref, m_ref, g_ref, wg_ref, wu_ref, wd_ref, gf_ref, o_ref, xin_ref):
    tm = x_ref.shape[0]

    @pl.when(pl.program_id(1) == 0)
    def _():
        xin_ref[0:N_META, :] = m_ref[...]
        xin_ref[N_META:, :] = x_ref[0:tm - N_META, :]
        _ffn_kernel(xin_ref, g_ref, wg_ref, wu_ref, wd_ref, gf_ref, o_ref, final=False)

    @pl.when(pl.program_id(1) > 0)
    def _():
        _ffn_kernel(x_ref, g_ref, wg_ref, wu_ref, wd_ref, gf_ref, o_ref, final=False)


def _ffn_embed(x, meta_tokens, g, wg, wu, wd):
    batch, t, d = x.shape
    seq = N_META + t
    tm = _row_tile(seq)
    x_tile = pl.BlockSpec(
        (pl.Element(tm), pl.Element(d)),
        lambda b, j: (pl.multiple_of(b * t + jnp.maximum(j * tm - N_META, 0), N_META), 0))
    return pl.pallas_call(
        _ffn_embed_kernel,
        grid=(batch, seq // tm),
        in_specs=[x_tile, _const_spec((N_META, d))] + _ffn_specs(d, wg.shape[1]),
        out_specs=pl.BlockSpec((tm, d), lambda b, j: (b * (seq // tm) + j, 0)),
        out_shape=jax.ShapeDtypeStruct((batch * seq, d), F32),
        scratch_shapes=[pltpu.VMEM((tm, d), F32)],
        compiler_params=_params(("parallel", "parallel")),
        name="ffn_embed",
    )(x.reshape(batch * t, d), meta_tokens.astype(F32), g, wg, wu, wd, g)


def _ffn_final(h, g, wg, wu, wd, final_g, batch, t):
    n, d = h.shape
    seq = n // batch
    tm = _row_tile(t)
    out = pl.pallas_call(
        functools.partial(_ffn_kernel, final=True),
        grid=(batch, t // tm),
        in_specs=[_grid_rows_spec(t, seq, tm, d)] + _ffn_specs(d, wg.shape[1]),
        out_specs=pl.BlockSpec((tm, d), lambda b, j: (b * (t // tm) + j, 0)),
        out_shape=jax.ShapeDtypeStruct((batch * t, d), F32),
        compiler_params=_params(("parallel", "parallel")),
        name="ffn_final",
    )(h, g, wg, wu, wd, final_g)
    return out.reshape(batch, t, d)


def _seq_tile(seq):
    best = None
    for tq in range(BF16_ROWS, MAX_ROW_TILE // SCAN_GROUP + 1, BF16_ROWS):
        if seq % tq == 0:
            best = tq
    assert best is not None, seq
    return best


def _inproj_kernel(h_ref, g_ref, w_ref, q_ref, k_ref, v_ref, xr_ref, yr_ref, gna_ref, glru_ref):
    nseq, tq, d = h_ref.shape
    xn = _rms(h_ref[...].reshape(nseq * tq, d), g_ref[...]).astype(BF16)

    def proj(j):
        return jnp.dot(xn, w_ref[:, j * d:(j + 1) * d], preferred_element_type=F32)

    zq = (proj(0) * np.float32(NA_HEAD_DIM ** -0.5)).astype(BF16)
    zk = proj(1).astype(BF16)
    zv = proj(2).astype(BF16)
    for p in range(N_PAIRS):
        sl = slice(p * LANES, (p + 1) * LANES)
        for j in range(nseq):
            rows = slice(j * tq, (j + 1) * tq)
            q_ref[p, j] = zq[rows, sl]
            k_ref[p, j] = zk[rows, sl]
            v_ref[p, j] = zv[rows, sl]
    xr = proj(3)
    for s in range(d // LANES):
        for j in range(nseq):
            xr_ref[s, pl.ds(j, tq, stride=nseq), :] = xr[j * tq:(j + 1) * tq, s * LANES:(s + 1) * LANES]
    yr_ref[...] = proj(4).reshape(nseq, tq, d).astype(BF16)
    gna_ref[...] = proj(5).reshape(nseq, tq, d).astype(BF16)
    glru_ref[...] = proj(6).reshape(nseq, tq, d).astype(BF16)


def _inproj(h, g, w_in, groups, seq):
    n, d = h.shape
    batch = groups * SCAN_GROUP
    tq = _seq_tile(seq)
    n_slabs = d // LANES
    tile = pl.BlockSpec((SCAN_GROUP, tq, d), lambda g_, i: (g_, i, 0))
    pair = pl.BlockSpec((N_PAIRS, SCAN_GROUP, tq, LANES), lambda g_, i: (0, g_, i, 0))
    slab = pl.BlockSpec((n_slabs, None, tq * SCAN_GROUP, LANES), lambda g_, i: (0, g_, i, 0))
    pair_shape = jax.ShapeDtypeStruct((N_PAIRS, batch, seq, LANES), BF16)
    tile_shape = jax.ShapeDtypeStruct((batch, seq, d), BF16)
    slab_shape = jax.ShapeDtypeStruct((n_slabs, groups, seq * SCAN_GROUP, LANES), F32)
    return pl.pallas_call(
        _inproj_kernel,
        grid=(groups, seq // tq),
        in_specs=[tile, _const_spec((1, d)), _const_spec(w_in.shape)],
        out_specs=[pair, pair, pair, slab, tile, tile, tile],
        out_shape=[pair_shape, pair_shape, pair_shape, slab_shape, tile_shape, tile_shape, tile_shape],
        compiler_params=_params(("parallel", "parallel")),
        name="inproj",
    )(h.reshape(batch, seq, d), g, w_in)


def _attn_block_cases(rows):
    nb = rows // ATTN_QROWS
    assert rows % ATTN_QROWS == 0 and nb >= 3 and nb % 2 == 0, rows

    def case(m):
        ks = min(max(ATTN_QROWS * m - NA_KH // 2, 0), rows - ATTN_KROWS)
        out = []
        for q in range(ATTN_QROWS):
            r = ATTN_QROWS * m + q
            rs = min(max(r - NA_KH // 2, 0), rows - NA_KH)
            out.append((rs - ks, r - rs))
        return tuple(out)

    assert all(case(m) == case(1) for m in range(1, nb - 1))
    return case(0), case(1), case(nb - 1)


def _attn_bias_table(rel_bias, rows):
    c = np.arange(GRID_W)
    cs = np.clip(c - NA_KW // 2, 0, GRID_W - NA_KW)
    kc = np.arange(GRID_W)
    ok = (kc[None, :] >= cs[:, None]) & (kc[None, :] < cs[:, None] + NA_KW)
    dcol = kc[None, :] - c[:, None] + NA_KW - 1
    onehot = (dcol[:, :, None] == np.arange(2 * NA_KW - 1)).astype(np.float32)
    cols = jnp.einsum('hdj,ckj->hdck', rel_bias.astype(F32), onehot, precision=lax.Precision.HIGHEST)
    cols = jnp.where(ok[None, None], cols, np.float32(MASK_VALUE))
    masked = jnp.full((NA_HEADS, GRID_W, GRID_W), np.float32(MASK_VALUE))
    tail = jnp.concatenate([jnp.zeros((NA_HEADS, GRID_W, N_META), F32),
                            jnp.full((NA_HEADS, GRID_W, ATTN_KEXT - ATTN_KLOC - N_META), np.float32(MASK_VALUE))],
                           axis=-1)
    per_case = []
    for cases in _attn_block_cases(rows):
        for off, e in cases:
            pieces = [cols[:, i - off + NA_KH - 1 - e] if 0 <= i - off < NA_KH else masked
                      for i in range(ATTN_KROWS)]
            per_case.append(jnp.concatenate(pieces + [tail], axis=-1))
    tbl = jnp.stack(per_case, axis=1)
    tbl = tbl.reshape(N_PAIRS, 2, 3, ATTN_QROWS, GRID_W, ATTN_KEXT)
    return jnp.transpose(tbl, (0, 2, 1, 3, 4, 5)).reshape(N_PAIRS, 3, 2 * ATTN_QROWS * GRID_W, ATTN_KEXT)


def _attn_kernel(q_ref, k_ref, v_ref, bias_ref, o_ref, kx_ref, vx_ref, s_ref, p_ref, den_ref):
    n_seq, seq, _ = q_ref.shape
    rows = (seq - N_META) // GRID_W
    nb = rows // ATTN_QROWS
    n_blocks = n_seq * nb
    nq = ATTN_QROWS * GRID_W
    nt = (((1,), (1,)), ((), ()))
    lo = lax.broadcasted_iota(jnp.int32, (1, LANES), 1) < NA_HEAD_DIM

    def stack_heads(x):
        zero = jnp.zeros_like(x)
        return jnp.concatenate([jnp.where(lo, x, zero), jnp.where(lo, zero, x)], axis=0)

    def pick_heads(o, n):
        return jnp.where(lo, o[:n], o[n:])

    for sq in range(n_seq):
        km = k_ref[sq, 0:N_META, :]
        vm = v_ref[sq, 0:N_META, :]
        qm = stack_heads(q_ref[sq, 0:N_META, :])
        sm = lax.dot_general(qm, km, nt, preferred_element_type=F32)
        em = jnp.exp(sm - jnp.max(sm, axis=-1, keepdims=True))
        om = jnp.dot(em.astype(BF16), vm, preferred_element_type=F32)
        om = om / jnp.sum(em, axis=-1, keepdims=True)
        o_ref[sq, 0:N_META, :] = pick_heads(om, N_META).astype(o_ref.dtype)

    pad = jnp.zeros((ATTN_KEXT - ATTN_KLOC - N_META, LANES), BF16)
    for j in range(2):
        kx_ref[j, ATTN_KLOC + N_META:, :] = pad
        vx_ref[j, ATTN_KLOC + N_META:, :] = pad

    def locate(m):
        sq = m // nb
        local = m - sq * nb
        ks = jnp.clip(ATTN_QROWS * local - NA_KH // 2, 0, rows - ATTN_KROWS)
        q0 = pl.multiple_of(N_META + local * nq, BF16_ROWS)
        k0 = pl.multiple_of(N_META + ks * GRID_W, BF16_ROWS)
        return sq, local, q0, k0

    def scores(m, j):
        sq, local, q0, k0 = locate(m)
        kx_ref[j, 0:ATTN_KLOC, :] = k_ref[sq, pl.ds(k0, ATTN_KLOC), :]
        kx_ref[j, ATTN_KLOC:ATTN_KLOC + N_META, :] = k_ref[sq, 0:N_META, :]
        qs = stack_heads(q_ref[sq, pl.ds(q0, nq), :])
        kind = jnp.where(local == 0, 0, jnp.where(local == nb - 1, 2, 1))
        s_ref[j] = lax.dot_general(qs, kx_ref[j], nt, preferred_element_type=F32) + bias_ref[kind]

    def softmax(j):
        s = s_ref[j]
        e = jnp.exp(s - jnp.max(s, axis=-1, keepdims=True))
        den_ref[j] = jnp.sum(e, axis=-1, keepdims=True)
        p_ref[j] = e.astype(BF16)

    def values(m, j):
        sq, _, q0, k0 = locate(m)
        vx_ref[j, 0:ATTN_KLOC, :] = v_ref[sq, pl.ds(k0, ATTN_KLOC), :]
        vx_ref[j, ATTN_KLOC:ATTN_KLOC + N_META, :] = v_ref[sq, 0:N_META, :]
        o = jnp.dot(p_ref[j], vx_ref[j], preferred_element_type=F32) / den_ref[j]
        o_ref[sq, pl.ds(q0, nq), :] = pick_heads(o, nq).astype(o_ref.dtype)

    scores(0, 0)
    scores(1, 1)
    softmax(0)

    def two_blocks(i, carry):
        m = 2 * i + 2
        scores(m, 0)
        softmax(1)
        values(m - 2, 0)
        scores(m + 1, 1)
        softmax(0)
        values(m - 1, 1)
        return carry

    lax.fori_loop(0, n_blocks // 2 - 1, two_blocks, 0)
    softmax(1)
    values(n_blocks - 2, 0)
    values(n_blocks - 1, 1)


def _attention(q, k, v, bias_tbl):
    _, batch, seq, _ = q.shape
    blocks_per_seq = (seq - N_META) // (GRID_W * ATTN_QROWS)
    n_seq = max(1, min(batch, ATTN_BLOCKS_PER_STEP // blocks_per_seq))
    assert batch % n_seq == 0
    blk = pl.BlockSpec((None, n_seq, seq, LANES), lambda p, b: (p, b, 0, 0))
    bias_spec = pl.BlockSpec((None,) + bias_tbl.shape[1:], lambda p, b: (p, 0, 0, 0))
    out = pl.pallas_call(
        _attn_kernel,
        grid=(N_PAIRS, batch // n_seq),
        in_specs=[blk, blk, blk, bias_spec],
        out_specs=blk,
        out_shape=jax.ShapeDtypeStruct(q.shape, BF16),
        scratch_shapes=[pltpu.VMEM((2, ATTN_KEXT, LANES), BF16), pltpu.VMEM((2, ATTN_KEXT, LANES), BF16),
                        pltpu.VMEM((2, 2 * ATTN_QROWS * GRID_W, ATTN_KEXT), F32),
                        pltpu.VMEM((2, 2 * ATTN_QROWS * GRID_W, ATTN_KEXT), BF16),
                        pltpu.VMEM((2, 2 * ATTN_QROWS * GRID_W, 1), F32)],
        compiler_params=_params(("parallel", "parallel")),
        name="attention",
    )(q, k, v, bias_tbl)
    return out


def _scan_time_block(seq):
    best = None
    for tt in range(LRU_TC, LRU_MAX_TT + 1, LRU_TC):
        if seq % tt == 0:
            best = tt
    assert best is not None, seq
    return best


def _lru_kernel(xl_ref, x_ref, xn_ref, par_ref, w_ref, *rest, nt, reverse):
    acc_ref, o_ref, x_s, carry = rest if reverse else (None,) + rest
    step = pl.program_id(2)
    n_slabs, n_rows, _ = x_ref.shape
    blk = (nt - 1 - step) if reverse else step
    has_left = blk > 0
    has_right = blk < nt - 1
    left_rows = xl_ref.shape[1]
    chunk_rows = LRU_TC * SCAN_GROUP
    n_chunks = n_rows // chunk_rows

    @pl.when(step == 0)
    def _():
        carry[...] = jnp.zeros_like(carry)

    for s in range(n_slabs):
        sl = slice(s * LANES, (s + 1) * LANES)
        x_s[0:left_rows, sl] = jnp.where(has_left, xl_ref[s], jnp.zeros_like(xl_ref[s]))
        x_s[left_rows:left_rows + n_rows, sl] = x_ref[s]
        x_s[left_rows + n_rows:, sl] = jnp.where(has_right, xn_ref[s], jnp.zeros_like(xn_ref[s]))

    cw = par_ref[0:CONV_W, :]
    cb = par_ref[CONV_W:CONV_W + 1, :]
    ba = par_ref[CONV_W + 1:CONV_W + 2, :]
    bx = par_ref[CONV_W + 2:CONV_W + 3, :]
    lam = par_ref[CONV_W + 3:CONV_W + 4, :]
    softplus = jnp.maximum(-lam, 0.0) + jnp.log1p(jnp.exp(-jnp.abs(lam)))
    neg_c_sp = np.float32(-LRU_C) * softplus

    def chunk(i, h):
        c = (n_chunks - 1 - i) if reverse else i
        row0 = pl.multiple_of(c * chunk_rows, chunk_rows)
        xc = cb
        for j in range(CONV_W):
            xc = xc + x_s[pl.ds(row0 + j * SCAN_GROUP, chunk_rows), :] * cw[j:j + 1, :]
        pre = jnp.dot(xc.astype(BF16), w_ref[...], preferred_element_type=F32)
        r = jax.nn.sigmoid(pre[:, :LRU_CHUNK] + ba)
        gi = jax.nn.sigmoid(pre[:, LRU_CHUNK:] + bx)
        log_a = neg_c_sp * r
        a = jnp.exp(log_a)
        y = -jnp.tanh(log_a) * (a * a + 1.0)
        root = jnp.where(y > 0.0, y * lax.rsqrt(y), 0.0)
        u = root * (gi * xc)
        hs = [None] * LRU_TC
        order = range(LRU_TC - 1, -1, -1) if reverse else range(LRU_TC)
        for k in order:
            rows = slice(k * SCAN_GROUP, (k + 1) * SCAN_GROUP)
            h = a[rows] * h + u[rows]
            hs[k] = h
        hc = jnp.concatenate(hs, axis=0)
        for s in range(n_slabs):
            part = hc[:, s * LANES:(s + 1) * LANES]
            if reverse:
                part = part + acc_ref[s, pl.ds(row0, chunk_rows), :]
            o_ref[s, pl.ds(row0, chunk_rows), :] = part
        return h

    carry[...] = lax.fori_loop(0, n_chunks, chunk, carry[...], unroll=8)


def _lru(xr, conv_w, conv_b, w_gate, ba, bx, lam, seq, forward_out=None):
    reverse = forward_out is not None
    n_slabs, groups, _, _ = xr.shape
    c = n_slabs * LANES
    tt = _scan_time_block(seq)
    nt = seq // tt
    n_chunks = c // LRU_CHUNK
    slabs = LRU_CHUNK // LANES
    d = 1 if reverse else 0
    left_steps = CONV_LEFT
    right_steps = CONV_W - 1 - CONV_LEFT
    assert tt % left_steps == 0 and right_steps == 1

    def tblk(i):
        return (nt - 1 - i) if reverse else i

    cur = pl.BlockSpec((slabs, None, tt * SCAN_GROUP, LANES), lambda g, ch, i: (ch, g, tblk(i), 0))
    left = pl.BlockSpec((slabs, None, left_steps * SCAN_GROUP, LANES),
                        lambda g, ch, i: (ch, g, jnp.maximum(tblk(i) * (tt // left_steps) - 1, 0), 0))
    right = pl.BlockSpec((slabs, None, right_steps * SCAN_GROUP, LANES),
                         lambda g, ch, i: (ch, g, jnp.minimum((tblk(i) + 1) * tt, seq - 1), 0))
    params = jnp.concatenate([conv_w, conv_b[None], ba[d][None], bx[d][None], lam[d][None]], axis=0)
    assert params.shape == (CONV_W + 4, c)
    return pl.pallas_call(
        functools.partial(_lru_kernel, nt=nt, reverse=reverse),
        grid=(groups, n_chunks, nt),
        in_specs=[left, cur, right, pl.BlockSpec((CONV_W + 4, LRU_CHUNK), lambda g, ch, i: (0, ch)),
                  pl.BlockSpec((None, None, LRU_CHUNK, 2 * LRU_CHUNK), lambda g, ch, i: (d, ch, 0, 0))]
                 + ([cur] if reverse else []),
        out_specs=cur,
        out_shape=jax.ShapeDtypeStruct(xr.shape, F32),
        input_output_aliases={5: 0} if reverse else {},
        scratch_shapes=[pltpu.VMEM(((tt + CONV_W - 1) * SCAN_GROUP, LRU_CHUNK), F32),
                        pltpu.VMEM((SCAN_GROUP, LRU_CHUNK), F32)],
        compiler_params=_params(("parallel", "parallel", "arbitrary")),
        name="lru_bwd" if reverse else "lru_fwd",
    )(xr, xr, xr, params, w_gate, *([forward_out] if reverse else []))


def _lru_gate_weights(wa, wx):
    per = LRU_CHUNK // LRU_BLOCK_DIM

    def dense(w):
        n_dir, n_blk, bd, _ = w.shape
        w = w.reshape(n_dir, n_blk // per, per, bd, bd)
        eye = jnp.eye(per, dtype=w.dtype)
        full = jnp.einsum('dcpij,pq->dcpiqj', w, eye)
        return full.reshape(n_dir, n_blk // per, LRU_CHUNK, LRU_CHUNK)

    return jnp.concatenate([dense(wa), dense(wx)], axis=-1).astype(BF16)


def _mixout_kernel(h_ref, na_ref, yr_ref, gna_ref, glru_ref, hr_ref, wn_ref, wl_ref, wo_ref, o_ref):
    nseq, tq, d = h_ref.shape
    rows = lambda ref: ref[...].reshape(nseq * tq, d)

    na = jnp.concatenate(
        [jnp.concatenate([na_ref[p, j] for j in range(nseq)], axis=0) for p in range(N_PAIRS)], axis=-1)
    na_p = jnp.dot(na, wn_ref[...], preferred_element_type=F32)
    hr = jnp.concatenate(
        [jnp.concatenate([hr_ref[s, pl.ds(j, tq, stride=nseq), :] for j in range(nseq)], axis=0)
         for s in range(d // LANES)], axis=-1)
    lru_in = (_gelu_tanh(rows(yr_ref).astype(F32)) * hr).astype(BF16)
    lru_p = jnp.dot(lru_in, wl_ref[...], preferred_element_type=F32)
    merged = (jax.nn.sigmoid(rows(gna_ref).astype(F32)) * na_p
              + jax.nn.sigmoid(rows(glru_ref).astype(F32)) * lru_p)
    out = rows(h_ref) + jnp.dot(merged.astype(BF16), wo_ref[...], preferred_element_type=F32)
    o_ref[...] = out.reshape(nseq, tq, d)


def _mixout(h, na, yr, gna, glru, hr, wn, wl, wo):
    n, d = h.shape
    _, batch, seq, _ = na.shape
    groups = batch // SCAN_GROUP
    tq = _seq_tile(seq)
    n_slabs = d // LANES
    tile = pl.BlockSpec((SCAN_GROUP, tq, d), lambda g_, i: (g_, i, 0))
    pair = pl.BlockSpec((N_PAIRS, SCAN_GROUP, tq, LANES), lambda g_, i: (0, g_, i, 0))
    slab = pl.BlockSpec((n_slabs, None, tq * SCAN_GROUP, LANES), lambda g_, i: (0, g_, i, 0))
    out = pl.pallas_call(
        _mixout_kernel,
        grid=(groups, seq // tq),
        in_specs=[tile, pair, tile, tile, tile, slab,
                  _const_spec((d, d)), _const_spec((d, d)), _const_spec((d, d))],
        out_specs=tile,
        out_shape=jax.ShapeDtypeStruct((batch, seq, d), F32),
        input_output_aliases={0: 0},
        compiler_params=_params(("parallel", "parallel")),
        name="mixout",
    )(h.reshape(batch, seq, d), na, yr, gna, glru, hr, wn, wl, wo)
    return out.reshape(n, d)


def _prepare_layer(l, p, rows):
    d = D_MODEL
    row = lambda a: a[l].reshape(1, d).astype(F32)
    return dict(
        norm_ffn1=row(p["norm_ffn1"]), norm_mix=row(p["norm_mix"]), norm_ffn2=row(p["norm_ffn2"]),
        ffn1=(p["ffn1_w_gate"][l].astype(BF16), p["ffn1_w_up"][l].astype(BF16), p["ffn1_w_down"][l].astype(BF16)),
        ffn2=(p["ffn2_w_gate"][l].astype(BF16), p["ffn2_w_up"][l].astype(BF16), p["ffn2_w_down"][l].astype(BF16)),
        w_in=p["w_in"][l].astype(BF16),
        bias_tbl=_attn_bias_table(p["na_rel_bias"][l], rows),
        conv_w=p["conv_w"][l].astype(F32), conv_b=p["conv_b"][l].astype(F32),
        w_gate=_lru_gate_weights(p["lru_wa"][l], p["lru_wx"][l]),
        ba=p["lru_ba"][l].astype(F32), bx=p["lru_bx"][l].astype(F32), lam=p["lru_lambda"][l].astype(F32),
        w_na_proj=p["w_na_proj"][l].astype(BF16), w_lru_proj=p["w_lru_proj"][l].astype(BF16),
        w_out=p["w_out"][l].astype(BF16),
    )


def _encode(x, meta_tokens, layers, final_norm):
    batch, t, d = x.shape
    seq = N_META + t
    assert batch % SCAN_GROUP == 0 and seq % N_META == 0
    groups = batch // SCAN_GROUP
    gf = final_norm.reshape(1, d).astype(F32)
    h = None
    for li, lp in enumerate(layers):
        if li == 0:
            h = _ffn_embed(x, meta_tokens, lp["norm_ffn1"], *lp["ffn1"])
        else:
            h = _ffn(h, lp["norm_ffn1"], *lp["ffn1"])
        q, k, v, xr, yr, gna, glru = _inproj(h, lp["norm_mix"], lp["w_in"], groups, seq)
        na = _attention(q, k, v, lp["bias_tbl"])
        lru_args = (xr, lp["conv_w"], lp["conv_b"], lp["w_gate"], lp["ba"], lp["bx"], lp["lam"], seq)
        hr = _lru(*lru_args, forward_out=_lru(*lru_args))
        h = _mixout(h, na, yr, gna, glru, hr, lp["w_na_proj"], lp["w_lru_proj"], lp["w_out"])
        if li < len(layers) - 1:
            h = _ffn(h, lp["norm_ffn2"], *lp["ffn2"])
    return _ffn_final(h, layers[-1]["norm_ffn2"], *layers[-1]["ffn2"], gf, batch, t)


def kernel(x_prompt, x_sample, meta_tokens, norm_ffn1, ffn1_w_gate, ffn1_w_up, ffn1_w_down, norm_mix, w_in, na_rel_bias, conv_w, conv_b, lru_wa, lru_ba, lru_wx, lru_bx, lru_lambda, w_na_proj, w_lru_proj, w_out, norm_ffn2, ffn2_w_gate, ffn2_w_up, ffn2_w_down, final_norm):
    p = dict(norm_ffn1=norm_ffn1, ffn1_w_gate=ffn1_w_gate, ffn1_w_up=ffn1_w_up, ffn1_w_down=ffn1_w_down,
             norm_mix=norm_mix, w_in=w_in, na_rel_bias=na_rel_bias, conv_w=conv_w, conv_b=conv_b,
             lru_wa=lru_wa, lru_ba=lru_ba, lru_wx=lru_wx, lru_bx=lru_bx, lru_lambda=lru_lambda,
             w_na_proj=w_na_proj, w_lru_proj=w_lru_proj, w_out=w_out, norm_ffn2=norm_ffn2,
             ffn2_w_gate=ffn2_w_gate, ffn2_w_up=ffn2_w_up, ffn2_w_down=ffn2_w_down)
    rows = x_prompt.shape[1] // GRID_W
    assert _attn_block_cases(rows) == _attn_block_cases(x_sample.shape[1] // GRID_W)
    layers = [_prepare_layer(l, p, rows) for l in range(norm_ffn1.shape[0])]
    y_prompt = _encode(x_prompt, meta_tokens, layers, final_norm)
    y_sample = _encode(x_sample, meta_tokens, layers, final_norm)
    return (y_prompt, y_sample)
```

```python
import functools

import numpy as np
import jax
import jax.numpy as jnp
from jax import lax
from jax.experimental import pallas as pl
from jax.experimental.pallas import tpu as pltpu

F32 = jnp.float32
BF16 = jnp.bfloat16

D_MODEL = 1024
N_META = 16
GRID_W = 64
NA_HEADS = 16
NA_HEAD_DIM = 64
NA_KH = 8
NA_KW = 16
LRU_BLOCK_DIM = 64
CONV_W = 4
CONV_LEFT = 2
LRU_C = 8.0
RMS_EPS = 1e-6
MASK_VALUE = -1e30

LANES = 128
SUBLANES = 8
BF16_ROWS = 16
MXU_TILE = 256
VMEM_LIMIT = 56 * 1024 * 1024
MAX_ROW_TILE = 704
N_PAIRS = NA_HEADS // 2
ATTN_QROWS = 4
ATTN_KROWS = ATTN_QROWS + NA_KH - 1
ATTN_KLOC = ATTN_KROWS * GRID_W
ATTN_KEXT = -(-(ATTN_KLOC + N_META) // MXU_TILE) * MXU_TILE
ATTN_BLOCKS_PER_STEP = 64
LRU_CHUNK = MXU_TILE
SCAN_GROUP = SUBLANES
LRU_TC = 16
LRU_MAX_TT = 704


def _row_tile(n):
    for t in range(MAX_ROW_TILE, 0, -BF16_ROWS):
        if n % t == 0:
            return t
    raise ValueError(f"no row tile for {n} rows")


def _const_spec(shape):
    zeros = (0,) * len(shape)
    return pl.BlockSpec(shape, lambda *_: zeros, pipeline_mode=pl.Buffered(1))


def _params(sem):
    return pltpu.CompilerParams(dimension_semantics=sem, vmem_limit_bytes=VMEM_LIMIT)


def _rms(x, g):
    inv = lax.rsqrt(jnp.mean(x * x, axis=-1, keepdims=True) + RMS_EPS)
    return (x * inv) * g


def _gelu_tanh(x):
    c = np.float32(np.sqrt(2.0 / np.pi))
    return x * (0.5 * (1.0 + jnp.tanh(c * (x + np.float32(0.044715) * (x * x * x)))))


def _ffn_kernel(h_ref, g_ref, wg_ref, wu_ref, wd_ref, gf_ref, o_ref, *, final):
    x = h_ref[...]
    xn = _rms(x, g_ref[...]).astype(BF16)
    gate = jnp.dot(xn, wg_ref[...], preferred_element_type=F32)
    up = jnp.dot(xn, wu_ref[...], preferred_element_type=F32)
    act = (gate * jax.nn.sigmoid(gate) * up).astype(BF16)
    down = jnp.dot(act, wd_ref[...], preferred_element_type=F32)
    y = x + 0.5 * down
    o_ref[...] = _rms(y, gf_ref[...]) if final else y


def _ffn_specs(d, dff):
    return [_const_spec((1, d)), _const_spec((d, dff)), _const_spec((d, dff)), _const_spec((dff, d)),
            _const_spec((1, d))]


def _ffn(h, g, wg, wu, wd):
    n, d = h.shape
    tm = _row_tile(n)
    row = pl.BlockSpec((tm, d), lambda i: (i, 0))
    return pl.pallas_call(
        functools.partial(_ffn_kernel, final=False),
        grid=(n // tm,),
        in_specs=[row] + _ffn_specs(d, wg.shape[1]),
        out_specs=row,
        out_shape=jax.ShapeDtypeStruct((n, d), F32),
        input_output_aliases={0: 0},
        compiler_params=_params(("parallel",)),
        name="ffn",
    )(h, g, wg, wu, wd, g)


def _grid_rows_spec(t, seq, tm, d):
    assert seq % N_META == 0 and tm % N_META == 0
    return pl.BlockSpec((pl.Element(tm), pl.Element(d)),
                        lambda b, j: (pl.multiple_of(b * seq + N_META + j * tm, N_META), 0))


def _ffn_embed_kernel(x_ref, m_ref, g_ref, wg_ref, wu_ref, wd_ref, gf_ref, o_ref, xin_ref):
    tm = x_ref.shape[0]

    @pl.when(pl.program_id(1) == 0)
    def _():
        xin_ref[0:N_META, :] = m_ref[...]
        xin_ref[N_META:, :] = x_ref[0:tm - N_META, :]
        _ffn_kernel(xin_ref, g_ref, wg_ref, wu_ref, wd_ref, gf_ref, o_ref, final=False)

    @pl.when(pl.program_id(1) > 0)
    def _():
        _ffn_kernel(x_ref, g_ref, wg_ref, wu_ref, wd_ref, gf_ref, o_ref, final=False)


def _ffn_embed(x, meta_tokens, g, wg, wu, wd):
    batch, t, d = x.shape
    seq = N_META + t
    tm = _row_tile(seq)
    x_tile = pl.BlockSpec(
        (pl.Element(tm), pl.Element(d)),
        lambda b, j: (pl.multiple_of(b * t + jnp.maximum(j * tm - N_META, 0), N_META), 0))
    return pl.pallas_call(
        _ffn_embed_kernel,
        grid=(batch, seq // tm),
        in_specs=[x_tile, _const_spec((N_META, d))] + _ffn_specs(d, wg.shape[1]),
        out_specs=pl.BlockSpec((tm, d), lambda b, j: (b * (seq // tm) + j, 0)),
        out_shape=jax.ShapeDtypeStruct((batch * seq, d), F32),
        scratch_shapes=[pltpu.VMEM((tm, d), F32)],
        compiler_params=_params(("parallel", "parallel")),
        name="ffn_embed",
    )(x.reshape(batch * t, d), meta_tokens.astype(F32), g, wg, wu, wd, g)


def _ffn_final(h, g, wg, wu, wd, final_g, batch, t):
    n, d = h.shape
    seq = n // batch
    tm = _row_tile(t)
    out = pl.pallas_call(
        functools.partial(_ffn_kernel, final=True),
        grid=(batch, t // tm),
        in_specs=[_grid_rows_spec(t, seq, tm, d)] + _ffn_specs(d, wg.shape[1]),
        out_specs=pl.BlockSpec((tm, d), lambda b, j: (b * (t // tm) + j, 0)),
        out_shape=jax.ShapeDtypeStruct((batch * t, d), F32),
        compiler_params=_params(("parallel", "parallel")),
        name="ffn_final",
    )(h, g, wg, wu, wd, final_g)
    return out.reshape(batch, t, d)


def _seq_tile(seq):
    best = None
    for tq in range(BF16_ROWS, MAX_ROW_TILE // SCAN_GROUP + 1, BF16_ROWS):
        if seq % tq == 0:
            best = tq
    assert best is not None, seq
    return best


def _inproj_kernel(h_ref, g_ref, w_ref, q_ref, k_ref, v_ref, xr_ref, yr_ref, gna_ref, glru_ref):
    nseq, tq, d = h_ref.shape
    xn = _rms(h_ref[...].reshape(nseq * tq, d), g_ref[...]).astype(BF16)

    def proj(j):
        return jnp.dot(xn, w_ref[:, j * d:(j + 1) * d], preferred_element_type=F32)

    zq = (proj(0) * np.float32(NA_HEAD_DIM ** -0.5)).astype(BF16)
    zk = proj(1).astype(BF16)
    zv = proj(2).astype(BF16)
    for p in range(N_PAIRS):
        sl = slice(p * LANES, (p + 1) * LANES)
        for j in range(nseq):
            rows = slice(j * tq, (j + 1) * tq)
            q_ref[p, j] = zq[rows, sl]
            k_ref[p, j] = zk[rows, sl]
            v_ref[p, j] = zv[rows, sl]
    xr = proj(3)
    for s in range(d // LANES):
        for j in range(nseq):
            xr_ref[s, pl.ds(j, tq, stride=nseq), :] = xr[j * tq:(j + 1) * tq, s * LANES:(s + 1) * LANES]
    yr_ref[...] = proj(4).reshape(nseq, tq, d).astype(BF16)
    gna_ref[...] = proj(5).reshape(nseq, tq, d).astype(BF16)
    glru_ref[...] = proj(6).reshape(nseq, tq, d).astype(BF16)


def _inproj(h, g, w_in, groups, seq):
    n, d = h.shape
    batch = groups * SCAN_GROUP
    tq = _seq_tile(seq)
    n_slabs = d // LANES
    tile = pl.BlockSpec((SCAN_GROUP, tq, d), lambda g_, i: (g_, i, 0))
    pair = pl.BlockSpec((N_PAIRS, SCAN_GROUP, tq, LANES), lambda g_, i: (0, g_, i, 0))
    slab = pl.BlockSpec((n_slabs, None, tq * SCAN_GROUP, LANES), lambda g_, i: (0, g_, i, 0))
    pair_shape = jax.ShapeDtypeStruct((N_PAIRS, batch, seq, LANES), BF16)
    tile_shape = jax.ShapeDtypeStruct((batch, seq, d), BF16)
    slab_shape = jax.ShapeDtypeStruct((n_slabs, groups, seq * SCAN_GROUP, LANES), F32)
    return pl.pallas_call(
        _inproj_kernel,
        grid=(groups, seq // tq),
        in_specs=[tile, _const_spec((1, d)), _const_spec(w_in.shape)],
        out_specs=[pair, pair, pair, slab, tile, tile, tile],
        out_shape=[pair_shape, pair_shape, pair_shape, slab_shape, tile_shape, tile_shape, tile_shape],
        compiler_params=_params(("parallel", "parallel")),
        name="inproj",
    )(h.reshape(batch, seq, d), g, w_in)


def _attn_block_cases(rows):
    nb = rows // ATTN_QROWS
    assert rows % ATTN_QROWS == 0 and nb >= 3 and nb % 2 == 0, rows

    def case(m):
        ks = min(max(ATTN_QROWS * m - NA_KH // 2, 0), rows - ATTN_KROWS)
        out = []
        for q in range(ATTN_QROWS):
            r = ATTN_QROWS * m + q
            rs = min(max(r - NA_KH // 2, 0), rows - NA_KH)
            out.append((rs - ks, r - rs))
        return tuple(out)

    assert all(case(m) == case(1) for m in range(1, nb - 1))
    return case(0), case(1), case(nb - 1)


def _attn_bias_table(rel_bias, rows):
    c = np.arange(GRID_W)
    cs = np.clip(c - NA_KW // 2, 0, GRID_W - NA_KW)
    kc = np.arange(GRID_W)
    ok = (kc[None, :] >= cs[:, None]) & (kc[None, :] < cs[:, None] + NA_KW)
    dcol = kc[None, :] - c[:, None] + NA_KW - 1
    onehot = (dcol[:, :, None] == np.arange(2 * NA_KW - 1)).astype(np.float32)
    cols = jnp.einsum('hdj,ckj->hdck', rel_bias.astype(F32), onehot, precision=lax.Precision.HIGHEST)
    cols = jnp.where(ok[None, None], cols, np.float32(MASK_VALUE))
    masked = jnp.full((NA_HEADS, GRID_W, GRID_W), np.float32(MASK_VALUE))
    tail = jnp.concatenate([jnp.zeros((NA_HEADS, GRID_W, N_META), F32),
                            jnp.full((NA_HEADS, GRID_W, ATTN_KEXT - ATTN_KLOC - N_META), np.float32(MASK_VALUE))],
                           axis=-1)
    per_case = []
    for cases in _attn_block_cases(rows):
        for off, e in cases:
            pieces = [cols[:, i - off + NA_KH - 1 - e] if 0 <= i - off < NA_KH else masked
                      for i in range(ATTN_KROWS)]
            per_case.append(jnp.concatenate(pieces + [tail], axis=-1))
    tbl = jnp.stack(per_case, axis=1)
    tbl = tbl.reshape(N_PAIRS, 2, 3, ATTN_QROWS, GRID_W, ATTN_KEXT)
    return jnp.transpose(tbl, (0, 2, 1, 3, 4, 5)).reshape(N_PAIRS, 3, 2 * ATTN_QROWS * GRID_W, ATTN_KEXT)


def _attn_kernel(q_ref, k_ref, v_ref, bias_ref, o_ref, kx_ref, vx_ref, s_ref, p_ref, den_ref):
    n_seq, seq, _ = q_ref.shape
    rows = (seq - N_META) // GRID_W
    nb = rows // ATTN_QROWS
    n_blocks = n_seq * nb
    nq = ATTN_QROWS * GRID_W
    nt = (((1,), (1,)), ((), ()))
    lo = lax.broadcasted_iota(jnp.int32, (1, LANES), 1) < NA_HEAD_DIM

    def stack_heads(x):
        zero = jnp.zeros_like(x)
        return jnp.concatenate([jnp.where(lo, x, zero), jnp.where(lo, zero, x)], axis=0)

    def pick_heads(o, n):
        return jnp.where(lo, o[:n], o[n:])

    for sq in range(n_seq):
        km = k_ref[sq, 0:N_META, :]
        vm = v_ref[sq, 0:N_META, :]
        qm = stack_heads(q_ref[sq, 0:N_META, :])
        sm = lax.dot_general(qm, km, nt, preferred_element_type=F32)
        em = jnp.exp(sm - jnp.max(sm, axis=-1, keepdims=True))
        om = jnp.dot(em.astype(BF16), vm, preferred_element_type=F32)
        om = om / jnp.sum(em, axis=-1, keepdims=True)
        o_ref[sq, 0:N_META, :] = pick_heads(om, N_META).astype(o_ref.dtype)

    pad = jnp.zeros((ATTN_KEXT - ATTN_KLOC - N_META, LANES), BF16)
    for j in range(2):
        kx_ref[j, ATTN_KLOC + N_META:, :] = pad
        vx_ref[j, ATTN_KLOC + N_META:, :] = pad

    def locate(m):
        sq = m // nb
        local = m - sq * nb
        ks = jnp.clip(ATTN_QROWS * local - NA_KH // 2, 0, rows - ATTN_KROWS)
        q0 = pl.multiple_of(N_META + local * nq, BF16_ROWS)
        k0 = pl.multiple_of(N_META + ks * GRID_W, BF16_ROWS)
        return sq, local, q0, k0

    def scores(m, j):
        sq, local, q0, k0 = locate(m)
        kx_ref[j, 0:ATTN_KLOC, :] = k_ref[sq, pl.ds(k0, ATTN_KLOC), :]
        kx_ref[j, ATTN_KLOC:ATTN_KLOC + N_META, :] = k_ref[sq, 0:N_META, :]
        qs = stack_heads(q_ref[sq, pl.ds(q0, nq), :])
        kind = jnp.where(local == 0, 0, jnp.where(local == nb - 1, 2, 1))
        s_ref[j] = lax.dot_general(qs, kx_ref[j], nt, preferred_element_type=F32) + bias_ref[kind]

    def softmax(j):
        s = s_ref[j]
        e = jnp.exp(s - jnp.max(s, axis=-1, keepdims=True))
        den_ref[j] = jnp.sum(e, axis=-1, keepdims=True)
        p_ref[j] = e.astype(BF16)

    def values(m, j):
        sq, _, q0, k0 = locate(m)
        vx_ref[j, 0:ATTN_KLOC, :] = v_ref[sq, pl.ds(k0, ATTN_KLOC), :]
        vx_ref[j, ATTN_KLOC:ATTN_KLOC + N_META, :] = v_ref[sq, 0:N_META, :]
        o = jnp.dot(p_ref[j], vx_ref[j], preferred_element_type=F32) / den_ref[j]
        o_ref[sq, pl.ds(q0, nq), :] = pick_heads(o, nq).astype(o_ref.dtype)

    scores(0, 0)
    scores(1, 1)
    softmax(0)

    def two_blocks(i, carry):
        m = 2 * i + 2
        scores(m, 0)
        softmax(1)
        values(m - 2, 0)
        scores(m + 1, 1)
        softmax(0)
        values(m - 1, 1)
        return carry

    lax.fori_loop(0, n_blocks // 2 - 1, two_blocks, 0)
    softmax(1)
    values(n_blocks - 2, 0)
    values(n_blocks - 1, 1)


def _attention(q, k, v, bias_tbl):
    _, batch, seq, _ = q.shape
    blocks_per_seq = (seq - N_META) // (GRID_W * ATTN_QROWS)
    n_seq = max(1, min(batch, ATTN_BLOCKS_PER_STEP // blocks_per_seq))
    assert batch % n_seq == 0
    blk = pl.BlockSpec((None, n_seq, seq, LANES), lambda p, b: (p, b, 0, 0))
    bias_spec = pl.BlockSpec((None,) + bias_tbl.shape[1:], lambda p, b: (p, 0, 0, 0))
    out = pl.pallas_call(
        _attn_kernel,
        grid=(N_PAIRS, batch // n_seq),
        in_specs=[blk, blk, blk, bias_spec],
        out_specs=blk,
        out_shape=jax.ShapeDtypeStruct(q.shape, BF16),
        scratch_shapes=[pltpu.VMEM((2, ATTN_KEXT, LANES), BF16), pltpu.VMEM((2, ATTN_KEXT, LANES), BF16),
                        pltpu.VMEM((2, 2 * ATTN_QROWS * GRID_W, ATTN_KEXT), F32),
                        pltpu.VMEM((2, 2 * ATTN_QROWS * GRID_W, ATTN_KEXT), BF16),
                        pltpu.VMEM((2, 2 * ATTN_QROWS * GRID_W, 1), F32)],
        compiler_params=_params(("parallel", "parallel")),
        name="attention",
    )(q, k, v, bias_tbl)
    return out


def _scan_time_block(seq):
    best = None
    for tt in range(LRU_TC, LRU_MAX_TT + 1, LRU_TC):
        if seq % tt == 0:
            best = tt
    assert best is not None, seq
    return best


def _lru_kernel(xl_ref, x_ref, xn_ref, par_ref, w_ref, *rest, nt, reverse):
    acc_ref, o_ref, x_s, carry = rest if reverse else (None,) + rest
    step = pl.program_id(2)
    n_slabs, n_rows, _ = x_ref.shape
    blk = (nt - 1 - step) if reverse else step
    has_left = blk > 0
    has_right = blk < nt - 1
    left_rows = xl_ref.shape[1]
    chunk_rows = LRU_TC * SCAN_GROUP
    n_chunks = n_rows // chunk_rows

    @pl.when(step == 0)
    def _():
        carry[...] = jnp.zeros_like(carry)

    for s in range(n_slabs):
        sl = slice(s * LANES, (s + 1) * LANES)
        x_s[0:left_rows, sl] = jnp.where(has_left, xl_ref[s], jnp.zeros_like(xl_ref[s]))
        x_s[left_rows:left_rows + n_rows, sl] = x_ref[s]
        x_s[left_rows + n_rows:, sl] = jnp.where(has_right, xn_ref[s], jnp.zeros_like(xn_ref[s]))

    cw = par_ref[0:CONV_W, :]
    cb = par_ref[CONV_W:CONV_W + 1, :]
    ba = par_ref[CONV_W + 1:CONV_W + 2, :]
    bx = par_ref[CONV_W + 2:CONV_W + 3, :]
    lam = par_ref[CONV_W + 3:CONV_W + 4, :]
    softplus = jnp.maximum(-lam, 0.0) + jnp.log1p(jnp.exp(-jnp.abs(lam)))
    neg_c_sp = np.float32(-LRU_C) * softplus

    def chunk(i, h):
        c = (n_chunks - 1 - i) if reverse else i
        row0 = pl.multiple_of(c * chunk_rows, chunk_rows)
        xc = cb
        for j in range(CONV_W):
            xc = xc + x_s[pl.ds(row0 + j * SCAN_GROUP, chunk_rows), :] * cw[j:j + 1, :]
        pre = jnp.dot(xc.astype(BF16), w_ref[...], preferred_element_type=F32)
        r = jax.nn.sigmoid(pre[:, :LRU_CHUNK] + ba)
        gi = jax.nn.sigmoid(pre[:, LRU_CHUNK:] + bx)
        log_a = neg_c_sp * r
        a = jnp.exp(log_a)
        y = -jnp.tanh(log_a) * (a * a + 1.0)
        root = jnp.where(y > 0.0, y * lax.rsqrt(y), 0.0)
        u = root * (gi * xc)
        hs = [None] * LRU_TC
        order = range(LRU_TC - 1, -1, -1) if reverse else range(LRU_TC)
        for k in order:
            rows = slice(k * SCAN_GROUP, (k + 1) * SCAN_GROUP)
            h = a[rows] * h + u[rows]
            hs[k] = h
        hc = jnp.concatenate(hs, axis=0)
        for s in range(n_slabs):
            part = hc[:, s * LANES:(s + 1) * LANES]
            if reverse:
                part = part + acc_ref[s, pl.ds(row0, chunk_rows), :]
            o_ref[s, pl.ds(row0, chunk_rows), :] = part
        return h

    carry[...] = lax.fori_loop(0, n_chunks, chunk, carry[...], unroll=16)


def _lru(xr, conv_w, conv_b, w_gate, ba, bx, lam, seq, forward_out=None):
    reverse = forward_out is not None
    n_slabs, groups, _, _ = xr.shape
    c = n_slabs * LANES
    tt = _scan_time_block(seq)
    nt = seq // tt
    n_chunks = c // LRU_CHUNK
    slabs = LRU_CHUNK // LANES
    d = 1 if reverse else 0
    left_steps = CONV_LEFT
    right_steps = CONV_W - 1 - CONV_LEFT
    assert tt % left_steps == 0 and right_steps == 1

    def tblk(i):
        return (nt - 1 - i) if reverse else i

    cur = pl.BlockSpec((slabs, None, tt * SCAN_GROUP, LANES), lambda g, ch, i: (ch, g, tblk(i), 0))
    left = pl.BlockSpec((slabs, None, left_steps * SCAN_GROUP, LANES),
                        lambda g, ch, i: (ch, g, jnp.maximum(tblk(i) * (tt // left_steps) - 1, 0), 0))
    right = pl.BlockSpec((slabs, None, right_steps * SCAN_GROUP, LANES),
                         lambda g, ch, i: (ch, g, jnp.minimum((tblk(i) + 1) * tt, seq - 1), 0))
    params = jnp.concatenate([conv_w, conv_b[None], ba[d][None], bx[d][None], lam[d][None]], axis=0)
    assert params.shape == (CONV_W + 4, c)
    return pl.pallas_call(
        functools.partial(_lru_kernel, nt=nt, reverse=reverse),
        grid=(groups, n_chunks, nt),
        in_specs=[left, cur, right, pl.BlockSpec((CONV_W + 4, LRU_CHUNK), lambda g, ch, i: (0, ch)),
                  pl.BlockSpec((None, None, LRU_CHUNK, 2 * LRU_CHUNK), lambda g, ch, i: (d, ch, 0, 0))]
                 + ([cur] if reverse else []),
        out_specs=cur,
        out_shape=jax.ShapeDtypeStruct(xr.shape, F32),
        input_output_aliases={5: 0} if reverse else {},
        scratch_shapes=[pltpu.VMEM(((tt + CONV_W - 1) * SCAN_GROUP, LRU_CHUNK), F32),
                        pltpu.VMEM((SCAN_GROUP, LRU_CHUNK), F32)],
        compiler_params=_params(("parallel", "parallel", "arbitrary")),
        name="lru_bwd" if reverse else "lru_fwd",
    )(xr, xr, xr, params, w_gate, *([forward_out] if reverse else []))


def _lru_gate_weights(wa, wx):
    per = LRU_CHUNK // LRU_BLOCK_DIM

    def dense(w):
        n_dir, n_blk, bd, _ = w.shape
        w = w.reshape(n_dir, n_blk // per, per, bd, bd)
        eye = jnp.eye(per, dtype=w.dtype)
        full = jnp.einsum('dcpij,pq->dcpiqj', w, eye)
        return full.reshape(n_dir, n_blk // per, LRU_CHUNK, LRU_CHUNK)

    return jnp.concatenate([dense(wa), dense(wx)], axis=-1).astype(BF16)


def _mixout_kernel(h_ref, na_ref, yr_ref, gna_ref, glru_ref, hr_ref, wn_ref, wl_ref, wo_ref, o_ref):
    nseq, tq, d = h_ref.shape
    rows = lambda ref: ref[...].reshape(nseq * tq, d)

    na = jnp.concatenate(
        [jnp.concatenate([na_ref[p, j] for j in range(nseq)], axis=0) for p in range(N_PAIRS)], axis=-1)
    na_p = jnp.dot(na, wn_ref[...], preferred_element_type=F32)
    hr = jnp.concatenate(
        [jnp.concatenate([hr_ref[s, pl.ds(j, tq, stride=nseq), :] for j in range(nseq)], axis=0)
         for s in range(d // LANES)], axis=-1)
    lru_in = (_gelu_tanh(rows(yr_ref).astype(F32)) * hr).astype(BF16)
    lru_p = jnp.dot(lru_in, wl_ref[...], preferred_element_type=F32)
    merged = (jax.nn.sigmoid(rows(gna_ref).astype(F32)) * na_p
              + jax.nn.sigmoid(rows(glru_ref).astype(F32)) * lru_p)
    out = rows(h_ref) + jnp.dot(merged.astype(BF16), wo_ref[...], preferred_element_type=F32)
    o_ref[...] = out.reshape(nseq, tq, d)


def _mixout(h, na, yr, gna, glru, hr, wn, wl, wo):
    n, d = h.shape
    _, batch, seq, _ = na.shape
    groups = batch // SCAN_GROUP
    tq = _seq_tile(seq)
    n_slabs = d // LANES
    tile = pl.BlockSpec((SCAN_GROUP, tq, d), lambda g_, i: (g_, i, 0))
    pair = pl.BlockSpec((N_PAIRS, SCAN_GROUP, tq, LANES), lambda g_, i: (0, g_, i, 0))
    slab = pl.BlockSpec((n_slabs, None, tq * SCAN_GROUP, LANES), lambda g_, i: (0, g_, i, 0))
    out = pl.pallas_call(
        _mixout_kernel,
        grid=(groups, seq // tq),
        in_specs=[tile, pair, tile, tile, tile, slab,
                  _const_spec((d, d)), _const_spec((d, d)), _const_spec((d, d))],
        out_specs=tile,
        out_shape=jax.ShapeDtypeStruct((batch, seq, d), F32),
        input_output_aliases={0: 0},
        compiler_params=_params(("parallel", "parallel")),
        name="mixout",
    )(h.reshape(batch, seq, d), na, yr, gna, glru, hr, wn, wl, wo)
    return out.reshape(n, d)


def _prepare_layer(l, p, rows):
    d = D_MODEL
    row = lambda a: a[l].reshape(1, d).astype(F32)
    return dict(
        norm_ffn1=row(p["norm_ffn1"]), norm_mix=row(p["norm_mix"]), norm_ffn2=row(p["norm_ffn2"]),
        ffn1=(p["ffn1_w_gate"][l].astype(BF16), p["ffn1_w_up"][l].astype(BF16), p["ffn1_w_down"][l].astype(BF16)),
        ffn2=(p["ffn2_w_gate"][l].astype(BF16), p["ffn2_w_up"][l].astype(BF16), p["ffn2_w_down"][l].astype(BF16)),
        w_in=p["w_in"][l].astype(BF16),
        bias_tbl=_attn_bias_table(p["na_rel_bias"][l], rows),
        conv_w=p["conv_w"][l].astype(F32), conv_b=p["conv_b"][l].astype(F32),
        w_gate=_lru_gate_weights(p["lru_wa"][l], p["lru_wx"][l]),
        ba=p["lru_ba"][l].astype(F32), bx=p["lru_bx"][l].astype(F32), lam=p["lru_lambda"][l].astype(F32),
        w_na_proj=p["w_na_proj"][l].astype(BF16), w_lru_proj=p["w_lru_proj"][l].astype(BF16),
        w_out=p["w_out"][l].astype(BF16),
    )


def _encode(x, meta_tokens, layers, final_norm):
    batch, t, d = x.shape
    seq = N_META + t
    assert batch % SCAN_GROUP == 0 and seq % N_META == 0
    groups = batch // SCAN_GROUP
    gf = final_norm.reshape(1, d).astype(F32)
    h = None
    for li, lp in enumerate(layers):
        if li == 0:
            h = _ffn_embed(x, meta_tokens, lp["norm_ffn1"], *lp["ffn1"])
        else:
            h = _ffn(h, lp["norm_ffn1"], *lp["ffn1"])
        q, k, v, xr, yr, gna, glru = _inproj(h, lp["norm_mix"], lp["w_in"], groups, seq)
        na = _attention(q, k, v, lp["bias_tbl"])
        lru_args = (xr, lp["conv_w"], lp["conv_b"], lp["w_gate"], lp["ba"], lp["bx"], lp["lam"], seq)
        hr = _lru(*lru_args, forward_out=_lru(*lru_args))
        h = _mixout(h, na, yr, gna, glru, hr, lp["w_na_proj"], lp["w_lru_proj"], lp["w_out"])
        if li < len(layers) - 1:
            h = _ffn(h, lp["norm_ffn2"], *lp["ffn2"])
    return _ffn_final(h, layers[-1]["norm_ffn2"], *layers[-1]["ffn2"], gf, batch, t)


def kernel(x_prompt, x_sample, meta_tokens, norm_ffn1, ffn1_w_gate, ffn1_w_up, ffn1_w_down, norm_mix, w_in, na_rel_bias, conv_w, conv_b, lru_wa, lru_ba, lru_wx, lru_bx, lru_lambda, w_na_proj, w_lru_proj, w_out, norm_ffn2, ffn2_w_gate, ffn2_w_up, ffn2_w_down, final_norm):
    p = dict(norm_ffn1=norm_ffn1, ffn1_w_gate=ffn1_w_gate, ffn1_w_up=ffn1_w_up, ffn1_w_down=ffn1_w_down,
             norm_mix=norm_mix, w_in=w_in, na_rel_bias=na_rel_bias, conv_w=conv_w, conv_b=conv_b,
             lru_wa=lru_wa, lru_ba=lru_ba, lru_wx=lru_wx, lru_bx=lru_bx, lru_lambda=lru_lambda,
             w_na_proj=w_na_proj, w_lru_proj=w_lru_proj, w_out=w_out, norm_ffn2=norm_ffn2,
             ffn2_w_gate=ffn2_w_gate, ffn2_w_up=ffn2_w_up, ffn2_w_down=ffn2_w_down)
    rows = x_prompt.shape[1] // GRID_W
    assert _attn_block_cases(rows) == _attn_block_cases(x_sample.shape[1] // GRID_W)
    layers = [_prepare_layer(l, p, rows) for l in range(norm_ffn1.shape[0])]
    y_prompt = _encode(x_prompt, meta_tokens, layers, final_norm)
    y_sample = _encode(x_sample, meta_tokens, layers, final_norm)
    return (y_prompt, y_sample)
```
